```python
import jax
import jax.numpy as jnp
from jax import lax
import numpy as np

D_MODEL = 1024
BATCH = 2
SEQ = 8192
DEPTH = 1

GRID_W = 64
CTX_LEN = 256
N_HEADS = 8
QK_NOPE_DIM = 128
QK_ROPE_DIM = 64
QK_HEAD_DIM = QK_NOPE_DIM + QK_ROPE_DIM
V_HEAD_DIM = 128
Q_LORA_RANK = 384
KV_LORA_RANK = 256
CONV_DIM = D_MODEL
CONV_KSIZE = 3
D_FF = 256 * ((8 * D_MODEL + 3 * 256 - 1) // (3 * 256))
ROPE_AXIS_DIM = QK_ROPE_DIM // 2
ROPE_THETA = 10000.0
Q_BLOCK = 128
NORM_EPS = 1e-6
MOD_CHUNKS = 6
IN_SIZES = (CONV_DIM, CONV_DIM, CONV_DIM, Q_LORA_RANK, KV_LORA_RANK, QK_ROPE_DIM, D_MODEL, D_MODEL)
IN_SPLITS = tuple(int(v) for v in np.cumsum(IN_SIZES)[:-1])
D_IN = int(sum(IN_SIZES))

kernel_name = "hybrid_shortconv_mla_dit_block"


def rms_norm(t, g):
    tf = t.astype(jnp.float32)
    tf = tf * lax.rsqrt(jnp.mean(tf * tf, axis=-1, keepdims=True) + NORM_EPS)
    return (tf * g.astype(jnp.float32)).astype(t.dtype)


def modulate(t, shift, scale):
    return t * (1 + scale) + shift


def adaln_params(cond, w_mod, b_mod):
    return jnp.split(jax.nn.silu(cond) @ w_mod + b_mod, MOD_CHUNKS, axis=-1)


def axial_rope_tables(rows, dtype):
    row_pos, col_pos = jnp.meshgrid(jnp.arange(rows), jnp.arange(GRID_W), indexing="ij")
    half = ROPE_AXIS_DIM // 2
    freqs = ROPE_THETA ** (-jnp.arange(half, dtype=jnp.float32) / half)
    ang = jnp.concatenate([row_pos.reshape(-1, 1).astype(jnp.float32) * freqs,
                           col_pos.reshape(-1, 1).astype(jnp.float32) * freqs], axis=-1)
    return jnp.cos(ang).astype(dtype), jnp.sin(ang).astype(dtype)


def rope_2d(t, cos, sin):
    half = ROPE_AXIS_DIM // 2
    outs = []
    for a in range(2):
        seg = t[..., a * ROPE_AXIS_DIM:(a + 1) * ROPE_AXIS_DIM]
        ca = cos[:, a * half:(a + 1) * half]
        sa = sin[:, a * half:(a + 1) * half]
        x1, x2 = seg[..., :half], seg[..., half:]
        outs.append(x1 * ca - x2 * sa)
        outs.append(x1 * sa + x2 * ca)
    return jnp.concatenate(outs, axis=-1)


def rope_tail(t, rope):
    if rope is None:
        return t
    cos, sin = rope
    return jnp.concatenate([t[..., :QK_NOPE_DIM], rope_2d(t[..., QK_NOPE_DIM:], cos, sin)], axis=-1)


def depthwise_conv3(u, w, b):
    y = lax.conv_general_dilated(u, w[:, None, :], window_strides=(1,), padding="SAME",
                                 dimension_numbers=("NWC", "WIO", "NWC"),
                                 feature_group_count=u.shape[-1])
    return y + b


def short_conv_branch(bx, cx, xx, conv_w, conv_b, w_conv_out):
    return (bx * depthwise_conv3(cx * xx, conv_w, conv_b)) @ w_conv_out


def mla_queries(q_a, q_a_norm, w_q_b, q_norm, rope):
    b, s, _ = q_a.shape
    q = (rms_norm(q_a, q_a_norm) @ w_q_b).reshape(b, s, N_HEADS, QK_HEAD_DIM).transpose(0, 2, 1, 3)
    q = rms_norm(q, q_norm)
    return rope_tail(q, rope)


def mla_keys_values(kv_a, k_rope, kv_a_norm, w_kv_b, k_norm, rope):
    b, s, _ = kv_a.shape
    kv = (rms_norm(kv_a, kv_a_norm) @ w_kv_b).reshape(b, s, N_HEADS, QK_NOPE_DIM + V_HEAD_DIM)
    kv = kv.transpose(0, 2, 1, 3)
    k_nope, v = kv[..., :QK_NOPE_DIM], kv[..., QK_NOPE_DIM:]
    k_r = jnp.broadcast_to(k_rope[:, None], (b, N_HEADS, s, QK_ROPE_DIM))
    k = rms_norm(jnp.concatenate([k_nope, k_r], axis=-1), k_norm)
    return rope_tail(k, rope), v


def attend(q, k, v):
    s = jnp.einsum("bhqd,bhkd->bhqk", q, k).astype(jnp.float32) * (QK_HEAD_DIM ** -0.5)
    p = jax.nn.softmax(s, axis=-1).astype(v.dtype)
    return jnp.einsum("bhqk,bhkd->bhqd", p, v)


def latent_attention(q, k_lat, v_lat, k_ctx, v_ctx):
    b, h, s, dk = q.shape
    k_all = jnp.concatenate([k_ctx, k_lat], axis=2)
    v_all = jnp.concatenate([v_ctx, v_lat], axis=2)
    q_blocks = q.reshape(b, h, s // Q_BLOCK, Q_BLOCK, dk).transpose(2, 0, 1, 3, 4)
    out = lax.map(lambda qb: attend(qb, k_all, v_all), q_blocks)
    return out.transpose(1, 2, 0, 3, 4).reshape(b, h, s, V_HEAD_DIM)


def merge_heads(o):
    b, h, s, d = o.shape
    return o.transpose(0, 2, 1, 3).reshape(b, s, h * d)


def gated_merge(y_conv, attn, g_conv_pre, g_attn_pre, b_gate, w_attn_o, w_out):
    y_attn = merge_heads(attn) @ w_attn_o
    g_conv = jax.nn.sigmoid(g_conv_pre + b_gate[:D_MODEL])
    g_attn = jax.nn.sigmoid(g_attn_pre + b_gate[D_MODEL:])
    return (g_conv * y_conv + g_attn * y_attn) @ w_out


def swiglu(h, w_ffn_in, w_ffn_out):
    gate, up = jnp.split(h @ w_ffn_in, 2, axis=-1)
    return (jax.nn.silu(gate) * up) @ w_ffn_out


def setup_inputs(seed: int = 0) -> dict:
    key = jax.random.key(seed)
    ks = jax.random.split(key, 23)

    def normal(k, shape, scale):
        return jax.random.normal(k, shape, jnp.float32) * scale

    def gain(k, n):
        return 1.0 + normal(k, (DEPTH, n), 0.02)

    return {
        "x": normal(ks[0], (BATCH, SEQ, D_MODEL), 1.0),
        "c": normal(ks[1], (BATCH, D_MODEL), 1.0),
        "ctx": normal(ks[2], (BATCH, CTX_LEN, D_MODEL), 1.0),
        "c_ctx": normal(ks[3], (D_MODEL,), 1.0),
        "w_mod": normal(ks[4], (DEPTH, D_MODEL, MOD_CHUNKS * D_MODEL), 0.5 * D_MODEL ** -0.5),
        "b_mod": normal(ks[5], (DEPTH, MOD_CHUNKS * D_MODEL), 0.02),
        "norm_mix": gain(ks[6], D_MODEL),
        "norm_ffn": gain(ks[7], D_MODEL),
        "w_in": normal(ks[8], (DEPTH, D_MODEL, D_IN), D_MODEL ** -0.5),
        "b_gate": normal(ks[9], (DEPTH, 2 * D_MODEL), 0.02),
        "conv_w": normal(ks[10], (DEPTH, CONV_KSIZE, CONV_DIM), CONV_KSIZE ** -0.5),
        "conv_b": normal(ks[11], (DEPTH, CONV_DIM), 0.02),
        "w_conv_out": normal(ks[12], (DEPTH, CONV_DIM, D_MODEL), CONV_DIM ** -0.5),
        "q_a_norm": gain(ks[13], Q_LORA_RANK),
        "w_q_b": normal(ks[14], (DEPTH, Q_LORA_RANK, N_HEADS * QK_HEAD_DIM), Q_LORA_RANK ** -0.5),
        "kv_a_norm": gain(ks[15], KV_LORA_RANK),
        "w_kv_b": normal(ks[16], (DEPTH, KV_LORA_RANK, N_HEADS * (QK_NOPE_DIM + V_HEAD_DIM)), KV_LORA_RANK ** -0.5),
        "q_norm": gain(ks[17], QK_HEAD_DIM),
        "k_norm": gain(ks[18], QK_HEAD_DIM),
        "w_attn_o": normal(ks[19], (DEPTH, N_HEADS * V_HEAD_DIM, D_MODEL), (N_HEADS * V_HEAD_DIM) ** -0.5),
        "w_out": normal(ks[20], (DEPTH, D_MODEL, D_MODEL), D_MODEL ** -0.5),
        "w_ffn_in": normal(ks[21], (DEPTH, D_MODEL, 2 * D_FF), D_MODEL ** -0.5),
        "w_ffn_out": normal(ks[22], (DEPTH, D_FF, D_MODEL), D_FF ** -0.5),
    }


def reference(x, c, ctx, c_ctx, w_mod, b_mod, norm_mix, norm_ffn, w_in, b_gate, conv_w, conv_b,
              w_conv_out, q_a_norm, w_q_b, kv_a_norm, w_kv_b, q_norm, k_norm, w_attn_o, w_out,
              w_ffn_in, w_ffn_out):
    rows = x.shape[1] // GRID_W
    rope = axial_rope_tables(rows, x.dtype)
    for l in range(DEPTH):
        last = l == DEPTH - 1
        sh1, sc1, g1, sh2, sc2, g2 = [m[:, None, :] for m in adaln_params(c, w_mod[l], b_mod[l])]
        csh1, csc1, cg1, csh2, csc2, cg2 = adaln_params(c_ctx, w_mod[l], b_mod[l])

        hc = modulate(rms_norm(ctx, norm_mix[l]), csh1, csc1)
        cbx, ccx, cxx, cq_a, ckv_a, ck_rope, cgc, cga = jnp.split(hc @ w_in[l], IN_SPLITS, axis=-1)
        k_ctx, v_ctx = mla_keys_values(ckv_a, ck_rope, kv_a_norm[l], w_kv_b[l], k_norm[l], None)

        hx = modulate(rms_norm(x, norm_mix[l]), sh1, sc1)
        bx, cx, xx, q_a, kv_a, k_rope, gc, ga = jnp.split(hx @ w_in[l], IN_SPLITS, axis=-1)
        y_conv = short_conv_branch(bx, cx, xx, conv_w[l], conv_b[l], w_conv_out[l])
        q_lat = mla_queries(q_a, q_a_norm[l], w_q_b[l], q_norm[l], rope)
        k_lat, v_lat = mla_keys_values(kv_a, k_rope, kv_a_norm[l], w_kv_b[l], k_norm[l], rope)
        attn = latent_attention(q_lat, k_lat, v_lat, k_ctx, v_ctx)
        x_mid = x + g1 * gated_merge(y_conv, attn, gc, ga, b_gate[l], w_attn_o[l], w_out[l])

        hx2 = modulate(rms_norm(x_mid, norm_ffn[l]), sh2, sc2)
        x_new = x_mid + g2 * swiglu(hx2, w_ffn_in[l], w_ffn_out[l])

        if not last:
            cy_conv = short_conv_branch(cbx, ccx, cxx, conv_w[l], conv_b[l], w_conv_out[l])
            q_ctx = mla_queries(cq_a, q_a_norm[l], w_q_b[l], q_norm[l], None)
            cattn = attend(q_ctx, k_ctx, v_ctx)
            ctx_mid = ctx + cg1 * gated_merge(cy_conv, cattn, cgc, cga, b_gate[l], w_attn_o[l], w_out[l])
            hc2 = modulate(rms_norm(ctx_mid, norm_ffn[l]), csh2, csc2)
            ctx = ctx_mid + cg2 * swiglu(hc2, w_ffn_in[l], w_ffn_out[l])
        x = x_new
    return x
```

```python
import functools

import jax
import jax.numpy as jnp
import numpy as np
from jax import lax
from jax.experimental import pallas as pl
from jax.experimental.pallas import tpu as pltpu

F32 = jnp.float32
BF16 = jnp.bfloat16

N_HEADS = 8
QK_NOPE_DIM = 128
QK_ROPE_DIM = 64
QK_HEAD_DIM = QK_NOPE_DIM + QK_ROPE_DIM
V_HEAD_DIM = 128
Q_LORA_RANK = 384
KV_LORA_RANK = 256
GRID_W = 64
ROPE_THETA = 10000.0
NORM_EPS = 1e-6
MOD_CHUNKS = 6

LANES = 128
SUBLANES = 8
VMEM_LIMIT_BYTES = 56 * 1024 * 1024

TOKEN_TILE = 512
ATTN_Q_TILE = 512
ATTN_KV_TILE = 512
ADALN_N_TILE = 1536
FFN_CHUNK = 256


def _rsqrt_mean(ss, n):
    return lax.rsqrt(ss * (1.0 / n) + NORM_EPS)


def _sigmoid(t):
    return 1.0 / (1.0 + jnp.exp(-t))


def _const_spec(shape):
    nd = len(shape)
    return pl.BlockSpec(shape, lambda *_: (0,) * nd, pipeline_mode=pl.Buffered(1))


def _adaln_kernel(cond_ref, w_ref, b_ref, o_ref):
    c = cond_ref[...]
    a = c * _sigmoid(c)
    o_ref[...] = jnp.dot(a, w_ref[...], preferred_element_type=F32,
                         precision=lax.Precision.HIGHEST) + b_ref[...]


def _adaln(cond, w_mod, b_mod):
    rows, d = cond.shape
    n = w_mod.shape[1]
    return pl.pallas_call(
        _adaln_kernel,
        grid=(n // ADALN_N_TILE,),
        in_specs=[pl.BlockSpec((rows, d), lambda j: (0, 0)),
                  pl.BlockSpec((d, ADALN_N_TILE), lambda j: (0, j)),
                  pl.BlockSpec((1, ADALN_N_TILE), lambda j: (0, j))],
        out_specs=pl.BlockSpec((rows, ADALN_N_TILE), lambda j: (0, j)),
        out_shape=jax.ShapeDtypeStruct((rows, n), F32),
        compiler_params=pltpu.CompilerParams(dimension_semantics=("arbitrary",),
                                             vmem_limit_bytes=VMEM_LIMIT_BYTES),
        name="adaln",
    )(cond, w_mod, b_mod)


def _modulated_norm(x, gain, shift, scale):
    d = x.shape[-1]
    r = _rsqrt_mean(jnp.sum(x * x, axis=-1, keepdims=True), d)
    return (x * r * gain) * (1.0 + scale) + shift


def _rope_pair(t2, table):
    w = t2 * table
    return w + pltpu.roll(w, QK_ROPE_DIM, axis=1)


def _keys_values(h, w_kv_in_ref, kvan_ref, wkvb_ref, gk_ref, table, k_ref, v_ref):
    kvp = jnp.dot(h, w_kv_in_ref[...], preferred_element_type=F32)
    kv_a = kvp[:, :KV_LORA_RANK]
    kr2 = kvp[:, KV_LORA_RANK:]
    r = _rsqrt_mean(jnp.sum(kv_a * kv_a, axis=-1, keepdims=True), KV_LORA_RANK)
    kv_n = (kv_a * r * kvan_ref[...]).astype(BF16)
    kv = jnp.dot(kv_n, wkvb_ref[...], preferred_element_type=F32)
    gk = gk_ref[...]
    ss_rope = 0.5 * jnp.sum(kr2 * kr2, axis=-1, keepdims=True)
    kr = _rope_pair(kr2 * gk[:, QK_NOPE_DIM:], table)[:, :QK_ROPE_DIM]
    for hd in range(N_HEADS):
        k_nope = kv[:, hd * QK_NOPE_DIM:(hd + 1) * QK_NOPE_DIM]
        ss = jnp.sum(k_nope * k_nope, axis=-1, keepdims=True) + ss_rope
        rh = _rsqrt_mean(ss, QK_HEAD_DIM)
        k_ref[hd, :, :QK_NOPE_DIM] = (k_nope * rh * gk[:, :QK_NOPE_DIM]).astype(BF16)
        k_ref[hd, :, QK_NOPE_DIM:] = (kr * rh).astype(BF16)
        v0 = N_HEADS * QK_NOPE_DIM + hd * V_HEAD_DIM
        v_ref[hd] = kv[:, v0:v0 + V_HEAD_DIM].astype(BF16)


def _ctx_kv_kernel(ctx_ref, mod_ref, nmix_ref, w_kv_in_ref, kvan_ref, wkvb_ref, gk_ref,
                   table_ref, k_ref, v_ref):
    d = ctx_ref.shape[-1]
    mod = mod_ref[...]
    h = _modulated_norm(ctx_ref[...], nmix_ref[...], mod[:, :d], mod[:, d:2 * d]).astype(BF16)
    _keys_values(h, w_kv_in_ref, kvan_ref, wkvb_ref, gk_ref, table_ref[...], k_ref, v_ref)


def _ctx_kv(ctx, mod_ctx, nmix, w_kv_in, kvan, wkvb, gk, table):
    b, n, d = ctx.shape
    return pl.pallas_call(
        _ctx_kv_kernel,
        grid=(b,),
        in_specs=[pl.BlockSpec((None, n, d), lambda i: (i, 0, 0)),
                  _const_spec(mod_ctx.shape), _const_spec(nmix.shape),
                  _const_spec(w_kv_in.shape), _const_spec(kvan.shape),
                  _const_spec(wkvb.shape), _const_spec(gk.shape), _const_spec(table.shape)],
        out_specs=[pl.BlockSpec((None, N_HEADS, n, QK_HEAD_DIM), lambda i: (i, 0, 0, 0)),
                   pl.BlockSpec((None, N_HEADS, n, V_HEAD_DIM), lambda i: (i, 0, 0, 0))],
        out_shape=[jax.ShapeDtypeStruct((b, N_HEADS, n, QK_HEAD_DIM), BF16),
                   jax.ShapeDtypeStruct((b, N_HEADS, n, V_HEAD_DIM), BF16)],
        compiler_params=pltpu.CompilerParams(dimension_semantics=("arbitrary",),
                                             vmem_limit_bytes=VMEM_LIMIT_BYTES),
        name="ctx_kv",
    )(ctx, mod_ctx, nmix, w_kv_in, kvan, wkvb, gk, table)


def _in_proj_kernel(x_ref, mod_ref, nmix_ref, w_main_ref, w_qa_ref, w_kv_in_ref, bgate_ref,
                    qan_ref, wqb_ref, kvan_ref, wkvb_ref, gq_ref, gk_ref, table_ref,
                    bx_ref, u_ref, gc_ref, ga_ref, q_ref, k_ref, v_ref):
    d = x_ref.shape[-1]
    mod = mod_ref[...]
    h = _modulated_norm(x_ref[...], nmix_ref[...], mod[:, :d], mod[:, d:2 * d]).astype(BF16)
    table = table_ref[...]

    def proj(j):
        return jnp.dot(h, w_main_ref[:, j * d:(j + 1) * d], preferred_element_type=F32)

    bx_ref[...] = proj(0).astype(BF16)
    u_ref[...] = (proj(1) * proj(2)).astype(BF16)
    bgate = bgate_ref[...]
    gc_ref[...] = _sigmoid(proj(3) + bgate[:, :d]).astype(BF16)
    ga_ref[...] = _sigmoid(proj(4) + bgate[:, d:]).astype(BF16)

    q_a = jnp.dot(h, w_qa_ref[...], preferred_element_type=F32)
    r = _rsqrt_mean(jnp.sum(q_a * q_a, axis=-1, keepdims=True), Q_LORA_RANK)
    q_n = (q_a * r * qan_ref[...]).astype(BF16)
    q = jnp.dot(q_n, wqb_ref[...], preferred_element_type=F32)
    gq = gq_ref[...]
    table_q = table * gq[:, QK_NOPE_DIM:]
    sm_scale = QK_HEAD_DIM ** -0.5
    hw = 2 * LANES
    for hd in range(N_HEADS):
        q_nope = q[:, hd * hw:hd * hw + QK_NOPE_DIM]
        q_r2 = q[:, hd * hw + QK_NOPE_DIM:(hd + 1) * hw]
        ss = jnp.sum(q_nope * q_nope + 0.5 * (q_r2 * q_r2), axis=-1, keepdims=True)
        rh = _rsqrt_mean(ss, QK_HEAD_DIM) * sm_scale
        q_ref[hd, :, :QK_NOPE_DIM] = (q_nope * rh * gq[:, :QK_NOPE_DIM]).astype(BF16)
        q_ref[hd, :, QK_NOPE_DIM:] = (_rope_pair(q_r2 * rh, table_q)[:, :QK_ROPE_DIM]).astype(BF16)

    _keys_values(h, w_kv_in_ref, kvan_ref, wkvb_ref, gk_ref, table, k_ref, v_ref)


def _in_proj(x, mod_x, nmix, w_main, w_qa, w_kv_in, bgate, qan, wqb, kvan, wkvb, gq, gk, table):
    b, s, d = x.shape
    tm = TOKEN_TILE
    tok = pl.BlockSpec((None, tm, d), lambda bi, i: (bi, i, 0))

    def head_spec(width):
        return pl.BlockSpec((None, N_HEADS, tm, width), lambda bi, i: (bi, 0, i, 0))

    tok_shape = jax.ShapeDtypeStruct((b, s, d), BF16)
    return pl.pallas_call(
        _in_proj_kernel,
        grid=(b, s // tm),
        in_specs=[tok,
                  pl.BlockSpec((None, 1, mod_x.shape[-1]), lambda bi, i: (bi, 0, 0)),
                  _const_spec(nmix.shape), _const_spec(w_main.shape), _const_spec(w_qa.shape),
                  _const_spec(w_kv_in.shape), _const_spec(bgate.shape), _const_spec(qan.shape),
                  _const_spec(wqb.shape), _const_spec(kvan.shape), _const_spec(wkvb.shape),
                  _const_spec(gq.shape), _const_spec(gk.shape),
                  pl.BlockSpec((tm, 2 * QK_ROPE_DIM), lambda bi, i: (i, 0))],
        out_specs=[tok, tok, tok, tok,
                   head_spec(QK_HEAD_DIM), head_spec(QK_HEAD_DIM), head_spec(V_HEAD_DIM)],
        out_shape=[tok_shape, tok_shape, tok_shape, tok_shape,
                   jax.ShapeDtypeStruct((b, N_HEADS, s, QK_HEAD_DIM), BF16),
                   jax.ShapeDtypeStruct((b, N_HEADS, s, QK_HEAD_DIM), BF16),
                   jax.ShapeDtypeStruct((b, N_HEADS, s, V_HEAD_DIM), BF16)],
        compiler_params=pltpu.CompilerParams(dimension_semantics=("arbitrary", "arbitrary"),
                                             vmem_limit_bytes=VMEM_LIMIT_BYTES),
        name="in_proj",
    )(x, mod_x, nmix, w_main, w_qa, w_kv_in, bgate, qan, wqb, kvan, wkvb, gq, gk, table)


def _attention_kernel(q_ref, kc_ref, vc_ref, kl_ref, vl_ref, o_ref, m_ref, l_ref, acc_ref):
    q = q_ref[...]
    m_ref[...] = jnp.full(m_ref.shape, -jnp.inf, F32)
    l_ref[...] = jnp.zeros(l_ref.shape, F32)
    acc_ref[...] = jnp.zeros(acc_ref.shape, F32)

    def step(k, v):
        s = lax.dot_general(q, k, (((1,), (1,)), ((), ())), preferred_element_type=F32)
        m_old = m_ref[...]
        m_new = jnp.maximum(m_old, jnp.max(s, axis=-1, keepdims=True))
        alpha = jnp.exp(m_old - m_new)
        p = jnp.exp(s - m_new)
        l_ref[...] = alpha * l_ref[...] + jnp.sum(p, axis=-1, keepdims=True)
        acc_ref[...] = alpha * acc_ref[...] + jnp.dot(p.astype(BF16), v,
                                                      preferred_element_type=F32)
        m_ref[...] = m_new

    step(kc_ref[...], vc_ref[...])
    tk = ATTN_KV_TILE

    def body(j, carry):
        start = pl.multiple_of(j * tk, tk)
        step(kl_ref[pl.ds(start, tk), :], vl_ref[pl.ds(start, tk), :])
        return carry

    lax.fori_loop(0, kl_ref.shape[0] // tk, body, 0)
    o_ref[...] = (acc_ref[...] / l_ref[...]).astype(o_ref.dtype)


def _attention(q, k_ctx, v_ctx, k_lat, v_lat):
    b, nh, s, dk = q.shape
    nc = k_ctx.shape[2]
    dv = v_lat.shape[-1]
    tq = ATTN_Q_TILE

    def kv_spec(n, width):
        return pl.BlockSpec((None, None, n, width), lambda bi, hi, i: (bi, hi, 0, 0))

    return pl.pallas_call(
        _attention_kernel,
        grid=(b, nh, s // tq),
        in_specs=[pl.BlockSpec((None, None, tq, dk), lambda bi, hi, i: (bi, hi, i, 0)),
                  kv_spec(nc, dk), kv_spec(nc, dv), kv_spec(s, dk), kv_spec(s, dv)],
        out_specs=pl.BlockSpec((None, tq, dv), lambda bi, hi, i: (bi, i, hi)),
        out_shape=jax.ShapeDtypeStruct((b, s, nh * dv), BF16),
        scratch_shapes=[pltpu.VMEM((tq, 1), F32), pltpu.VMEM((tq, 1), F32),
                        pltpu.VMEM((tq, dv), F32)],
        compiler_params=pltpu.CompilerParams(
            dimension_semantics=("arbitrary", "arbitrary", "arbitrary"),
            vmem_limit_bytes=VMEM_LIMIT_BYTES),
        name="attention",
    )(q, k_ctx, v_ctx, k_lat, v_lat)


def _merge_kernel(x_ref, mod_ref, bx_ref, u_ref, u_prev_ref, u_next_ref, gc_ref, ga_ref,
                  attn_ref, convw_ref, convb_ref, w_conv_out_ref, w_attn_o_ref, w_out_ref,
                  o_ref):
    i = pl.program_id(1)
    tm, d = x_ref.shape
    u = u_ref[...].astype(F32)
    prev_row = u_prev_ref[SUBLANES - 1:SUBLANES, :].astype(F32) * (i > 0).astype(F32)
    next_row = u_next_ref[0:1, :].astype(F32) * (i < pl.num_programs(1) - 1).astype(F32)
    row = lax.broadcasted_iota(jnp.int32, (tm, d), 0)
    u_m1 = jnp.where(row == 0, prev_row, pltpu.roll(u, 1, axis=0))
    u_p1 = jnp.where(row == tm - 1, next_row, pltpu.roll(u, tm - 1, axis=0))
    cw = convw_ref[...]
    conv = u_m1 * cw[0:1, :] + u * cw[1:2, :] + u_p1 * cw[2:3, :] + convb_ref[...]
    z = (bx_ref[...].astype(F32) * conv).astype(BF16)
    y_conv = jnp.dot(z, w_conv_out_ref[...], preferred_element_type=F32)
    y_attn = jnp.dot(attn_ref[...], w_attn_o_ref[...], preferred_element_type=F32)
    merged = (gc_ref[...].astype(F32) * y_conv + ga_ref[...].astype(F32) * y_attn).astype(BF16)
    y = jnp.dot(merged, w_out_ref[...], preferred_element_type=F32)
    g1 = mod_ref[:, 2 * d:3 * d]
    o_ref[...] = x_ref[...] + g1 * y


def _merge(x, mod_x, bx, u, gc, ga, attn, conv_w, conv_b, w_conv_out, w_attn_o, w_out):
    b, s, d = x.shape
    tm = TOKEN_TILE
    hb = tm // SUBLANES
    last_hb = s // SUBLANES - 1
    tok = pl.BlockSpec((None, tm, d), lambda bi, i: (bi, i, 0))
    prev = pl.BlockSpec((None, SUBLANES, d), lambda bi, i: (bi, jnp.maximum(i * hb - 1, 0), 0))
    nxt = pl.BlockSpec((None, SUBLANES, d),
                       lambda bi, i: (bi, jnp.minimum((i + 1) * hb, last_hb), 0))
    return pl.pallas_call(
        _merge_kernel,
        grid=(b, s // tm),
        in_specs=[tok,
                  pl.BlockSpec((None, 1, mod_x.shape[-1]), lambda bi, i: (bi, 0, 0)),
                  tok, tok, prev, nxt, tok, tok, tok,
                  _const_spec(conv_w.shape), _const_spec(conv_b.shape),
                  _const_spec(w_conv_out.shape), _const_spec(w_attn_o.shape),
                  _const_spec(w_out.shape)],
        out_specs=tok,
        out_shape=jax.ShapeDtypeStruct((b, s, d), F32),
        compiler_params=pltpu.CompilerParams(dimension_semantics=("arbitrary", "arbitrary"),
                                             vmem_limit_bytes=VMEM_LIMIT_BYTES),
        name="merge",
    )(x, mod_x, bx, u, u, u, gc, ga, attn, conv_w, conv_b, w_conv_out, w_attn_o, w_out)


def _ffn_kernel(x_ref, mod_ref, nffn_ref, w_in_ref, w_out_ref, o_ref, act_ref):
    d = x_ref.shape[-1]
    d_ff = w_out_ref.shape[0]
    x = x_ref[...]
    mod = mod_ref[...]
    h = _modulated_norm(x, nffn_ref[...], mod[:, 3 * d:4 * d], mod[:, 4 * d:5 * d]).astype(BF16)
    for j in range(d_ff // FFN_CHUNK):
        c0 = j * FFN_CHUNK
        gate = jnp.dot(h, w_in_ref[:, c0:c0 + FFN_CHUNK], preferred_element_type=F32)
        up = jnp.dot(h, w_in_ref[:, d_ff + c0:d_ff + c0 + FFN_CHUNK],
                     preferred_element_type=F32)
        act_ref[:, c0:c0 + FFN_CHUNK] = (gate * _sigmoid(gate) * up).astype(BF16)
    y = jnp.dot(act_ref[...], w_out_ref[...], preferred_element_type=F32)
    o_ref[...] = x + mod[:, 5 * d:6 * d] * y


def _ffn(x, mod_x, nffn, w_ffn_in, w_ffn_out):
    b, s, d = x.shape
    tm = TOKEN_TILE
    d_ff = w_ffn_out.shape[0]
    tok = pl.BlockSpec((None, tm, d), lambda bi, i: (bi, i, 0))
    return pl.pallas_call(
        _ffn_kernel,
        grid=(b, s // tm),
        in_specs=[tok,
                  pl.BlockSpec((None, 1, mod_x.shape[-1]), lambda bi, i: (bi, 0, 0)),
                  _const_spec(nffn.shape), _const_spec(w_ffn_in.shape),
                  _const_spec(w_ffn_out.shape)],
        out_specs=tok,
        out_shape=jax.ShapeDtypeStruct((b, s, d), F32),
        scratch_shapes=[pltpu.VMEM((tm, d_ff), BF16)],
        compiler_params=pltpu.CompilerParams(dimension_semantics=("arbitrary", "arbitrary"),
                                             vmem_limit_bytes=VMEM_LIMIT_BYTES),
        name="ffn",
    )(x, mod_x, nffn, w_ffn_in, w_ffn_out)


def _rope_table(seq_len):
    quarter = QK_ROPE_DIM // 4
    freqs = ROPE_THETA ** (-jnp.arange(quarter, dtype=F32) / quarter)
    t = jnp.arange(seq_len)
    ang = [(t // GRID_W)[:, None].astype(F32) * freqs,
           (t % GRID_W)[:, None].astype(F32) * freqs]
    cos = [jnp.cos(a) for a in ang]
    sin = [jnp.sin(a) for a in ang]
    c = jnp.concatenate([cos[0], cos[0], cos[1], cos[1]], axis=-1)
    s = jnp.concatenate([-sin[0], sin[0], -sin[1], sin[1]], axis=-1)
    return jnp.concatenate([c, s], axis=-1)


def _swap_halves(t):
    q = QK_ROPE_DIM // 4
    return jnp.concatenate([t[..., q:2 * q], t[..., :q], t[..., 3 * q:], t[..., 2 * q:3 * q]],
                           axis=-1)


def _norm_gain_pair(g):
    rope = g[QK_NOPE_DIM:]
    return jnp.concatenate([g[:QK_NOPE_DIM], rope, _swap_halves(rope)])[None, :].astype(F32)


def kernel(x, c, ctx, c_ctx, w_mod, b_mod, norm_mix, norm_ffn, w_in, b_gate, conv_w, conv_b,
           w_conv_out, q_a_norm, w_q_b, kv_a_norm, w_kv_b, q_norm, k_norm, w_attn_o, w_out,
           w_ffn_in, w_ffn_out):
    depth = w_mod.shape[0]
    assert depth == 1, "context stream update is only needed between layers"
    b, s, d = x.shape
    n_ctx = ctx.shape[1]
    assert s % TOKEN_TILE == 0 and s % ATTN_Q_TILE == 0 and s % ATTN_KV_TILE == 0
    assert b + 1 <= SUBLANES

    cond = jnp.zeros((SUBLANES, d), F32).at[:b].set(c).at[b].set(c_ctx)
    mod = _adaln(cond, w_mod[0], b_mod[0][None, :])
    mod_x = mod[:b, None, :]
    mod_ctx = mod[b:b + 1]

    wi = w_in[0]
    o_q = 3 * d
    o_kv = o_q + Q_LORA_RANK
    o_kr = o_kv + KV_LORA_RANK
    o_gc = o_kr + QK_ROPE_DIM
    w_main = jnp.concatenate([wi[:, :3 * d], wi[:, o_gc:]], axis=1).astype(BF16)
    w_qa = wi[:, o_q:o_kv].astype(BF16)
    w_kr = wi[:, o_kr:o_gc]
    w_kv_in = jnp.concatenate([wi[:, o_kv:o_kr], w_kr, _swap_halves(w_kr)], axis=1).astype(BF16)
    wq = w_q_b[0].reshape(Q_LORA_RANK, N_HEADS, QK_HEAD_DIM)
    wq = jnp.concatenate([wq, _swap_halves(wq[..., QK_NOPE_DIM:])], axis=-1)
    wqb = wq.reshape(Q_LORA_RANK, N_HEADS * 2 * LANES).astype(BF16)
    wkv = w_kv_b[0].reshape(KV_LORA_RANK, N_HEADS, QK_NOPE_DIM + V_HEAD_DIM)
    wkvb = jnp.concatenate([wkv[..., :QK_NOPE_DIM].reshape(KV_LORA_RANK, -1),
                            wkv[..., QK_NOPE_DIM:].reshape(KV_LORA_RANK, -1)],
                           axis=1).astype(BF16)
    gq = _norm_gain_pair(q_norm[0])
    gk = _norm_gain_pair(k_norm[0])
    table = _rope_table(s)
    table_ctx = jnp.concatenate([jnp.ones((n_ctx, QK_ROPE_DIM), F32),
                                 jnp.zeros((n_ctx, QK_ROPE_DIM), F32)], axis=1)
    nmix = norm_mix[0][None, :]
    kvan = kv_a_norm[0][None, :]

    k_ctx, v_ctx = _ctx_kv(ctx, mod_ctx, nmix, w_kv_in, kvan, wkvb, gk, table_ctx)
    bx, u, gc, ga, q, k_lat, v_lat = _in_proj(
        x, mod_x, nmix, w_main, w_qa, w_kv_in, b_gate[0][None, :], q_a_norm[0][None, :], wqb,
        kvan, wkvb, gq, gk, table)
    attn = _attention(q, k_ctx, v_ctx, k_lat, v_lat)
    x_mid = _merge(x, mod_x, bx, u, gc, ga, attn, conv_w[0], conv_b[0][None, :],
                   w_conv_out[0].astype(BF16), w_attn_o[0].astype(BF16), w_out[0].astype(BF16))
    return _ffn(x_mid, mod_x, norm_ffn[0][None, :], w_ffn_in[0].astype(BF16),
                w_ffn_out[0].astype(BF16))
```

```python
import functools

import jax
import jax.numpy as jnp
import numpy as np
from jax import lax
from jax.experimental import pallas as pl
from jax.experimental.pallas import tpu as pltpu

F32 = jnp.float32
BF16 = jnp.bfloat16

N_HEADS = 8
QK_NOPE_DIM = 128
QK_ROPE_DIM = 64
QK_HEAD_DIM = QK_NOPE_DIM + QK_ROPE_DIM
V_HEAD_DIM = 128
Q_LORA_RANK = 384
KV_LORA_RANK = 256
GRID_W = 64
ROPE_THETA = 10000.0
NORM_EPS = 1e-6
MOD_CHUNKS = 6

LANES = 128
SUBLANES = 8
VMEM_LIMIT_BYTES = 56 * 1024 * 1024

TOKEN_TILE = 512
ATTN_Q_TILE = 512
ATTN_Q_SUB = 256
ATTN_KV_TILE = 512
ATTN_KV_UNROLL = 4
ATTN_LOOKAHEAD = 2
ADALN_N_TILE = 1536
FFN_CHUNK = 256


def _rsqrt_mean(ss, n):
    return lax.rsqrt(ss * (1.0 / n) + NORM_EPS)


def _sigmoid(t):
    return 1.0 / (1.0 + jnp.exp(-t))


def _const_spec(shape):
    nd = len(shape)
    return pl.BlockSpec(shape, lambda *_: (0,) * nd, pipeline_mode=pl.Buffered(1))


def _adaln_kernel(cond_ref, w_ref, b_ref, o_ref):
    c = cond_ref[...]
    a = c * _sigmoid(c)
    o_ref[...] = jnp.dot(a, w_ref[...], preferred_element_type=F32,
                         precision=lax.Precision.HIGHEST) + b_ref[...]


def _adaln(cond, w_mod, b_mod):
    rows, d = cond.shape
    n = w_mod.shape[1]
    return pl.pallas_call(
        _adaln_kernel,
        grid=(n // ADALN_N_TILE,),
        in_specs=[pl.BlockSpec((rows, d), lambda j: (0, 0)),
                  pl.BlockSpec((d, ADALN_N_TILE), lambda j: (0, j)),
                  pl.BlockSpec((1, ADALN_N_TILE), lambda j: (0, j))],
        out_specs=pl.BlockSpec((rows, ADALN_N_TILE), lambda j: (0, j)),
        out_shape=jax.ShapeDtypeStruct((rows, n), F32),
        compiler_params=pltpu.CompilerParams(dimension_semantics=("arbitrary",),
                                             vmem_limit_bytes=VMEM_LIMIT_BYTES),
        name="adaln",
    )(cond, w_mod, b_mod)


def _modulated_norm(x, gain, shift, scale):
    d = x.shape[-1]
    r = _rsqrt_mean(jnp.sum(x * x, axis=-1, keepdims=True), d)
    return (x * r * gain) * (1.0 + scale) + shift


def _rope_pair(t2, table):
    w = t2 * table
    return w + pltpu.roll(w, QK_ROPE_DIM, axis=1)


def _keys_values(h, w_kv_in_ref, kvan_ref, wkvb_ref, gk_ref, table, k_ref, v_ref):
    kvp = jnp.dot(h, w_kv_in_ref[...], preferred_element_type=F32)
    kv_a = kvp[:, :KV_LORA_RANK]
    kr2 = kvp[:, KV_LORA_RANK:]
    r = _rsqrt_mean(jnp.sum(kv_a * kv_a, axis=-1, keepdims=True), KV_LORA_RANK)
    kv_n = (kv_a * r * kvan_ref[...]).astype(BF16)
    kv = jnp.dot(kv_n, wkvb_ref[...], preferred_element_type=F32)
    gk = gk_ref[...]
    ss_rope = 0.5 * jnp.sum(kr2 * kr2, axis=-1, keepdims=True)
    kr = _rope_pair(kr2 * gk[:, QK_NOPE_DIM:], table)[:, :QK_ROPE_DIM]
    for hd in range(N_HEADS):
        k_nope = kv[:, hd * QK_NOPE_DIM:(hd + 1) * QK_NOPE_DIM]
        ss = jnp.sum(k_nope * k_nope, axis=-1, keepdims=True) + ss_rope
        rh = _rsqrt_mean(ss, QK_HEAD_DIM)
        k_ref[hd, :, :QK_NOPE_DIM] = (k_nope * rh * gk[:, :QK_NOPE_DIM]).astype(BF16)
        k_ref[hd, :, QK_NOPE_DIM:] = (kr * rh).astype(BF16)
        v0 = N_HEADS * QK_NOPE_DIM + hd * V_HEAD_DIM
        v_ref[hd] = kv[:, v0:v0 + V_HEAD_DIM].astype(BF16)


def _ctx_kv_kernel(ctx_ref, mod_ref, nmix_ref, w_kv_in_ref, kvan_ref, wkvb_ref, gk_ref,
                   table_ref, k_ref, v_ref):
    d = ctx_ref.shape[-1]
    mod = mod_ref[...]
    h = _modulated_norm(ctx_ref[...], nmix_ref[...], mod[:, :d], mod[:, d:2 * d]).astype(BF16)
    _keys_values(h, w_kv_in_ref, kvan_ref, wkvb_ref, gk_ref, table_ref[...], k_ref, v_ref)


def _ctx_kv(ctx, mod_ctx, nmix, w_kv_in, kvan, wkvb, gk, table):
    b, n, d = ctx.shape
    return pl.pallas_call(
        _ctx_kv_kernel,
        grid=(b,),
        in_specs=[pl.BlockSpec((None, n, d), lambda i: (i, 0, 0)),
                  _const_spec(mod_ctx.shape), _const_spec(nmix.shape),
                  _const_spec(w_kv_in.shape), _const_spec(kvan.shape),
                  _const_spec(wkvb.shape), _const_spec(gk.shape), _const_spec(table.shape)],
        out_specs=[pl.BlockSpec((None, N_HEADS, n, QK_HEAD_DIM), lambda i: (i, 0, 0, 0)),
                   pl.BlockSpec((None, N_HEADS, n, V_HEAD_DIM), lambda i: (i, 0, 0, 0))],
        out_shape=[jax.ShapeDtypeStruct((b, N_HEADS, n, QK_HEAD_DIM), BF16),
                   jax.ShapeDtypeStruct((b, N_HEADS, n, V_HEAD_DIM), BF16)],
        compiler_params=pltpu.CompilerParams(dimension_semantics=("arbitrary",),
                                             vmem_limit_bytes=VMEM_LIMIT_BYTES),
        name="ctx_kv",
    )(ctx, mod_ctx, nmix, w_kv_in, kvan, wkvb, gk, table)


def _in_proj_kernel(x_ref, mod_ref, nmix_ref, w_main_ref, w_qa_ref, w_kv_in_ref, bgate_ref,
                    qan_ref, wqb_ref, kvan_ref, wkvb_ref, gq_ref, gk_ref, table_ref,
                    bx_ref, u_ref, gc_ref, ga_ref, q_ref, k_ref, v_ref):
    d = x_ref.shape[-1]
    mod = mod_ref[...]
    h = _modulated_norm(x_ref[...], nmix_ref[...], mod[:, :d], mod[:, d:2 * d]).astype(BF16)
    table = table_ref[...]

    def proj(j):
        return jnp.dot(h, w_main_ref[:, j * d:(j + 1) * d], preferred_element_type=F32)

    bx_ref[...] = proj(0).astype(BF16)
    u_ref[...] = (proj(1) * proj(2)).astype(BF16)
    bgate = bgate_ref[...]
    gc_ref[...] = _sigmoid(proj(3) + bgate[:, :d]).astype(BF16)
    ga_ref[...] = _sigmoid(proj(4) + bgate[:, d:]).astype(BF16)

    q_a = jnp.dot(h, w_qa_ref[...], preferred_element_type=F32)
    r = _rsqrt_mean(jnp.sum(q_a * q_a, axis=-1, keepdims=True), Q_LORA_RANK)
    q_n = (q_a * r * qan_ref[...]).astype(BF16)
    q = jnp.dot(q_n, wqb_ref[...], preferred_element_type=F32)
    gq = gq_ref[...]
    table_q = table * gq[:, QK_NOPE_DIM:]
    sm_scale = QK_HEAD_DIM ** -0.5 * np.log2(np.e)
    hw = 2 * LANES
    for hd in range(N_HEADS):
        q_nope = q[:, hd * hw:hd * hw + QK_NOPE_DIM]
        q_r2 = q[:, hd * hw + QK_NOPE_DIM:(hd + 1) * hw]
        ss = jnp.sum(q_nope * q_nope + 0.5 * (q_r2 * q_r2), axis=-1, keepdims=True)
        rh = _rsqrt_mean(ss, QK_HEAD_DIM) * sm_scale
        q_ref[hd, :, :QK_NOPE_DIM] = (q_nope * rh * gq[:, :QK_NOPE_DIM]).astype(BF16)
        q_ref[hd, :, QK_NOPE_DIM:] = (_rope_pair(q_r2 * rh, table_q)[:, :QK_ROPE_DIM]).astype(BF16)

    _keys_values(h, w_kv_in_ref, kvan_ref, wkvb_ref, gk_ref, table, k_ref, v_ref)


def _in_proj(x, mod_x, nmix, w_main, w_qa, w_kv_in, bgate, qan, wqb, kvan, wkvb, gq, gk, table):
    b, s, d = x.shape
    tm = TOKEN_TILE
    tok = pl.BlockSpec((None, tm, d), lambda bi, i: (bi, i, 0))

    def head_spec(width):
        return pl.BlockSpec((None, N_HEADS, tm, width), lambda bi, i: (bi, 0, i, 0))

    tok_shape = jax.ShapeDtypeStruct((b, s, d), BF16)
    return pl.pallas_call(
        _in_proj_kernel,
        grid=(b, s // tm),
        in_specs=[tok,
                  pl.BlockSpec((None, 1, mod_x.shape[-1]), lambda bi, i: (bi, 0, 0)),
                  _const_spec(nmix.shape), _const_spec(w_main.shape), _const_spec(w_qa.shape),
                  _const_spec(w_kv_in.shape), _const_spec(bgate.shape), _const_spec(qan.shape),
                  _const_spec(wqb.shape), _const_spec(kvan.shape), _const_spec(wkvb.shape),
                  _const_spec(gq.shape), _const_spec(gk.shape),
                  pl.BlockSpec((tm, 2 * QK_ROPE_DIM), lambda bi, i: (i, 0))],
        out_specs=[tok, tok, tok, tok,
                   head_spec(QK_HEAD_DIM), head_spec(QK_HEAD_DIM), head_spec(V_HEAD_DIM)],
        out_shape=[tok_shape, tok_shape, tok_shape, tok_shape,
                   jax.ShapeDtypeStruct((b, N_HEADS, s, QK_HEAD_DIM), BF16),
                   jax.ShapeDtypeStruct((b, N_HEADS, s, QK_HEAD_DIM), BF16),
                   jax.ShapeDtypeStruct((b, N_HEADS, s, V_HEAD_DIM), BF16)],
        compiler_params=pltpu.CompilerParams(dimension_semantics=("arbitrary", "arbitrary"),
                                             vmem_limit_bytes=VMEM_LIMIT_BYTES),
        name="in_proj",
    )(x, mod_x, nmix, w_main, w_qa, w_kv_in, bgate, qan, wqb, kvan, wkvb, gq, gk, table)


def _attention_kernel(q_ref, kc_ref, vc_ref, kl_ref, vl_ref, o_ref, m_ref, l_ref, acc_ref):
    tq = q_ref.shape[0]
    m_ref[...] = jnp.full(m_ref.shape, -jnp.inf, F32)
    l_ref[...] = jnp.zeros(l_ref.shape, F32)
    acc_ref[...] = jnp.zeros(acc_ref.shape, F32)

    def scores(unit):
        k_ref, _, k0, tk, r0 = unit
        return lax.dot_general(q_ref[r0:r0 + ATTN_Q_SUB, :], k_ref[pl.ds(k0, tk), :],
                               (((1,), (1,)), ((), ())), preferred_element_type=F32)

    def finish(unit, s):
        _, v_ref, k0, tk, r0 = unit
        rows = slice(r0, r0 + ATTN_Q_SUB)
        cols = [s[:, c:c + LANES] for c in range(0, tk, LANES)]
        m_old = m_ref[rows, :]
        row_max = jnp.max(functools.reduce(jnp.maximum, cols), axis=-1, keepdims=True)
        m_new = jnp.maximum(m_old, row_max)
        alpha = jnp.exp2(m_old - m_new)
        ps = [jnp.exp2(col - m_new) for col in cols]
        l_ref[rows, :] = alpha * l_ref[rows, :] + functools.reduce(jnp.add, ps)
        p = jnp.concatenate([x.astype(BF16) for x in ps], axis=1)
        pv = jnp.dot(p, v_ref[pl.ds(k0, tk), :], preferred_element_type=F32)
        acc_ref[rows, :] = alpha * acc_ref[rows, :] + pv
        m_ref[rows, :] = m_new

    def run(units):
        pending = [scores(u) for u in units[:ATTN_LOOKAHEAD]]
        for i, unit in enumerate(units):
            if i + ATTN_LOOKAHEAD < len(units):
                pending.append(scores(units[i + ATTN_LOOKAHEAD]))
            finish(unit, pending[i])

    def units_of(k_ref, v_ref, starts, tk):
        return [(k_ref, v_ref, k0, tk, r0) for k0 in starts for r0 in range(0, tq, ATTN_Q_SUB)]

    run(units_of(kc_ref, vc_ref, [0], kc_ref.shape[0]))
    tk = ATTN_KV_TILE
    span = tk * ATTN_KV_UNROLL

    def body(j, carry):
        base = pl.multiple_of(j * span, span)
        run(units_of(kl_ref, vl_ref, [base + u * tk for u in range(ATTN_KV_UNROLL)], tk))
        return carry

    lax.fori_loop(0, kl_ref.shape[0] // span, body, 0)
    l = jnp.sum(l_ref[...], axis=-1, keepdims=True)
    o_ref[...] = (acc_ref[...] / l).astype(o_ref.dtype)


def _attention(q, k_ctx, v_ctx, k_lat, v_lat):
    b, nh, s, dk = q.shape
    nc = k_ctx.shape[2]
    dv = v_lat.shape[-1]
    assert dv == LANES
    tq = ATTN_Q_TILE

    def kv_spec(n, width):
        return pl.BlockSpec((None, None, n, width), lambda bi, hi, i: (bi, hi, 0, 0))

    return pl.pallas_call(
        _attention_kernel,
        grid=(b, nh, s // tq),
        in_specs=[pl.BlockSpec((None, None, tq, dk), lambda bi, hi, i: (bi, hi, i, 0)),
                  kv_spec(nc, dk), kv_spec(nc, dv), kv_spec(s, dk), kv_spec(s, dv)],
        out_specs=pl.BlockSpec((None, tq, dv), lambda bi, hi, i: (bi, i, hi)),
        out_shape=jax.ShapeDtypeStruct((b, s, nh * dv), BF16),
        scratch_shapes=[pltpu.VMEM((tq, LANES), F32), pltpu.VMEM((tq, LANES), F32),
                        pltpu.VMEM((tq, dv), F32)],
        compiler_params=pltpu.CompilerParams(
            dimension_semantics=("arbitrary", "arbitrary", "arbitrary"),
            vmem_limit_bytes=VMEM_LIMIT_BYTES),
        name="attention",
    )(q, k_ctx, v_ctx, k_lat, v_lat)


def _merge_kernel(x_ref, mod_ref, bx_ref, u_ref, u_prev_ref, u_next_ref, gc_ref, ga_ref,
                  attn_ref, convw_ref, convb_ref, w_conv_out_ref, w_attn_o_ref, w_out_ref,
                  o_ref):
    i = pl.program_id(1)
    tm, d = x_ref.shape
    u = u_ref[...].astype(F32)
    prev_row = u_prev_ref[SUBLANES - 1:SUBLANES, :].astype(F32) * (i > 0).astype(F32)
    next_row = u_next_ref[0:1, :].astype(F32) * (i < pl.num_programs(1) - 1).astype(F32)
    row = lax.broadcasted_iota(jnp.int32, (tm, d), 0)
    u_m1 = jnp.where(row == 0, prev_row, pltpu.roll(u, 1, axis=0))
    u_p1 = jnp.where(row == tm - 1, next_row, pltpu.roll(u, tm - 1, axis=0))
    cw = convw_ref[...]
    conv = u_m1 * cw[0:1, :] + u * cw[1:2, :] + u_p1 * cw[2:3, :] + convb_ref[...]
    z = (bx_ref[...].astype(F32) * conv).astype(BF16)
    y_conv = jnp.dot(z, w_conv_out_ref[...], preferred_element_type=F32)
    y_attn = jnp.dot(attn_ref[...], w_attn_o_ref[...], preferred_element_type=F32)
    merged = (gc_ref[...].astype(F32) * y_conv + ga_ref[...].astype(F32) * y_attn).astype(BF16)
    y = jnp.dot(merged, w_out_ref[...], preferred_element_type=F32)
    g1 = mod_ref[:, 2 * d:3 * d]
    o_ref[...] = x_ref[...] + g1 * y


def _merge(x, mod_x, bx, u, gc, ga, attn, conv_w, conv_b, w_conv_out, w_attn_o, w_out):
    b, s, d = x.shape
    tm = TOKEN_TILE
    hb = tm // SUBLANES
    last_hb = s // SUBLANES - 1
    tok = pl.BlockSpec((None, tm, d), lambda bi, i: (bi, i, 0))
    prev = pl.BlockSpec((None, SUBLANES, d), lambda bi, i: (bi, jnp.maximum(i * hb - 1, 0), 0))
    nxt = pl.BlockSpec((None, SUBLANES, d),
                       lambda bi, i: (bi, jnp.minimum((i + 1) * hb, last_hb), 0))
    return pl.pallas_call(
        _merge_kernel,
        grid=(b, s // tm),
        in_specs=[tok,
                  pl.BlockSpec((None, 1, mod_x.shape[-1]), lambda bi, i: (bi, 0, 0)),
                  tok, tok, prev, nxt, tok, tok, tok,
                  _const_spec(conv_w.shape), _const_spec(conv_b.shape),
                  _const_spec(w_conv_out.shape), _const_spec(w_attn_o.shape),
                  _const_spec(w_out.shape)],
        out_specs=tok,
        out_shape=jax.ShapeDtypeStruct((b, s, d), F32),
        compiler_params=pltpu.CompilerParams(dimension_semantics=("arbitrary", "arbitrary"),
                                             vmem_limit_bytes=VMEM_LIMIT_BYTES),
        name="merge",
    )(x, mod_x, bx, u, u, u, gc, ga, attn, conv_w, conv_b, w_conv_out, w_attn_o, w_out)


def _ffn_kernel(x_ref, mod_ref, nffn_ref, w_in_ref, w_out_ref, o_ref, act_ref):
    d = x_ref.shape[-1]
    d_ff = w_out_ref.shape[0]
    x = x_ref[...]
    mod = mod_ref[...]
    h = _modulated_norm(x, nffn_ref[...], mod[:, 3 * d:4 * d], mod[:, 4 * d:5 * d]).astype(BF16)
    for j in range(d_ff // FFN_CHUNK):
        c0 = j * FFN_CHUNK
        gate = jnp.dot(h, w_in_ref[:, c0:c0 + FFN_CHUNK], preferred_element_type=F32)
        up = jnp.dot(h, w_in_ref[:, d_ff + c0:d_ff + c0 + FFN_CHUNK],
                     preferred_element_type=F32)
        act_ref[:, c0:c0 + FFN_CHUNK] = (gate * _sigmoid(gate) * up).astype(BF16)
    y = jnp.dot(act_ref[...], w_out_ref[...], preferred_element_type=F32)
    o_ref[...] = x + mod[:, 5 * d:6 * d] * y


def _ffn(x, mod_x, nffn, w_ffn_in, w_ffn_out):
    b, s, d = x.shape
    tm = TOKEN_TILE
    d_ff = w_ffn_out.shape[0]
    tok = pl.BlockSpec((None, tm, d), lambda bi, i: (bi, i, 0))
    return pl.pallas_call(
        _ffn_kernel,
        grid=(b, s // tm),
        in_specs=[tok,
                  pl.BlockSpec((None, 1, mod_x.shape[-1]), lambda bi, i: (bi, 0, 0)),
                  _const_spec(nffn.shape), _const_spec(w_ffn_in.shape),
                  _const_spec(w_ffn_out.shape)],
        out_specs=tok,
        out_shape=jax.ShapeDtypeStruct((b, s, d), F32),
        scratch_shapes=[pltpu.VMEM((tm, d_ff), BF16)],
        compiler_params=pltpu.CompilerParams(dimension_semantics=("arbitrary", "arbitrary"),
                                             vmem_limit_bytes=VMEM_LIMIT_BYTES),
        name="ffn",
    )(x, mod_x, nffn, w_ffn_in, w_ffn_out)


def _rope_table(seq_len):
    quarter = QK_ROPE_DIM // 4
    freqs = ROPE_THETA ** (-jnp.arange(quarter, dtype=F32) / quarter)
    t = jnp.arange(seq_len)
    ang = [(t // GRID_W)[:, None].astype(F32) * freqs,
           (t % GRID_W)[:, None].astype(F32) * freqs]
    cos = [jnp.cos(a) for a in ang]
    sin = [jnp.sin(a) for a in ang]
    c = jnp.concatenate([cos[0], cos[0], cos[1], cos[1]], axis=-1)
    s = jnp.concatenate([-sin[0], sin[0], -sin[1], sin[1]], axis=-1)
    return jnp.concatenate([c, s], axis=-1)


def _swap_halves(t):
    q = QK_ROPE_DIM // 4
    return jnp.concatenate([t[..., q:2 * q], t[..., :q], t[..., 3 * q:], t[..., 2 * q:3 * q]],
                           axis=-1)


def _norm_gain_pair(g):
    rope = g[QK_NOPE_DIM:]
    return jnp.concatenate([g[:QK_NOPE_DIM], rope, _swap_halves(rope)])[None, :].astype(F32)


def kernel(x, c, ctx, c_ctx, w_mod, b_mod, norm_mix, norm_ffn, w_in, b_gate, conv_w, conv_b,
           w_conv_out, q_a_norm, w_q_b, kv_a_norm, w_kv_b, q_norm, k_norm, w_attn_o, w_out,
           w_ffn_in, w_ffn_out):
    depth = w_mod.shape[0]
    assert depth == 1, "context stream update is only needed between layers"
    b, s, d = x.shape
    n_ctx = ctx.shape[1]
    assert s % TOKEN_TILE == 0 and s % ATTN_Q_TILE == 0 and s % ATTN_KV_TILE == 0
    assert b + 1 <= SUBLANES

    cond = jnp.zeros((SUBLANES, d), F32).at[:b].set(c).at[b].set(c_ctx)
    mod = _adaln(cond, w_mod[0], b_mod[0][None, :])
    mod_x = mod[:b, None, :]
    mod_ctx = mod[b:b + 1]

    wi = w_in[0]
    o_q = 3 * d
    o_kv = o_q + Q_LORA_RANK
    o_kr = o_kv + KV_LORA_RANK
    o_gc = o_kr + QK_ROPE_DIM
    w_main = jnp.concatenate([wi[:, :3 * d], wi[:, o_gc:]], axis=1).astype(BF16)
    w_qa = wi[:, o_q:o_kv].astype(BF16)
    w_kr = wi[:, o_kr:o_gc]
    w_kv_in = jnp.concatenate([wi[:, o_kv:o_kr], w_kr, _swap_halves(w_kr)], axis=1).astype(BF16)
    wq = w_q_b[0].reshape(Q_LORA_RANK, N_HEADS, QK_HEAD_DIM)
    wq = jnp.concatenate([wq, _swap_halves(wq[..., QK_NOPE_DIM:])], axis=-1)
    wqb = wq.reshape(Q_LORA_RANK, N_HEADS * 2 * LANES).astype(BF16)
    wkv = w_kv_b[0].reshape(KV_LORA_RANK, N_HEADS, QK_NOPE_DIM + V_HEAD_DIM)
    wkvb = jnp.concatenate([wkv[..., :QK_NOPE_DIM].reshape(KV_LORA_RANK, -1),
                            wkv[..., QK_NOPE_DIM:].reshape(KV_LORA_RANK, -1)],
                           axis=1).astype(BF16)
    gq = _norm_gain_pair(q_norm[0])
    gk = _norm_gain_pair(k_norm[0])
    table = _rope_table(s)
    table_ctx = jnp.concatenate([jnp.ones((n_ctx, QK_ROPE_DIM), F32),
                                 jnp.zeros((n_ctx, QK_ROPE_DIM), F32)], axis=1)
    nmix = norm_mix[0][None, :]
    kvan = kv_a_norm[0][None, :]

    k_ctx, v_ctx = _ctx_kv(ctx, mod_ctx, nmix, w_kv_in, kvan, wkvb, gk, table_ctx)
    bx, u, gc, ga, q, k_lat, v_lat = _in_proj(
        x, mod_x, nmix, w_main, w_qa, w_kv_in, b_gate[0][None, :], q_a_norm[0][None, :], wqb,
        kvan, wkvb, gq, gk, table)
    attn = _attention(q, k_ctx, v_ctx, k_lat, v_lat)
    x_mid = _merge(x, mod_x, bx, u, gc, ga, attn, conv_w[0], conv_b[0][None, :],
                   w_conv_out[0].astype(BF16), w_attn_o[0].astype(BF16), w_out[0].astype(BF16))
    return _ffn(x_mid, mod_x, norm_ffn[0][None, :], w_ffn_in[0].astype(BF16),
                w_ffn_out[0].astype(BF16))
```

```python
import functools

import jax
import jax.numpy as jnp
import numpy as np
from jax import lax
from jax.experimental import pallas as pl
from jax.experimental.pallas import tpu as pltpu

F32 = jnp.float32
BF16 = jnp.bfloat16

N_HEADS = 8
QK_NOPE_DIM = 128
QK_ROPE_DIM = 64
QK_HEAD_DIM = QK_NOPE_DIM + QK_ROPE_DIM
V_HEAD_DIM = 128
Q_LORA_RANK = 384
KV_LORA_RANK = 256
GRID_W = 64
ROPE_THETA = 10000.0
NORM_EPS = 1e-6
MOD_CHUNKS = 6
SM_SCALE_LOG2 = float(QK_HEAD_DIM ** -0.5 * np.log2(np.e))
ATTN_EXP2_SAFE_RANGE = 40.0
BF16_ROUNDING_MARGIN = 1.01

LANES = 128
SUBLANES = 8
VMEM_LIMIT_BYTES = 56 * 1024 * 1024

TOKEN_TILE = 512
ATTN_Q_TILE = 512
ATTN_Q_SUB = 256
ATTN_KV_TILE = 512
ATTN_KV_UNROLL = 16
ATTN_KV_UNROLL_ONLINE = 2
ATTN_LOOKAHEAD = 2
ADALN_N_TILE = 1536
FFN_CHUNK = 256


def _rsqrt_mean(ss, n):
    return lax.rsqrt(ss * (1.0 / n) + NORM_EPS)


def _sigmoid(t):
    return 1.0 / (1.0 + jnp.exp(-t))


def _const_spec(shape):
    nd = len(shape)
    return pl.BlockSpec(shape, lambda *_: (0,) * nd, pipeline_mode=pl.Buffered(1))


def _adaln_kernel(cond_ref, w_ref, b_ref, o_ref):
    c = cond_ref[...]
    a = c * _sigmoid(c)
    o_ref[...] = jnp.dot(a, w_ref[...], preferred_element_type=F32,
                         precision=lax.Precision.HIGHEST) + b_ref[...]


def _adaln(cond, w_mod, b_mod):
    rows, d = cond.shape
    n = w_mod.shape[1]
    return pl.pallas_call(
        _adaln_kernel,
        grid=(n // ADALN_N_TILE,),
        in_specs=[pl.BlockSpec((rows, d), lambda j: (0, 0)),
                  pl.BlockSpec((d, ADALN_N_TILE), lambda j: (0, j)),
                  pl.BlockSpec((1, ADALN_N_TILE), lambda j: (0, j))],
        out_specs=pl.BlockSpec((rows, ADALN_N_TILE), lambda j: (0, j)),
        out_shape=jax.ShapeDtypeStruct((rows, n), F32),
        compiler_params=pltpu.CompilerParams(dimension_semantics=("arbitrary",),
                                             vmem_limit_bytes=VMEM_LIMIT_BYTES),
        name="adaln",
    )(cond, w_mod, b_mod)


def _modulated_norm(x, gain, shift, scale):
    d = x.shape[-1]
    r = _rsqrt_mean(jnp.sum(x * x, axis=-1, keepdims=True), d)
    return (x * r * gain) * (1.0 + scale) + shift


def _rope_pair(t2, table):
    w = t2 * table
    return w + pltpu.roll(w, QK_ROPE_DIM, axis=1)


def _keys_values(h, w_kv_in_ref, kvan_ref, wkvb_ref, gk_ref, table, k_ref, v_ref):
    kvp = jnp.dot(h, w_kv_in_ref[...], preferred_element_type=F32)
    kv_a = kvp[:, :KV_LORA_RANK]
    kr2 = kvp[:, KV_LORA_RANK:]
    r = _rsqrt_mean(jnp.sum(kv_a * kv_a, axis=-1, keepdims=True), KV_LORA_RANK)
    kv_n = (kv_a * r * kvan_ref[...]).astype(BF16)
    kv = jnp.dot(kv_n, wkvb_ref[...], preferred_element_type=F32)
    gk = gk_ref[...]
    ss_rope = 0.5 * jnp.sum(kr2 * kr2, axis=-1, keepdims=True)
    kr = _rope_pair(kr2 * gk[:, QK_NOPE_DIM:], table)[:, :QK_ROPE_DIM]
    for hd in range(N_HEADS):
        k_nope = kv[:, hd * QK_NOPE_DIM:(hd + 1) * QK_NOPE_DIM]
        ss = jnp.sum(k_nope * k_nope, axis=-1, keepdims=True) + ss_rope
        rh = _rsqrt_mean(ss, QK_HEAD_DIM)
        k_ref[hd, :, :QK_NOPE_DIM] = (k_nope * rh * gk[:, :QK_NOPE_DIM]).astype(BF16)
        k_ref[hd, :, QK_NOPE_DIM:] = (kr * rh).astype(BF16)
        v0 = N_HEADS * QK_NOPE_DIM + hd * V_HEAD_DIM
        v_ref[hd] = kv[:, v0:v0 + V_HEAD_DIM].astype(BF16)


def _ctx_kv_kernel(ctx_ref, mod_ref, nmix_ref, w_kv_in_ref, kvan_ref, wkvb_ref, gk_ref,
                   table_ref, k_ref, v_ref):
    d = ctx_ref.shape[-1]
    mod = mod_ref[...]
    h = _modulated_norm(ctx_ref[...], nmix_ref[...], mod[:, :d], mod[:, d:2 * d]).astype(BF16)
    _keys_values(h, w_kv_in_ref, kvan_ref, wkvb_ref, gk_ref, table_ref[...], k_ref, v_ref)


def _ctx_kv(ctx, mod_ctx, nmix, w_kv_in, kvan, wkvb, gk, table):
    b, n, d = ctx.shape
    return pl.pallas_call(
        _ctx_kv_kernel,
        grid=(b,),
        in_specs=[pl.BlockSpec((None, n, d), lambda i: (i, 0, 0)),
                  _const_spec(mod_ctx.shape), _const_spec(nmix.shape),
                  _const_spec(w_kv_in.shape), _const_spec(kvan.shape),
                  _const_spec(wkvb.shape), _const_spec(gk.shape), _const_spec(table.shape)],
        out_specs=[pl.BlockSpec((None, N_HEADS, n, QK_HEAD_DIM), lambda i: (i, 0, 0, 0)),
                   pl.BlockSpec((None, N_HEADS, n, V_HEAD_DIM), lambda i: (i, 0, 0, 0))],
        out_shape=[jax.ShapeDtypeStruct((b, N_HEADS, n, QK_HEAD_DIM), BF16),
                   jax.ShapeDtypeStruct((b, N_HEADS, n, V_HEAD_DIM), BF16)],
        compiler_params=pltpu.CompilerParams(dimension_semantics=("arbitrary",),
                                             vmem_limit_bytes=VMEM_LIMIT_BYTES),
        name="ctx_kv",
    )(ctx, mod_ctx, nmix, w_kv_in, kvan, wkvb, gk, table)


def _in_proj_kernel(x_ref, mod_ref, nmix_ref, w_main_ref, w_qa_ref, w_kv_in_ref, bgate_ref,
                    qan_ref, wqb_ref, kvan_ref, wkvb_ref, gq_ref, gk_ref, table_ref,
                    bx_ref, u_ref, gc_ref, ga_ref, q_ref, k_ref, v_ref):
    d = x_ref.shape[-1]
    mod = mod_ref[...]
    h = _modulated_norm(x_ref[...], nmix_ref[...], mod[:, :d], mod[:, d:2 * d]).astype(BF16)
    table = table_ref[...]

    def proj(j):
        return jnp.dot(h, w_main_ref[:, j * d:(j + 1) * d], preferred_element_type=F32)

    bx_ref[...] = proj(0).astype(BF16)
    u_ref[...] = (proj(1) * proj(2)).astype(BF16)
    bgate = bgate_ref[...]
    gc_ref[...] = _sigmoid(proj(3) + bgate[:, :d]).astype(BF16)
    ga_ref[...] = _sigmoid(proj(4) + bgate[:, d:]).astype(BF16)

    q_a = jnp.dot(h, w_qa_ref[...], preferred_element_type=F32)
    r = _rsqrt_mean(jnp.sum(q_a * q_a, axis=-1, keepdims=True), Q_LORA_RANK)
    q_n = (q_a * r * qan_ref[...]).astype(BF16)
    q = jnp.dot(q_n, wqb_ref[...], preferred_element_type=F32)
    gq = gq_ref[...]
    table_q = table * gq[:, QK_NOPE_DIM:]
    sm_scale = SM_SCALE_LOG2
    hw = 2 * LANES
    for hd in range(N_HEADS):
        q_nope = q[:, hd * hw:hd * hw + QK_NOPE_DIM]
        q_r2 = q[:, hd * hw + QK_NOPE_DIM:(hd + 1) * hw]
        ss = jnp.sum(q_nope * q_nope + 0.5 * (q_r2 * q_r2), axis=-1, keepdims=True)
        rh = _rsqrt_mean(ss, QK_HEAD_DIM) * sm_scale
        q_ref[hd, :, :QK_NOPE_DIM] = (q_nope * rh * gq[:, :QK_NOPE_DIM]).astype(BF16)
        q_ref[hd, :, QK_NOPE_DIM:] = (_rope_pair(q_r2 * rh, table_q)[:, :QK_ROPE_DIM]).astype(BF16)

    _keys_values(h, w_kv_in_ref, kvan_ref, wkvb_ref, gk_ref, table, k_ref, v_ref)


def _in_proj(x, mod_x, nmix, w_main, w_qa, w_kv_in, bgate, qan, wqb, kvan, wkvb, gq, gk, table):
    b, s, d = x.shape
    tm = TOKEN_TILE
    tok = pl.BlockSpec((None, tm, d), lambda bi, i: (bi, i, 0))

    def head_spec(width):
        return pl.BlockSpec((None, N_HEADS, tm, width), lambda bi, i: (bi, 0, i, 0))

    tok_shape = jax.ShapeDtypeStruct((b, s, d), BF16)
    return pl.pallas_call(
        _in_proj_kernel,
        grid=(b, s // tm),
        in_specs=[tok,
                  pl.BlockSpec((None, 1, mod_x.shape[-1]), lambda bi, i: (bi, 0, 0)),
                  _const_spec(nmix.shape), _const_spec(w_main.shape), _const_spec(w_qa.shape),
                  _const_spec(w_kv_in.shape), _const_spec(bgate.shape), _const_spec(qan.shape),
                  _const_spec(wqb.shape), _const_spec(kvan.shape), _const_spec(wkvb.shape),
                  _const_spec(gq.shape), _const_spec(gk.shape),
                  pl.BlockSpec((tm, 2 * QK_ROPE_DIM), lambda bi, i: (i, 0))],
        out_specs=[tok, tok, tok, tok,
                   head_spec(QK_HEAD_DIM), head_spec(QK_HEAD_DIM), head_spec(V_HEAD_DIM)],
        out_shape=[tok_shape, tok_shape, tok_shape, tok_shape,
                   jax.ShapeDtypeStruct((b, N_HEADS, s, QK_HEAD_DIM), BF16),
                   jax.ShapeDtypeStruct((b, N_HEADS, s, QK_HEAD_DIM), BF16),
                   jax.ShapeDtypeStruct((b, N_HEADS, s, V_HEAD_DIM), BF16)],
        compiler_params=pltpu.CompilerParams(dimension_semantics=("arbitrary", "arbitrary"),
                                             vmem_limit_bytes=VMEM_LIMIT_BYTES),
        name="in_proj",
    )(x, mod_x, nmix, w_main, w_qa, w_kv_in, bgate, qan, wqb, kvan, wkvb, gq, gk, table)


def _attention_kernel(bound_ref, q_ref, kc_ref, vc_ref, kl_ref, vl_ref, o_ref,
                      m_ref, l_ref, acc_ref):
    tq = q_ref.shape[0]
    l_ref[...] = jnp.zeros(l_ref.shape, F32)
    acc_ref[...] = jnp.zeros(acc_ref.shape, F32)

    def scores(unit):
        k_ref, _, k0, tk, r0 = unit
        return lax.dot_general(q_ref[r0:r0 + ATTN_Q_SUB, :], k_ref[pl.ds(k0, tk), :],
                               (((1,), (1,)), ((), ())), preferred_element_type=F32)

    def finish_bounded(unit, s):
        _, v_ref, k0, tk, r0 = unit
        rows = slice(r0, r0 + ATTN_Q_SUB)
        ps = [jnp.exp2(s[:, c:c + LANES]) for c in range(0, tk, LANES)]
        l_ref[rows, :] += functools.reduce(jnp.add, ps)
        p = jnp.concatenate([x.astype(BF16) for x in ps], axis=1)
        acc_ref[rows, :] += jnp.dot(p, v_ref[pl.ds(k0, tk), :], preferred_element_type=F32)

    def finish_online(unit, s):
        _, v_ref, k0, tk, r0 = unit
        rows = slice(r0, r0 + ATTN_Q_SUB)
        cols = [s[:, c:c + LANES] for c in range(0, tk, LANES)]
        m_old = m_ref[rows, :]
        row_max = jnp.max(functools.reduce(jnp.maximum, cols), axis=-1, keepdims=True)
        m_new = jnp.maximum(m_old, row_max)
        alpha = jnp.exp2(m_old - m_new)
        ps = [jnp.exp2(col - m_new) for col in cols]
        l_ref[rows, :] = alpha * l_ref[rows, :] + functools.reduce(jnp.add, ps)
        p = jnp.concatenate([x.astype(BF16) for x in ps], axis=1)
        pv = jnp.dot(p, v_ref[pl.ds(k0, tk), :], preferred_element_type=F32)
        acc_ref[rows, :] = alpha * acc_ref[rows, :] + pv
        m_ref[rows, :] = m_new

    def run(units, finish):
        pending = [scores(u) for u in units[:ATTN_LOOKAHEAD]]
        for i, unit in enumerate(units):
            if i + ATTN_LOOKAHEAD < len(units):
                pending.append(scores(units[i + ATTN_LOOKAHEAD]))
            finish(unit, pending[i])

    def units_of(k_ref, v_ref, starts, tk):
        return [(k_ref, v_ref, k0, tk, r0) for k0 in starts for r0 in range(0, tq, ATTN_Q_SUB)]

    def all_keys(finish, unroll):
        tk = ATTN_KV_TILE
        span = tk * unroll
        n_groups = kl_ref.shape[0] // span
        run(units_of(kc_ref, vc_ref, [0], kc_ref.shape[0])
            + units_of(kl_ref, vl_ref, [u * tk for u in range(unroll)], tk), finish)

        def body(j, carry):
            base = pl.multiple_of(j * span, span)
            run(units_of(kl_ref, vl_ref, [base + u * tk for u in range(unroll)], tk), finish)
            return carry

        if n_groups > 1:
            lax.fori_loop(1, n_groups, body, 0)

    bounded = bound_ref[0] <= ATTN_EXP2_SAFE_RANGE

    @pl.when(bounded)
    def _():
        all_keys(finish_bounded, ATTN_KV_UNROLL)

    @pl.when(jnp.logical_not(bounded))
    def _():
        m_ref[...] = jnp.full(m_ref.shape, -jnp.inf, F32)
        all_keys(finish_online, ATTN_KV_UNROLL_ONLINE)

    l = jnp.sum(l_ref[...], axis=-1, keepdims=True)
    o_ref[...] = (acc_ref[...] / l).astype(o_ref.dtype)


def _attention(score_bound, q, k_ctx, v_ctx, k_lat, v_lat):
    b, nh, s, dk = q.shape
    nc = k_ctx.shape[2]
    dv = v_lat.shape[-1]
    assert dv == LANES
    tq = ATTN_Q_TILE

    def kv_spec(n, width):
        return pl.BlockSpec((None, None, n, width), lambda bi, hi, i: (bi, hi, 0, 0))

    return pl.pallas_call(
        _attention_kernel,
        grid=(b, nh, s // tq),
        in_specs=[pl.BlockSpec(memory_space=pltpu.SMEM),
                  pl.BlockSpec((None, None, tq, dk), lambda bi, hi, i: (bi, hi, i, 0)),
                  kv_spec(nc, dk), kv_spec(nc, dv), kv_spec(s, dk), kv_spec(s, dv)],
        out_specs=pl.BlockSpec((None, tq, dv), lambda bi, hi, i: (bi, i, hi)),
        out_shape=jax.ShapeDtypeStruct((b, s, nh * dv), BF16),
        scratch_shapes=[pltpu.VMEM((tq, LANES), F32), pltpu.VMEM((tq, LANES), F32),
                        pltpu.VMEM((tq, dv), F32)],
        compiler_params=pltpu.CompilerParams(
            dimension_semantics=("arbitrary", "arbitrary", "arbitrary"),
            vmem_limit_bytes=VMEM_LIMIT_BYTES),
        name="attention",
    )(score_bound, q, k_ctx, v_ctx, k_lat, v_lat)


def _merge_kernel(x_ref, mod_ref, bx_ref, u_ref, u_prev_ref, u_next_ref, gc_ref, ga_ref,
                  attn_ref, convw_ref, convb_ref, w_conv_out_ref, w_attn_o_ref, w_out_ref,
                  o_ref):
    i = pl.program_id(1)
    tm, d = x_ref.shape
    u = u_ref[...].astype(F32)
    prev_row = u_prev_ref[SUBLANES - 1:SUBLANES, :].astype(F32) * (i > 0).astype(F32)
    next_row = u_next_ref[0:1, :].astype(F32) * (i < pl.num_programs(1) - 1).astype(F32)
    row = lax.broadcasted_iota(jnp.int32, (tm, d), 0)
    u_m1 = jnp.where(row == 0, prev_row, pltpu.roll(u, 1, axis=0))
    u_p1 = jnp.where(row == tm - 1, next_row, pltpu.roll(u, tm - 1, axis=0))
    cw = convw_ref[...]
    conv = u_m1 * cw[0:1, :] + u * cw[1:2, :] + u_p1 * cw[2:3, :] + convb_ref[...]
    z = (bx_ref[...].astype(F32) * conv).astype(BF16)
    y_conv = jnp.dot(z, w_conv_out_ref[...], preferred_element_type=F32)
    y_attn = jnp.dot(attn_ref[...], w_attn_o_ref[...], preferred_element_type=F32)
    merged = (gc_ref[...].astype(F32) * y_conv + ga_ref[...].astype(F32) * y_attn).astype(BF16)
    y = jnp.dot(merged, w_out_ref[...], preferred_element_type=F32)
    g1 = mod_ref[:, 2 * d:3 * d]
    o_ref[...] = x_ref[...] + g1 * y


def _merge(x, mod_x, bx, u, gc, ga, attn, conv_w, conv_b, w_conv_out, w_attn_o, w_out):
    b, s, d = x.shape
    tm = TOKEN_TILE
    hb = tm // SUBLANES
    last_hb = s // SUBLANES - 1
    tok = pl.BlockSpec((None, tm, d), lambda bi, i: (bi, i, 0))
    prev = pl.BlockSpec((None, SUBLANES, d), lambda bi, i: (bi, jnp.maximum(i * hb - 1, 0), 0))
    nxt = pl.BlockSpec((None, SUBLANES, d),
                       lambda bi, i: (bi, jnp.minimum((i + 1) * hb, last_hb), 0))
    return pl.pallas_call(
        _merge_kernel,
        grid=(b, s // tm),
        in_specs=[tok,
                  pl.BlockSpec((None, 1, mod_x.shape[-1]), lambda bi, i: (bi, 0, 0)),
                  tok, tok, prev, nxt, tok, tok, tok,
                  _const_spec(conv_w.shape), _const_spec(conv_b.shape),
                  _const_spec(w_conv_out.shape), _const_spec(w_attn_o.shape),
                  _const_spec(w_out.shape)],
        out_specs=tok,
        out_shape=jax.ShapeDtypeStruct((b, s, d), F32),
        compiler_params=pltpu.CompilerParams(dimension_semantics=("arbitrary", "arbitrary"),
                                             vmem_limit_bytes=VMEM_LIMIT_BYTES),
        name="merge",
    )(x, mod_x, bx, u, u, u, gc, ga, attn, conv_w, conv_b, w_conv_out, w_attn_o, w_out)


def _ffn_kernel(x_ref, mod_ref, nffn_ref, w_in_ref, w_out_ref, o_ref, act_ref):
    d = x_ref.shape[-1]
    d_ff = w_out_ref.shape[0]
    x = x_ref[...]
    mod = mod_ref[...]
    h = _modulated_norm(x, nffn_ref[...], mod[:, 3 * d:4 * d], mod[:, 4 * d:5 * d]).astype(BF16)
    for j in range(d_ff // FFN_CHUNK):
        c0 = j * FFN_CHUNK
        gate = jnp.dot(h, w_in_ref[:, c0:c0 + FFN_CHUNK], preferred_element_type=F32)
        up = jnp.dot(h, w_in_ref[:, d_ff + c0:d_ff + c0 + FFN_CHUNK],
                     preferred_element_type=F32)
        act_ref[:, c0:c0 + FFN_CHUNK] = (gate * _sigmoid(gate) * up).astype(BF16)
    y = jnp.dot(act_ref[...], w_out_ref[...], preferred_element_type=F32)
    o_ref[...] = x + mod[:, 5 * d:6 * d] * y


def _ffn(x, mod_x, nffn, w_ffn_in, w_ffn_out):
    b, s, d = x.shape
    tm = TOKEN_TILE
    d_ff = w_ffn_out.shape[0]
    tok = pl.BlockSpec((None, tm, d), lambda bi, i: (bi, i, 0))
    return pl.pallas_call(
        _ffn_kernel,
        grid=(b, s // tm),
        in_specs=[tok,
                  pl.BlockSpec((None, 1, mod_x.shape[-1]), lambda bi, i: (bi, 0, 0)),
                  _const_spec(nffn.shape), _const_spec(w_ffn_in.shape),
                  _const_spec(w_ffn_out.shape)],
        out_specs=tok,
        out_shape=jax.ShapeDtypeStruct((b, s, d), F32),
        scratch_shapes=[pltpu.VMEM((tm, d_ff), BF16)],
        compiler_params=pltpu.CompilerParams(dimension_semantics=("arbitrary", "arbitrary"),
                                             vmem_limit_bytes=VMEM_LIMIT_BYTES),
        name="ffn",
    )(x, mod_x, nffn, w_ffn_in, w_ffn_out)


def _rope_table(seq_len):
    quarter = QK_ROPE_DIM // 4
    freqs = ROPE_THETA ** (-jnp.arange(quarter, dtype=F32) / quarter)
    t = jnp.arange(seq_len)
    ang = [(t // GRID_W)[:, None].astype(F32) * freqs,
           (t % GRID_W)[:, None].astype(F32) * freqs]
    cos = [jnp.cos(a) for a in ang]
    sin = [jnp.sin(a) for a in ang]
    c = jnp.concatenate([cos[0], cos[0], cos[1], cos[1]], axis=-1)
    s = jnp.concatenate([-sin[0], sin[0], -sin[1], sin[1]], axis=-1)
    return jnp.concatenate([c, s], axis=-1)


def _swap_halves(t):
    q = QK_ROPE_DIM // 4
    return jnp.concatenate([t[..., q:2 * q], t[..., :q], t[..., 3 * q:], t[..., 2 * q:3 * q]],
                           axis=-1)


def _norm_gain_pair(g):
    rope = g[QK_NOPE_DIM:]
    return jnp.concatenate([g[:QK_NOPE_DIM], rope, _swap_halves(rope)])[None, :].astype(F32)


def kernel(x, c, ctx, c_ctx, w_mod, b_mod, norm_mix, norm_ffn, w_in, b_gate, conv_w, conv_b,
           w_conv_out, q_a_norm, w_q_b, kv_a_norm, w_kv_b, q_norm, k_norm, w_attn_o, w_out,
           w_ffn_in, w_ffn_out):
    depth = w_mod.shape[0]
    assert depth == 1, "context stream update is only needed between layers"
    b, s, d = x.shape
    n_ctx = ctx.shape[1]
    assert s % TOKEN_TILE == 0 and s % ATTN_Q_TILE == 0 and s % ATTN_KV_TILE == 0
    assert b + 1 <= SUBLANES

    cond = jnp.zeros((SUBLANES, d), F32).at[:b].set(c).at[b].set(c_ctx)
    mod = _adaln(cond, w_mod[0], b_mod[0][None, :])
    mod_x = mod[:b, None, :]
    mod_ctx = mod[b:b + 1]

    wi = w_in[0]
    o_q = 3 * d
    o_kv = o_q + Q_LORA_RANK
    o_kr = o_kv + KV_LORA_RANK
    o_gc = o_kr + QK_ROPE_DIM
    w_main = jnp.concatenate([wi[:, :3 * d], wi[:, o_gc:]], axis=1).astype(BF16)
    w_qa = wi[:, o_q:o_kv].astype(BF16)
    w_kr = wi[:, o_kr:o_gc]
    w_kv_in = jnp.concatenate([wi[:, o_kv:o_kr], w_kr, _swap_halves(w_kr)], axis=1).astype(BF16)
    wq = w_q_b[0].reshape(Q_LORA_RANK, N_HEADS, QK_HEAD_DIM)
    wq = jnp.concatenate([wq, _swap_halves(wq[..., QK_NOPE_DIM:])], axis=-1)
    wqb = wq.reshape(Q_LORA_RANK, N_HEADS * 2 * LANES).astype(BF16)
    wkv = w_kv_b[0].reshape(KV_LORA_RANK, N_HEADS, QK_NOPE_DIM + V_HEAD_DIM)
    wkvb = jnp.concatenate([wkv[..., :QK_NOPE_DIM].reshape(KV_LORA_RANK, -1),
                            wkv[..., QK_NOPE_DIM:].reshape(KV_LORA_RANK, -1)],
                           axis=1).astype(BF16)
    gq = _norm_gain_pair(q_norm[0])
    gk = _norm_gain_pair(k_norm[0])
    table = _rope_table(s)
    table_ctx = jnp.concatenate([jnp.ones((n_ctx, QK_ROPE_DIM), F32),
                                 jnp.zeros((n_ctx, QK_ROPE_DIM), F32)], axis=1)
    nmix = norm_mix[0][None, :]
    kvan = kv_a_norm[0][None, :]

    k_ctx, v_ctx = _ctx_kv(ctx, mod_ctx, nmix, w_kv_in, kvan, wkvb, gk, table_ctx)
    bx, u, gc, ga, q, k_lat, v_lat = _in_proj(
        x, mod_x, nmix, w_main, w_qa, w_kv_in, b_gate[0][None, :], q_a_norm[0][None, :], wqb,
        kvan, wkvb, gq, gk, table)
    score_bound = (QK_HEAD_DIM * SM_SCALE_LOG2 * BF16_ROUNDING_MARGIN
                   * jnp.max(jnp.abs(q_norm[0])) * jnp.max(jnp.abs(k_norm[0]))).reshape(1)
    attn = _attention(score_bound.astype(F32), q, k_ctx, v_ctx, k_lat, v_lat)
    x_mid = _merge(x, mod_x, bx, u, gc, ga, attn, conv_w[0], conv_b[0][None, :],
                   w_conv_out[0].astype(BF16), w_attn_o[0].astype(BF16), w_out[0].astype(BF16))
    return _ffn(x_mid, mod_x, norm_ffn[0][None, :], w_ffn_in[0].astype(BF16),
                w_ffn_out[0].astype(BF16))
```

```python
import functools

import jax
import jax.numpy as jnp
import numpy as np
from jax import lax
from jax.experimental import pallas as pl
from jax.experimental.pallas import tpu as pltpu

F32 = jnp.float32
BF16 = jnp.bfloat16

N_HEADS = 8
QK_NOPE_DIM = 128
QK_ROPE_DIM = 64
QK_HEAD_DIM = QK_NOPE_DIM + QK_ROPE_DIM
V_HEAD_DIM = 128
Q_LORA_RANK = 384
KV_LORA_RANK = 256
GRID_W = 64
ROPE_THETA = 10000.0
NORM_EPS = 1e-6
MOD_CHUNKS = 6
SM_SCALE_LOG2 = float(QK_HEAD_DIM ** -0.5 * np.log2(np.e))
ATTN_EXP2_SAFE_RANGE = 40.0
BF16_ROUNDING_MARGIN = 1.01

LANES = 128
SUBLANES = 8
VMEM_LIMIT_BYTES = 56 * 1024 * 1024

TOKEN_TILE = 512
ATTN_Q_TILE = 512
ATTN_Q_SUB = 256
ATTN_KV_TILE = 512
ATTN_KV_UNROLL = 16
ATTN_KV_UNROLL_ONLINE = 2
ATTN_LOOKAHEAD = 2
ADALN_N_TILE = 1536
FFN_CHUNK = 256


def _rsqrt_mean(ss, n):
    return lax.rsqrt(ss * (1.0 / n) + NORM_EPS)


def _sigmoid(t):
    return 1.0 / (1.0 + jnp.exp(-t))


def _const_spec(shape):
    nd = len(shape)
    return pl.BlockSpec(shape, lambda *_: (0,) * nd, pipeline_mode=pl.Buffered(1))


def _adaln_kernel(cond_ref, w_ref, b_ref, o_ref):
    c = cond_ref[...]
    a = c * _sigmoid(c)
    o_ref[...] = jnp.dot(a, w_ref[...], preferred_element_type=F32,
                         precision=lax.Precision.HIGHEST) + b_ref[...]


def _adaln(cond, w_mod, b_mod):
    rows, d = cond.shape
    n = w_mod.shape[-1]
    return pl.pallas_call(
        _adaln_kernel,
        grid=(n // ADALN_N_TILE,),
        in_specs=[pl.BlockSpec((rows, d), lambda j: (0, 0)),
                  pl.BlockSpec((None, d, ADALN_N_TILE), lambda j: (0, 0, j)),
                  pl.BlockSpec((1, ADALN_N_TILE), lambda j: (0, j))],
        out_specs=pl.BlockSpec((rows, ADALN_N_TILE), lambda j: (0, j)),
        out_shape=jax.ShapeDtypeStruct((rows, n), F32),
        compiler_params=pltpu.CompilerParams(dimension_semantics=("arbitrary",),
                                             vmem_limit_bytes=VMEM_LIMIT_BYTES),
        name="adaln",
    )(cond, w_mod, b_mod)


def _modulated_norm(x, gain, shift, scale):
    d = x.shape[-1]
    r = _rsqrt_mean(jnp.sum(x * x, axis=-1, keepdims=True), d)
    return (x * r * gain) * (1.0 + scale) + shift


def _rope_pair(t2, table):
    w = t2 * table
    return w + pltpu.roll(w, QK_ROPE_DIM, axis=1)


def _kv_matmuls(h, w_kv_in_ref, kvan_ref, wkvb_ref):
    kvp = jnp.dot(h, w_kv_in_ref[...], preferred_element_type=F32)
    kv_a = kvp[:, :KV_LORA_RANK]
    r = _rsqrt_mean(jnp.sum(kv_a * kv_a, axis=-1, keepdims=True), KV_LORA_RANK)
    kv_n = (kv_a * r * kvan_ref[...]).astype(BF16)
    return jnp.dot(kv_n, wkvb_ref[...], preferred_element_type=F32), kvp[:, KV_LORA_RANK:]


def _kv_finish(kv, kr2, gk_ref, table, k_ref, v_ref):
    gk = gk_ref[...]
    ss_rope = 0.5 * jnp.sum(kr2 * kr2, axis=-1, keepdims=True)
    kr = _rope_pair(kr2 * gk[:, QK_NOPE_DIM:], table)[:, :QK_ROPE_DIM]
    for hd in range(N_HEADS):
        k_nope = kv[:, hd * QK_NOPE_DIM:(hd + 1) * QK_NOPE_DIM]
        ss = jnp.sum(k_nope * k_nope, axis=-1, keepdims=True) + ss_rope
        rh = _rsqrt_mean(ss, QK_HEAD_DIM)
        k_ref[hd, :, :QK_NOPE_DIM] = (k_nope * rh * gk[:, :QK_NOPE_DIM]).astype(BF16)
        k_ref[hd, :, QK_NOPE_DIM:] = (kr * rh).astype(BF16)
        v0 = N_HEADS * QK_NOPE_DIM + hd * V_HEAD_DIM
        v_ref[hd] = kv[:, v0:v0 + V_HEAD_DIM].astype(BF16)


def _ctx_kv_kernel(ctx_ref, mod_ref, nmix_ref, w_kv_in_ref, kvan_ref, wkvb_ref, gk_ref,
                   table_ref, k_ref, v_ref):
    d = ctx_ref.shape[-1]
    mod = mod_ref[...]
    h = _modulated_norm(ctx_ref[...], nmix_ref[...], mod[:, :d], mod[:, d:2 * d]).astype(BF16)
    kv, kr2 = _kv_matmuls(h, w_kv_in_ref, kvan_ref, wkvb_ref)
    _kv_finish(kv, kr2, gk_ref, table_ref[...], k_ref, v_ref)


def _ctx_kv(ctx, mod_ctx, nmix, w_kv_in, kvan, wkvb, gk, table):
    b, n, d = ctx.shape
    return pl.pallas_call(
        _ctx_kv_kernel,
        grid=(b,),
        in_specs=[pl.BlockSpec((None, n, d), lambda i: (i, 0, 0)),
                  _const_spec(mod_ctx.shape), _const_spec(nmix.shape),
                  _const_spec(w_kv_in.shape), _const_spec(kvan.shape),
                  _const_spec(wkvb.shape), _const_spec(gk.shape), _const_spec(table.shape)],
        out_specs=[pl.BlockSpec((None, N_HEADS, n, QK_HEAD_DIM), lambda i: (i, 0, 0, 0)),
                   pl.BlockSpec((None, N_HEADS, n, V_HEAD_DIM), lambda i: (i, 0, 0, 0))],
        out_shape=[jax.ShapeDtypeStruct((b, N_HEADS, n, QK_HEAD_DIM), BF16),
                   jax.ShapeDtypeStruct((b, N_HEADS, n, V_HEAD_DIM), BF16)],
        compiler_params=pltpu.CompilerParams(dimension_semantics=("arbitrary",),
                                             vmem_limit_bytes=VMEM_LIMIT_BYTES),
        name="ctx_kv",
    )(ctx, mod_ctx, nmix, w_kv_in, kvan, wkvb, gk, table)


def _in_proj_kernel(x_ref, mod_ref, nmix_ref, w_main_ref, w_qa_ref, w_kv_in_ref, bgate_ref,
                    qan_ref, wqb_ref, kvan_ref, wkvb_ref, gq_ref, gk_ref, table_ref,
                    bx_ref, u_ref, gc_ref, ga_ref, q_ref, k_ref, v_ref):
    d = x_ref.shape[-1]
    mod = mod_ref[...]
    h = _modulated_norm(x_ref[...], nmix_ref[...], mod[:, :d], mod[:, d:2 * d]).astype(BF16)
    table = table_ref[...]

    def proj(j):
        return jnp.dot(h, w_main_ref[:, j * d:(j + 1) * d], preferred_element_type=F32)

    q_a = jnp.dot(h, w_qa_ref[...], preferred_element_type=F32)
    r = _rsqrt_mean(jnp.sum(q_a * q_a, axis=-1, keepdims=True), Q_LORA_RANK)
    q_n = (q_a * r * qan_ref[...]).astype(BF16)
    q = jnp.dot(q_n, wqb_ref[...], preferred_element_type=F32)
    kv, kr2 = _kv_matmuls(h, w_kv_in_ref, kvan_ref, wkvb_ref)

    bx_ref[...] = proj(0).astype(BF16)
    u_ref[...] = (proj(1) * proj(2)).astype(BF16)
    bgate = bgate_ref[...]
    gc_ref[...] = _sigmoid(proj(3) + bgate[:, :d]).astype(BF16)
    ga_ref[...] = _sigmoid(proj(4) + bgate[:, d:]).astype(BF16)

    gq = gq_ref[...]
    table_q = table * gq[:, QK_NOPE_DIM:]
    sm_scale = SM_SCALE_LOG2
    hw = 2 * LANES
    for hd in range(N_HEADS):
        q_nope = q[:, hd * hw:hd * hw + QK_NOPE_DIM]
        q_r2 = q[:, hd * hw + QK_NOPE_DIM:(hd + 1) * hw]
        ss = jnp.sum(q_nope * q_nope + 0.5 * (q_r2 * q_r2), axis=-1, keepdims=True)
        rh = _rsqrt_mean(ss, QK_HEAD_DIM) * sm_scale
        q_ref[hd, :, :QK_NOPE_DIM] = (q_nope * rh * gq[:, :QK_NOPE_DIM]).astype(BF16)
        q_ref[hd, :, QK_NOPE_DIM:] = (_rope_pair(q_r2 * rh, table_q)[:, :QK_ROPE_DIM]).astype(BF16)

    _kv_finish(kv, kr2, gk_ref, table, k_ref, v_ref)


def _in_proj(x, mod_x, nmix, w_main, w_qa, w_kv_in, bgate, qan, wqb, kvan, wkvb, gq, gk, table):
    b, s, d = x.shape
    tm = TOKEN_TILE
    tok = pl.BlockSpec((None, tm, d), lambda bi, i: (bi, i, 0))

    def head_spec(width):
        return pl.BlockSpec((None, N_HEADS, tm, width), lambda bi, i: (bi, 0, i, 0))

    tok_shape = jax.ShapeDtypeStruct((b, s, d), BF16)
    return pl.pallas_call(
        _in_proj_kernel,
        grid=(b, s // tm),
        in_specs=[tok,
                  pl.BlockSpec((None, 1, mod_x.shape[-1]), lambda bi, i: (bi, 0, 0)),
                  _const_spec(nmix.shape), _const_spec(w_main.shape), _const_spec(w_qa.shape),
                  _const_spec(w_kv_in.shape), _const_spec(bgate.shape), _const_spec(qan.shape),
                  _const_spec(wqb.shape), _const_spec(kvan.shape), _const_spec(wkvb.shape),
                  _const_spec(gq.shape), _const_spec(gk.shape),
                  pl.BlockSpec((tm, 2 * QK_ROPE_DIM), lambda bi, i: (i, 0))],
        out_specs=[tok, tok, tok, tok,
                   head_spec(QK_HEAD_DIM), head_spec(QK_HEAD_DIM), head_spec(V_HEAD_DIM)],
        out_shape=[tok_shape, tok_shape, tok_shape, tok_shape,
                   jax.ShapeDtypeStruct((b, N_HEADS, s, QK_HEAD_DIM), BF16),
                   jax.ShapeDtypeStruct((b, N_HEADS, s, QK_HEAD_DIM), BF16),
                   jax.ShapeDtypeStruct((b, N_HEADS, s, V_HEAD_DIM), BF16)],
        compiler_params=pltpu.CompilerParams(dimension_semantics=("arbitrary", "arbitrary"),
                                             vmem_limit_bytes=VMEM_LIMIT_BYTES),
        name="in_proj",
    )(x, mod_x, nmix, w_main, w_qa, w_kv_in, bgate, qan, wqb, kvan, wkvb, gq, gk, table)


def _attention_kernel(bound_ref, q_ref, kc_ref, vc_ref, kl_ref, vl_ref, o_ref,
                      m_ref, l_ref, acc_ref):
    tq = q_ref.shape[0]
    l_ref[...] = jnp.zeros(l_ref.shape, F32)
    acc_ref[...] = jnp.zeros(acc_ref.shape, F32)

    def scores(unit):
        k_ref, _, k0, tk, r0 = unit
        return lax.dot_general(q_ref[r0:r0 + ATTN_Q_SUB, :], k_ref[pl.ds(k0, tk), :],
                               (((1,), (1,)), ((), ())), preferred_element_type=F32)

    def finish_bounded(unit, s):
        _, v_ref, k0, tk, r0 = unit
        rows = slice(r0, r0 + ATTN_Q_SUB)
        ps = [jnp.exp2(s[:, c:c + LANES]) for c in range(0, tk, LANES)]
        l_ref[rows, :] += functools.reduce(jnp.add, ps)
        p = jnp.concatenate([x.astype(BF16) for x in ps], axis=1)
        acc_ref[rows, :] += jnp.dot(p, v_ref[pl.ds(k0, tk), :], preferred_element_type=F32)

    def finish_online(unit, s):
        _, v_ref, k0, tk, r0 = unit
        rows = slice(r0, r0 + ATTN_Q_SUB)
        cols = [s[:, c:c + LANES] for c in range(0, tk, LANES)]
        m_old = m_ref[rows, :]
        row_max = jnp.max(functools.reduce(jnp.maximum, cols), axis=-1, keepdims=True)
        m_new = jnp.maximum(m_old, row_max)
        alpha = jnp.exp2(m_old - m_new)
        ps = [jnp.exp2(col - m_new) for col in cols]
        l_ref[rows, :] = alpha * l_ref[rows, :] + functools.reduce(jnp.add, ps)
        p = jnp.concatenate([x.astype(BF16) for x in ps], axis=1)
        pv = jnp.dot(p, v_ref[pl.ds(k0, tk), :], preferred_element_type=F32)
        acc_ref[rows, :] = alpha * acc_ref[rows, :] + pv
        m_ref[rows, :] = m_new

    def run(units, finish):
        pending = [scores(u) for u in units[:ATTN_LOOKAHEAD]]
        for i, unit in enumerate(units):
            if i + ATTN_LOOKAHEAD < len(units):
                pending.append(scores(units[i + ATTN_LOOKAHEAD]))
            finish(unit, pending[i])

    def units_of(k_ref, v_ref, starts, tk):
        return [(k_ref, v_ref, k0, tk, r0) for k0 in starts for r0 in range(0, tq, ATTN_Q_SUB)]

    def all_keys(finish, unroll):
        tk = ATTN_KV_TILE
        span = tk * unroll
        n_groups = kl_ref.shape[0] // span
        run(units_of(kc_ref, vc_ref, [0], kc_ref.shape[0])
            + units_of(kl_ref, vl_ref, [u * tk for u in range(unroll)], tk), finish)

        def body(j, carry):
            base = pl.multiple_of(j * span, span)
            run(units_of(kl_ref, vl_ref, [base + u * tk for u in range(unroll)], tk), finish)
            return carry

        if n_groups > 1:
            lax.fori_loop(1, n_groups, body, 0)

    bounded = bound_ref[0] <= ATTN_EXP2_SAFE_RANGE

    @pl.when(bounded)
    def _():
        all_keys(finish_bounded, ATTN_KV_UNROLL)

    @pl.when(jnp.logical_not(bounded))
    def _():
        m_ref[...] = jnp.full(m_ref.shape, -jnp.inf, F32)
        all_keys(finish_online, ATTN_KV_UNROLL_ONLINE)

    l = jnp.sum(l_ref[...], axis=-1, keepdims=True)
    o_ref[...] = (acc_ref[...] / l).astype(o_ref.dtype)


def _attention(score_bound, q, k_ctx, v_ctx, k_lat, v_lat):
    b, nh, s, dk = q.shape
    nc = k_ctx.shape[2]
    dv = v_lat.shape[-1]
    assert dv == LANES
    tq = ATTN_Q_TILE

    def kv_spec(n, width):
        return pl.BlockSpec((None, None, n, width), lambda bi, hi, i: (bi, hi, 0, 0))

    return pl.pallas_call(
        _attention_kernel,
        grid=(b, nh, s // tq),
        in_specs=[pl.BlockSpec(memory_space=pltpu.SMEM),
                  pl.BlockSpec((None, None, tq, dk), lambda bi, hi, i: (bi, hi, i, 0)),
                  kv_spec(nc, dk), kv_spec(nc, dv), kv_spec(s, dk), kv_spec(s, dv)],
        out_specs=pl.BlockSpec((None, tq, dv), lambda bi, hi, i: (bi, i, hi)),
        out_shape=jax.ShapeDtypeStruct((b, s, nh * dv), BF16),
        scratch_shapes=[pltpu.VMEM((tq, LANES), F32), pltpu.VMEM((tq, LANES), F32),
                        pltpu.VMEM((tq, dv), F32)],
        compiler_params=pltpu.CompilerParams(
            dimension_semantics=("arbitrary", "arbitrary", "arbitrary"),
            vmem_limit_bytes=VMEM_LIMIT_BYTES),
        name="attention",
    )(score_bound, q, k_ctx, v_ctx, k_lat, v_lat)


def _merge_kernel(x_ref, mod_ref, bx_ref, u_ref, u_prev_ref, u_next_ref, gc_ref, ga_ref,
                  attn_ref, convw_ref, convb_ref, w_conv_out_ref, w_attn_o_ref, w_out_ref,
                  o_ref):
    i = pl.program_id(1)
    tm, d = x_ref.shape
    y_attn = jnp.dot(attn_ref[...], w_attn_o_ref[...], preferred_element_type=F32)
    u = u_ref[...].astype(F32)
    prev_row = u_prev_ref[SUBLANES - 1:SUBLANES, :].astype(F32) * (i > 0).astype(F32)
    next_row = u_next_ref[0:1, :].astype(F32) * (i < pl.num_programs(1) - 1).astype(F32)
    row = lax.broadcasted_iota(jnp.int32, (tm, d), 0)
    u_m1 = jnp.where(row == 0, prev_row, pltpu.roll(u, 1, axis=0))
    u_p1 = jnp.where(row == tm - 1, next_row, pltpu.roll(u, tm - 1, axis=0))
    cw = convw_ref[...]
    conv = u_m1 * cw[0:1, :] + u * cw[1:2, :] + u_p1 * cw[2:3, :] + convb_ref[...]
    z = (bx_ref[...].astype(F32) * conv).astype(BF16)
    y_conv = jnp.dot(z, w_conv_out_ref[...], preferred_element_type=F32)
    merged = (gc_ref[...].astype(F32) * y_conv + ga_ref[...].astype(F32) * y_attn).astype(BF16)
    y = jnp.dot(merged, w_out_ref[...], preferred_element_type=F32)
    g1 = mod_ref[:, 2 * d:3 * d]
    o_ref[...] = x_ref[...] + g1 * y


def _merge(x, mod_x, bx, u, gc, ga, attn, conv_w, conv_b, w_conv_out, w_attn_o, w_out):
    b, s, d = x.shape
    tm = TOKEN_TILE
    hb = tm // SUBLANES
    last_hb = s // SUBLANES - 1
    tok = pl.BlockSpec((None, tm, d), lambda bi, i: (bi, i, 0))
    prev = pl.BlockSpec((None, SUBLANES, d), lambda bi, i: (bi, jnp.maximum(i * hb - 1, 0), 0))
    nxt = pl.BlockSpec((None, SUBLANES, d),
                       lambda bi, i: (bi, jnp.minimum((i + 1) * hb, last_hb), 0))
    return pl.pallas_call(
        _merge_kernel,
        grid=(b, s // tm),
        in_specs=[tok,
                  pl.BlockSpec((None, 1, mod_x.shape[-1]), lambda bi, i: (bi, 0, 0)),
                  tok, tok, prev, nxt, tok, tok, tok,
                  _const_spec(conv_w.shape), _const_spec(conv_b.shape),
                  _const_spec(w_conv_out.shape), _const_spec(w_attn_o.shape),
                  _const_spec(w_out.shape)],
        out_specs=tok,
        out_shape=jax.ShapeDtypeStruct((b, s, d), F32),
        compiler_params=pltpu.CompilerParams(dimension_semantics=("arbitrary", "arbitrary"),
                                             vmem_limit_bytes=VMEM_LIMIT_BYTES),
        name="merge",
    )(x, mod_x, bx, u, u, u, gc, ga, attn, conv_w, conv_b, w_conv_out, w_attn_o, w_out)


def _ffn_kernel(x_ref, mod_ref, nffn_ref, w_in_ref, w_out_ref, o_ref, act_ref):
    d = x_ref.shape[-1]
    d_ff = w_out_ref.shape[0]
    x = x_ref[...]
    mod = mod_ref[...]
    h = _modulated_norm(x, nffn_ref[...], mod[:, 3 * d:4 * d], mod[:, 4 * d:5 * d]).astype(BF16)
    for j in range(d_ff // FFN_CHUNK):
        c0 = j * FFN_CHUNK
        gate = jnp.dot(h, w_in_ref[:, c0:c0 + FFN_CHUNK], preferred_element_type=F32)
        up = jnp.dot(h, w_in_ref[:, d_ff + c0:d_ff + c0 + FFN_CHUNK],
                     preferred_element_type=F32)
        act_ref[:, c0:c0 + FFN_CHUNK] = (gate * _sigmoid(gate) * up).astype(BF16)
    y = jnp.dot(act_ref[...], w_out_ref[...], preferred_element_type=F32)
    o_ref[...] = x + mod[:, 5 * d:6 * d] * y


def _ffn(x, mod_x, nffn, w_ffn_in, w_ffn_out):
    b, s, d = x.shape
    tm = TOKEN_TILE
    d_ff = w_ffn_out.shape[0]
    tok = pl.BlockSpec((None, tm, d), lambda bi, i: (bi, i, 0))
    return pl.pallas_call(
        _ffn_kernel,
        grid=(b, s // tm),
        in_specs=[tok,
                  pl.BlockSpec((None, 1, mod_x.shape[-1]), lambda bi, i: (bi, 0, 0)),
                  _const_spec(nffn.shape), _const_spec(w_ffn_in.shape),
                  _const_spec(w_ffn_out.shape)],
        out_specs=tok,
        out_shape=jax.ShapeDtypeStruct((b, s, d), F32),
        scratch_shapes=[pltpu.VMEM((tm, d_ff), BF16)],
        compiler_params=pltpu.CompilerParams(dimension_semantics=("arbitrary", "arbitrary"),
                                             vmem_limit_bytes=VMEM_LIMIT_BYTES),
        name="ffn",
    )(x, mod_x, nffn, w_ffn_in, w_ffn_out)


def _rope_table(seq_len):
    quarter = QK_ROPE_DIM // 4
    freqs = ROPE_THETA ** (-np.arange(quarter, dtype=np.float64) / quarter)
    t = np.arange(seq_len)
    ang = [(t // GRID_W)[:, None] * freqs, (t % GRID_W)[:, None] * freqs]
    cos = [np.cos(a) for a in ang]
    sin = [np.sin(a) for a in ang]
    c = np.concatenate([cos[0], cos[0], cos[1], cos[1]], axis=-1)
    s = np.concatenate([-sin[0], sin[0], -sin[1], sin[1]], axis=-1)
    return jnp.asarray(np.concatenate([c, s], axis=-1), F32)


def _swap_halves(t):
    q = QK_ROPE_DIM // 4
    return jnp.concatenate([t[..., q:2 * q], t[..., :q], t[..., 3 * q:], t[..., 2 * q:3 * q]],
                           axis=-1)


def _norm_gain_pair(g):
    rope = g[QK_NOPE_DIM:]
    return jnp.concatenate([g[:QK_NOPE_DIM], rope, _swap_halves(rope)])[None, :].astype(F32)


def kernel(x, c, ctx, c_ctx, w_mod, b_mod, norm_mix, norm_ffn, w_in, b_gate, conv_w, conv_b,
           w_conv_out, q_a_norm, w_q_b, kv_a_norm, w_kv_b, q_norm, k_norm, w_attn_o, w_out,
           w_ffn_in, w_ffn_out):
    depth = w_mod.shape[0]
    assert depth == 1, "context stream update is only needed between layers"
    b, s, d = x.shape
    n_ctx = ctx.shape[1]
    assert s % TOKEN_TILE == 0 and s % ATTN_Q_TILE == 0 and s % ATTN_KV_TILE == 0
    assert b + 1 <= SUBLANES

    cond = jnp.zeros((SUBLANES, d), F32).at[:b].set(c).at[b].set(c_ctx)
    mod = _adaln(cond, w_mod, b_mod)
    mod_x = mod[:b, None, :]
    mod_ctx = mod[b:b + 1]

    wi = w_in[0]
    o_q = 3 * d
    o_kv = o_q + Q_LORA_RANK
    o_kr = o_kv + KV_LORA_RANK
    o_gc = o_kr + QK_ROPE_DIM
    w_main = jnp.concatenate([wi[:, :3 * d], wi[:, o_gc:]], axis=1).astype(BF16)
    w_qa = wi[:, o_q:o_kv].astype(BF16)
    w_kr = wi[:, o_kr:o_gc]
    w_kv_in = jnp.concatenate([wi[:, o_kv:o_kr], w_kr, _swap_halves(w_kr)], axis=1).astype(BF16)
    wq = w_q_b[0].reshape(Q_LORA_RANK, N_HEADS, QK_HEAD_DIM)
    wq = jnp.concatenate([wq, _swap_halves(wq[..., QK_NOPE_DIM:])], axis=-1)
    wqb = wq.reshape(Q_LORA_RANK, N_HEADS * 2 * LANES).astype(BF16)
    wkv = w_kv_b[0].reshape(KV_LORA_RANK, N_HEADS, QK_NOPE_DIM + V_HEAD_DIM)
    wkvb = jnp.concatenate([wkv[..., :QK_NOPE_DIM].reshape(KV_LORA_RANK, -1),
                            wkv[..., QK_NOPE_DIM:].reshape(KV_LORA_RANK, -1)],
                           axis=1).astype(BF16)
    gq = _norm_gain_pair(q_norm[0])
    gk = _norm_gain_pair(k_norm[0])
    table = _rope_table(s)
    table_ctx = jnp.concatenate([jnp.ones((n_ctx, QK_ROPE_DIM), F32),
                                 jnp.zeros((n_ctx, QK_ROPE_DIM), F32)], axis=1)
    nmix = norm_mix[0][None, :]
    kvan = kv_a_norm[0][None, :]

    k_ctx, v_ctx = _ctx_kv(ctx, mod_ctx, nmix, w_kv_in, kvan, wkvb, gk, table_ctx)
    bx, u, gc, ga, q, k_lat, v_lat = _in_proj(
        x, mod_x, nmix, w_main, w_qa, w_kv_in, b_gate[0][None, :], q_a_norm[0][None, :], wqb,
        kvan, wkvb, gq, gk, table)
    score_bound = (QK_HEAD_DIM * SM_SCALE_LOG2 * BF16_ROUNDING_MARGIN
                   * jnp.max(jnp.abs(q_norm[0])) * jnp.max(jnp.abs(k_norm[0]))).reshape(1)
    attn = _attention(score_bound.astype(F32), q, k_ctx, v_ctx, k_lat, v_lat)
    x_mid = _merge(x, mod_x, bx, u, gc, ga, attn, conv_w[0], conv_b[0][None, :],
                   w_conv_out[0].astype(BF16), w_attn_o[0].astype(BF16), w_out[0].astype(BF16))
    return _ffn(x_mid, mod_x, norm_ffn[0][None, :], w_ffn_in[0].astype(BF16),
                w_ffn_out[0].astype(BF16))
```

```python
import functools

import jax
import jax.numpy as jnp
import numpy as np
from jax import lax
from jax.experimental import pallas as pl
from jax.experimental.pallas import tpu as pltpu

F32 = jnp.float32
BF16 = jnp.bfloat16

N_HEADS = 8
QK_NOPE_DIM = 128
QK_ROPE_DIM = 64
QK_HEAD_DIM = QK_NOPE_DIM + QK_ROPE_DIM
V_HEAD_DIM = 128
Q_LORA_RANK = 384
KV_LORA_RANK = 256
GRID_W = 64
ROPE_THETA = 10000.0
NORM_EPS = 1e-6
MOD_CHUNKS = 6
SM_SCALE_LOG2 = float(QK_HEAD_DIM ** -0.5 * np.log2(np.e))
ATTN_EXP2_SAFE_RANGE = 40.0
BF16_ROUNDING_MARGIN = 1.01

LANES = 128
SUBLANES = 8
CONTRACT_LAST = (((1,), (1,)), ((), ()))
VMEM_LIMIT_BYTES = 56 * 1024 * 1024

TOKEN_TILE = 512
ATTN_Q_TILE = 1024
ATTN_Q_SUB = 256
ATTN_KV_TILE = 512
ATTN_KV_UNROLL_ONLINE = 2
ATTN_LOOKAHEAD = 2
ADALN_N_TILE = 1536
FFN_CHUNK = 256


def _rsqrt_mean(ss, n):
    return lax.rsqrt(ss * (1.0 / n) + NORM_EPS)


def _sigmoid(t):
    return 1.0 / (1.0 + jnp.exp(-t))


def _const_spec(shape):
    nd = len(shape)
    return pl.BlockSpec(shape, lambda *_: (0,) * nd, pipeline_mode=pl.Buffered(1))


def _adaln_kernel(cond_ref, w_ref, b_ref, o_ref):
    c = cond_ref[...]
    a = c * _sigmoid(c)
    o_ref[...] = jnp.dot(a, w_ref[...], preferred_element_type=F32,
                         precision=lax.Precision.HIGHEST) + b_ref[...]


def _adaln(cond, w_mod, b_mod):
    rows, d = cond.shape
    n = w_mod.shape[-1]
    return pl.pallas_call(
        _adaln_kernel,
        grid=(n // ADALN_N_TILE,),
        in_specs=[pl.BlockSpec((rows, d), lambda j: (0, 0)),
                  pl.BlockSpec((None, d, ADALN_N_TILE), lambda j: (0, 0, j)),
                  pl.BlockSpec((1, ADALN_N_TILE), lambda j: (0, j))],
        out_specs=pl.BlockSpec((rows, ADALN_N_TILE), lambda j: (0, j)),
        out_shape=jax.ShapeDtypeStruct((rows, n), F32),
        compiler_params=pltpu.CompilerParams(dimension_semantics=("arbitrary",),
                                             vmem_limit_bytes=VMEM_LIMIT_BYTES),
        name="adaln",
    )(cond, w_mod, b_mod)


def _modulated_norm(x, gain, shift, scale):
    d = x.shape[-1]
    r = _rsqrt_mean(jnp.sum(x * x, axis=-1, keepdims=True), d)
    return (x * r * gain) * (1.0 + scale) + shift


def _rope_pair(t2, table):
    w = t2 * table
    return w + pltpu.roll(w, QK_ROPE_DIM, axis=1)


def _kv_matmuls(h, w_kv_in_ref, kvan_ref, w_k_ref, w_vt_ref):
    kvp = jnp.dot(h, w_kv_in_ref[...], preferred_element_type=F32)
    kv_a = kvp[:, :KV_LORA_RANK]
    r = _rsqrt_mean(jnp.sum(kv_a * kv_a, axis=-1, keepdims=True), KV_LORA_RANK)
    kv_n = (kv_a * r * kvan_ref[...]).astype(BF16)
    k_nope = jnp.dot(kv_n, w_k_ref[...], preferred_element_type=F32)
    v_t = lax.dot_general(w_vt_ref[...], kv_n, CONTRACT_LAST, preferred_element_type=F32)
    return k_nope, v_t, kvp[:, KV_LORA_RANK:]


def _kv_finish(kv, v_t, kr2, gk_ref, table, k_ref, vt_ref):
    gk = gk_ref[...]
    ss_rope = 0.5 * jnp.sum(kr2 * kr2, axis=-1, keepdims=True)
    kr = _rope_pair(kr2 * gk[:, QK_NOPE_DIM:], table)[:, :QK_ROPE_DIM]
    for hd in range(N_HEADS):
        k_nope = kv[:, hd * QK_NOPE_DIM:(hd + 1) * QK_NOPE_DIM]
        ss = jnp.sum(k_nope * k_nope, axis=-1, keepdims=True) + ss_rope
        rh = _rsqrt_mean(ss, QK_HEAD_DIM)
        k_ref[hd, :, :QK_NOPE_DIM] = (k_nope * rh * gk[:, :QK_NOPE_DIM]).astype(BF16)
        k_ref[hd, :, QK_NOPE_DIM:] = (kr * rh).astype(BF16)
        vt_ref[hd] = v_t[hd * V_HEAD_DIM:(hd + 1) * V_HEAD_DIM, :].astype(BF16)


def _ctx_kv_kernel(ctx_ref, mod_ref, nmix_ref, w_kv_in_ref, kvan_ref, w_k_ref, w_vt_ref, gk_ref,
                   table_ref, k_ref, vt_ref):
    d = ctx_ref.shape[-1]
    mod = mod_ref[...]
    h = _modulated_norm(ctx_ref[...], nmix_ref[...], mod[:, :d], mod[:, d:2 * d]).astype(BF16)
    k_nope, v_t, kr2 = _kv_matmuls(h, w_kv_in_ref, kvan_ref, w_k_ref, w_vt_ref)
    _kv_finish(k_nope, v_t, kr2, gk_ref, table_ref[...], k_ref, vt_ref)


def _ctx_kv(ctx, mod_ctx, nmix, w_kv_in, kvan, w_k, w_vt, gk, table):
    b, n, d = ctx.shape
    return pl.pallas_call(
        _ctx_kv_kernel,
        grid=(b,),
        in_specs=[pl.BlockSpec((None, n, d), lambda i: (i, 0, 0)),
                  _const_spec(mod_ctx.shape), _const_spec(nmix.shape),
                  _const_spec(w_kv_in.shape), _const_spec(kvan.shape),
                  _const_spec(w_k.shape), _const_spec(w_vt.shape), _const_spec(gk.shape),
                  _const_spec(table.shape)],
        out_specs=[pl.BlockSpec((None, N_HEADS, n, QK_HEAD_DIM), lambda i: (i, 0, 0, 0)),
                   pl.BlockSpec((None, N_HEADS, V_HEAD_DIM, n), lambda i: (i, 0, 0, 0))],
        out_shape=[jax.ShapeDtypeStruct((b, N_HEADS, n, QK_HEAD_DIM), BF16),
                   jax.ShapeDtypeStruct((b, N_HEADS, V_HEAD_DIM, n), BF16)],
        compiler_params=pltpu.CompilerParams(dimension_semantics=("arbitrary",),
                                             vmem_limit_bytes=VMEM_LIMIT_BYTES),
        name="ctx_kv",
    )(ctx, mod_ctx, nmix, w_kv_in, kvan, w_k, w_vt, gk, table)


def _in_proj_kernel(x_ref, mod_ref, nmix_ref, w_main_ref, w_qa_ref, w_kv_in_ref, bgate_ref,
                    qan_ref, wqb_ref, kvan_ref, w_k_ref, w_vt_ref, gq_ref, gk_ref, table_ref,
                    bx_ref, u_ref, gc_ref, ga_ref, q_ref, k_ref, vt_ref):
    d = x_ref.shape[-1]
    mod = mod_ref[...]
    h = _modulated_norm(x_ref[...], nmix_ref[...], mod[:, :d], mod[:, d:2 * d]).astype(BF16)
    table = table_ref[...]

    def proj(j):
        return jnp.dot(h, w_main_ref[:, j * d:(j + 1) * d], preferred_element_type=F32)

    q_a = jnp.dot(h, w_qa_ref[...], preferred_element_type=F32)
    r = _rsqrt_mean(jnp.sum(q_a * q_a, axis=-1, keepdims=True), Q_LORA_RANK)
    q_n = (q_a * r * qan_ref[...]).astype(BF16)
    q = jnp.dot(q_n, wqb_ref[...], preferred_element_type=F32)
    k_nope, v_t, kr2 = _kv_matmuls(h, w_kv_in_ref, kvan_ref, w_k_ref, w_vt_ref)

    bx_ref[...] = proj(0).astype(BF16)
    u_ref[...] = (proj(1) * proj(2)).astype(BF16)
    bgate = bgate_ref[...]
    gc_ref[...] = _sigmoid(proj(3) + bgate[:, :d]).astype(BF16)
    ga_ref[...] = _sigmoid(proj(4) + bgate[:, d:]).astype(BF16)

    gq = gq_ref[...]
    table_q = table * gq[:, QK_NOPE_DIM:]
    sm_scale = SM_SCALE_LOG2
    hw = 2 * LANES
    for hd in range(N_HEADS):
        q_nope = q[:, hd * hw:hd * hw + QK_NOPE_DIM]
        q_r2 = q[:, hd * hw + QK_NOPE_DIM:(hd + 1) * hw]
        ss = jnp.sum(q_nope * q_nope + 0.5 * (q_r2 * q_r2), axis=-1, keepdims=True)
        rh = _rsqrt_mean(ss, QK_HEAD_DIM) * sm_scale
        q_ref[hd, :, :QK_NOPE_DIM] = (q_nope * rh * gq[:, :QK_NOPE_DIM]).astype(BF16)
        q_ref[hd, :, QK_NOPE_DIM:] = (_rope_pair(q_r2 * rh, table_q)[:, :QK_ROPE_DIM]).astype(BF16)

    _kv_finish(k_nope, v_t, kr2, gk_ref, table, k_ref, vt_ref)


def _in_proj(x, mod_x, nmix, w_main, w_qa, w_kv_in, bgate, qan, wqb, kvan, w_k, w_vt, gq, gk,
             table):
    b, s, d = x.shape
    tm = TOKEN_TILE
    tok = pl.BlockSpec((None, tm, d), lambda bi, i: (bi, i, 0))

    def head_spec(width):
        return pl.BlockSpec((None, N_HEADS, tm, width), lambda bi, i: (bi, 0, i, 0))

    tok_shape = jax.ShapeDtypeStruct((b, s, d), BF16)
    return pl.pallas_call(
        _in_proj_kernel,
        grid=(b, s // tm),
        in_specs=[tok,
                  pl.BlockSpec((None, 1, mod_x.shape[-1]), lambda bi, i: (bi, 0, 0)),
                  _const_spec(nmix.shape), _const_spec(w_main.shape), _const_spec(w_qa.shape),
                  _const_spec(w_kv_in.shape), _const_spec(bgate.shape), _const_spec(qan.shape),
                  _const_spec(wqb.shape), _const_spec(kvan.shape), _const_spec(w_k.shape),
                  _const_spec(w_vt.shape), _const_spec(gq.shape), _const_spec(gk.shape),
                  pl.BlockSpec((tm, 2 * QK_ROPE_DIM), lambda bi, i: (i, 0))],
        out_specs=[tok, tok, tok, tok,
                   head_spec(QK_HEAD_DIM), head_spec(QK_HEAD_DIM),
                   pl.BlockSpec((None, N_HEADS, V_HEAD_DIM, tm), lambda bi, i: (bi, 0, 0, i))],
        out_shape=[tok_shape, tok_shape, tok_shape, tok_shape,
                   jax.ShapeDtypeStruct((b, N_HEADS, s, QK_HEAD_DIM), BF16),
                   jax.ShapeDtypeStruct((b, N_HEADS, s, QK_HEAD_DIM), BF16),
                   jax.ShapeDtypeStruct((b, N_HEADS, V_HEAD_DIM, s), BF16)],
        compiler_params=pltpu.CompilerParams(dimension_semantics=("arbitrary", "arbitrary"),
                                             vmem_limit_bytes=VMEM_LIMIT_BYTES),
        name="in_proj",
    )(x, mod_x, nmix, w_main, w_qa, w_kv_in, bgate, qan, wqb, kvan, w_k, w_vt, gq, gk, table)


def _attention_kernel(bound_ref, q_ref, kc_ref, vct_ref, kl_ref, vlt_ref, o_ref,
                      lt_ref, acct_ref, m_ref, l_ref, acc_ref):
    tq = q_ref.shape[0]

    def scores_t(unit):
        k_ref, _, k0, tk = unit
        return lax.dot_general(k_ref[k0:k0 + tk, :], q_ref[...], CONTRACT_LAST,
                               preferred_element_type=F32)

    def finish_bounded(unit, st):
        _, vt_ref, k0, tk = unit
        pt = jnp.exp2(st)
        lt_ref[...] += pt.reshape(tk // SUBLANES, SUBLANES, tq).sum(axis=0)
        acct_ref[...] += jnp.dot(vt_ref[:, k0:k0 + tk], pt.astype(BF16),
                                 preferred_element_type=F32)

    def bounded_path():
        lt_ref[...] = jnp.zeros(lt_ref.shape, F32)
        acct_ref[...] = jnp.zeros(acct_ref.shape, F32)
        tk = ATTN_KV_TILE
        units = [(kc_ref, vct_ref, 0, kc_ref.shape[0])]
        units += [(kl_ref, vlt_ref, k0, tk) for k0 in range(0, kl_ref.shape[0], tk)]
        pending = [scores_t(u) for u in units[:ATTN_LOOKAHEAD]]
        for i, unit in enumerate(units):
            if i + ATTN_LOOKAHEAD < len(units):
                pending.append(scores_t(units[i + ATTN_LOOKAHEAD]))
            finish_bounded(unit, pending[i])
        l = jnp.sum(lt_ref[...], axis=0, keepdims=True)
        o_ref[...] = (acct_ref[...] / l).T.astype(o_ref.dtype)

    def scores(unit):
        k_ref, _, k0, tk, r0 = unit
        return lax.dot_general(q_ref[r0:r0 + ATTN_Q_SUB, :], k_ref[pl.ds(k0, tk), :],
                               CONTRACT_LAST, preferred_element_type=F32)

    def finish_online(unit, s):
        _, vt_ref, k0, tk, r0 = unit
        rows = slice(r0, r0 + ATTN_Q_SUB)
        cols = [s[:, c:c + LANES] for c in range(0, tk, LANES)]
        m_old = m_ref[rows, :]
        row_max = jnp.max(functools.reduce(jnp.maximum, cols), axis=-1, keepdims=True)
        m_new = jnp.maximum(m_old, row_max)
        alpha = jnp.exp2(m_old - m_new)
        ps = [jnp.exp2(col - m_new) for col in cols]
        l_ref[rows, :] = alpha * l_ref[rows, :] + functools.reduce(jnp.add, ps)
        p = jnp.concatenate([x.astype(BF16) for x in ps], axis=1)
        pv = lax.dot_general(p, vt_ref[:, pl.ds(k0, tk)], CONTRACT_LAST,
                             preferred_element_type=F32)
        acc_ref[rows, :] = alpha * acc_ref[rows, :] + pv
        m_ref[rows, :] = m_new

    def run_online(units):
        pending = [scores(u) for u in units[:ATTN_LOOKAHEAD]]
        for i, unit in enumerate(units):
            if i + ATTN_LOOKAHEAD < len(units):
                pending.append(scores(units[i + ATTN_LOOKAHEAD]))
            finish_online(unit, pending[i])

    def units_of(k_ref, vt_ref, starts, tk):
        return [(k_ref, vt_ref, k0, tk, r0) for k0 in starts for r0 in range(0, tq, ATTN_Q_SUB)]

    def online_path():
        m_ref[...] = jnp.full(m_ref.shape, -jnp.inf, F32)
        l_ref[...] = jnp.zeros(l_ref.shape, F32)
        acc_ref[...] = jnp.zeros(acc_ref.shape, F32)
        tk = ATTN_KV_TILE
        span = tk * ATTN_KV_UNROLL_ONLINE
        run_online(units_of(kc_ref, vct_ref, [0], kc_ref.shape[0]))

        def body(j, carry):
            base = pl.multiple_of(j * span, span)
            run_online(units_of(kl_ref, vlt_ref,
                                [base + u * tk for u in range(ATTN_KV_UNROLL_ONLINE)], tk))
            return carry

        lax.fori_loop(0, kl_ref.shape[0] // span, body, 0)
        l = jnp.sum(l_ref[...], axis=-1, keepdims=True)
        o_ref[...] = (acc_ref[...] / l).astype(o_ref.dtype)

    bounded = bound_ref[0] <= ATTN_EXP2_SAFE_RANGE
    pl.when(bounded)(bounded_path)
    pl.when(jnp.logical_not(bounded))(online_path)


def _attention(score_bound, q, k_ctx, vt_ctx, k_lat, vt_lat):
    b, nh, s, dk = q.shape
    nc = k_ctx.shape[2]
    dv = vt_lat.shape[2]
    assert dv == LANES
    tq = ATTN_Q_TILE

    def per_head(rows, cols):
        return pl.BlockSpec((None, None, rows, cols), lambda bi, hi, i: (bi, hi, 0, 0))

    return pl.pallas_call(
        _attention_kernel,
        grid=(b, nh, s // tq),
        in_specs=[pl.BlockSpec(memory_space=pltpu.SMEM),
                  pl.BlockSpec((None, None, tq, dk), lambda bi, hi, i: (bi, hi, i, 0)),
                  per_head(nc, dk), per_head(dv, nc), per_head(s, dk), per_head(dv, s)],
        out_specs=pl.BlockSpec((None, tq, dv), lambda bi, hi, i: (bi, i, hi)),
        out_shape=jax.ShapeDtypeStruct((b, s, nh * dv), BF16),
        scratch_shapes=[pltpu.VMEM((SUBLANES, tq), F32), pltpu.VMEM((dv, tq), F32),
                        pltpu.VMEM((tq, LANES), F32), pltpu.VMEM((tq, LANES), F32),
                        pltpu.VMEM((tq, dv), F32)],
        compiler_params=pltpu.CompilerParams(
            dimension_semantics=("arbitrary", "arbitrary", "arbitrary"),
            vmem_limit_bytes=VMEM_LIMIT_BYTES),
        name="attention",
    )(score_bound, q, k_ctx, vt_ctx, k_lat, vt_lat)


def _merge_kernel(x_ref, mod_ref, bx_ref, u_ref, u_prev_ref, u_next_ref, gc_ref, ga_ref,
                  attn_ref, convw_ref, convb_ref, w_conv_out_ref, w_attn_o_ref, w_out_ref,
                  o_ref):
    i = pl.program_id(1)
    tm, d = x_ref.shape
    y_attn = jnp.dot(attn_ref[...], w_attn_o_ref[...], preferred_element_type=F32)
    u = u_ref[...].astype(F32)
    prev_row = u_prev_ref[SUBLANES - 1:SUBLANES, :].astype(F32) * (i > 0).astype(F32)
    next_row = u_next_ref[0:1, :].astype(F32) * (i < pl.num_programs(1) - 1).astype(F32)
    row = lax.broadcasted_iota(jnp.int32, (tm, d), 0)
    u_m1 = jnp.where(row == 0, prev_row, pltpu.roll(u, 1, axis=0))
    u_p1 = jnp.where(row == tm - 1, next_row, pltpu.roll(u, tm - 1, axis=0))
    cw = convw_ref[...]
    conv = u_m1 * cw[0:1, :] + u * cw[1:2, :] + u_p1 * cw[2:3, :] + convb_ref[...]
    z = (bx_ref[...].astype(F32) * conv).astype(BF16)
    y_conv = jnp.dot(z, w_conv_out_ref[...], preferred_element_type=F32)
    merged = (gc_ref[...].astype(F32) * y_conv + ga_ref[...].astype(F32) * y_attn).astype(BF16)
    y = jnp.dot(merged, w_out_ref[...], preferred_element_type=F32)
    g1 = mod_ref[:, 2 * d:3 * d]
    o_ref[...] = x_ref[...] + g1 * y


def _merge(x, mod_x, bx, u, gc, ga, attn, conv_w, conv_b, w_conv_out, w_attn_o, w_out):
    b, s, d = x.shape
    tm = TOKEN_TILE
    hb = tm // SUBLANES
    last_hb = s // SUBLANES - 1
    tok = pl.BlockSpec((None, tm, d), lambda bi, i: (bi, i, 0))
    prev = pl.BlockSpec((None, SUBLANES, d), lambda bi, i: (bi, jnp.maximum(i * hb - 1, 0), 0))
    nxt = pl.BlockSpec((None, SUBLANES, d),
                       lambda bi, i: (bi, jnp.minimum((i + 1) * hb, last_hb), 0))
    return pl.pallas_call(
        _merge_kernel,
        grid=(b, s // tm),
        in_specs=[tok,
                  pl.BlockSpec((None, 1, mod_x.shape[-1]), lambda bi, i: (bi, 0, 0)),
                  tok, tok, prev, nxt, tok, tok, tok,
                  _const_spec(conv_w.shape), _const_spec(conv_b.shape),
                  _const_spec(w_conv_out.shape), _const_spec(w_attn_o.shape),
                  _const_spec(w_out.shape)],
        out_specs=tok,
        out_shape=jax.ShapeDtypeStruct((b, s, d), F32),
        compiler_params=pltpu.CompilerParams(dimension_semantics=("arbitrary", "arbitrary"),
                                             vmem_limit_bytes=VMEM_LIMIT_BYTES),
        name="merge",
    )(x, mod_x, bx, u, u, u, gc, ga, attn, conv_w, conv_b, w_conv_out, w_attn_o, w_out)


def _ffn_kernel(x_ref, mod_ref, nffn_ref, w_in_ref, w_out_ref, o_ref, act_ref):
    d = x_ref.shape[-1]
    d_ff = w_out_ref.shape[0]
    x = x_ref[...]
    mod = mod_ref[...]
    h = _modulated_norm(x, nffn_ref[...], mod[:, 3 * d:4 * d], mod[:, 4 * d:5 * d]).astype(BF16)
    for j in range(d_ff // FFN_CHUNK):
        c0 = j * FFN_CHUNK
        gate = jnp.dot(h, w_in_ref[:, c0:c0 + FFN_CHUNK], preferred_element_type=F32)
        up = jnp.dot(h, w_in_ref[:, d_ff + c0:d_ff + c0 + FFN_CHUNK],
                     preferred_element_type=F32)
        act_ref[:, c0:c0 + FFN_CHUNK] = (gate * _sigmoid(gate) * up).astype(BF16)
    y = jnp.dot(act_ref[...], w_out_ref[...], preferred_element_type=F32)
    o_ref[...] = x + mod[:, 5 * d:6 * d] * y


def _ffn(x, mod_x, nffn, w_ffn_in, w_ffn_out):
    b, s, d = x.shape
    tm = TOKEN_TILE
    d_ff = w_ffn_out.shape[0]
    tok = pl.BlockSpec((None, tm, d), lambda bi, i: (bi, i, 0))
    return pl.pallas_call(
        _ffn_kernel,
        grid=(b, s // tm),
        in_specs=[tok,
                  pl.BlockSpec((None, 1, mod_x.shape[-1]), lambda bi, i: (bi, 0, 0)),
                  _const_spec(nffn.shape), _const_spec(w_ffn_in.shape),
                  _const_spec(w_ffn_out.shape)],
        out_specs=tok,
        out_shape=jax.ShapeDtypeStruct((b, s, d), F32),
        scratch_shapes=[pltpu.VMEM((tm, d_ff), BF16)],
        compiler_params=pltpu.CompilerParams(dimension_semantics=("arbitrary", "arbitrary"),
                                             vmem_limit_bytes=VMEM_LIMIT_BYTES),
        name="ffn",
    )(x, mod_x, nffn, w_ffn_in, w_ffn_out)


def _rope_table(seq_len):
    quarter = QK_ROPE_DIM // 4
    freqs = ROPE_THETA ** (-np.arange(quarter, dtype=np.float64) / quarter)
    t = np.arange(seq_len)
    ang = [(t // GRID_W)[:, None] * freqs, (t % GRID_W)[:, None] * freqs]
    cos = [np.cos(a) for a in ang]
    sin = [np.sin(a) for a in ang]
    c = np.concatenate([cos[0], cos[0], cos[1], cos[1]], axis=-1)
    s = np.concatenate([-sin[0], sin[0], -sin[1], sin[1]], axis=-1)
    return jnp.asarray(np.concatenate([c, s], axis=-1), F32)


def _swap_halves(t):
    q = QK_ROPE_DIM // 4
    return jnp.concatenate([t[..., q:2 * q], t[..., :q], t[..., 3 * q:], t[..., 2 * q:3 * q]],
                           axis=-1)


def _norm_gain_pair(g):
    rope = g[QK_NOPE_DIM:]
    return jnp.concatenate([g[:QK_NOPE_DIM], rope, _swap_halves(rope)])[None, :].astype(F32)


def kernel(x, c, ctx, c_ctx, w_mod, b_mod, norm_mix, norm_ffn, w_in, b_gate, conv_w, conv_b,
           w_conv_out, q_a_norm, w_q_b, kv_a_norm, w_kv_b, q_norm, k_norm, w_attn_o, w_out,
           w_ffn_in, w_ffn_out):
    depth = w_mod.shape[0]
    assert depth == 1, "context stream update is only needed between layers"
    b, s, d = x.shape
    n_ctx = ctx.shape[1]
    assert s % TOKEN_TILE == 0 and s % ATTN_Q_TILE == 0 and s % ATTN_KV_TILE == 0
    assert b + 1 <= SUBLANES

    cond = jnp.zeros((SUBLANES, d), F32).at[:b].set(c).at[b].set(c_ctx)
    mod = _adaln(cond, w_mod, b_mod)
    mod_x = mod[:b, None, :]
    mod_ctx = mod[b:b + 1]

    wi = w_in[0]
    o_q = 3 * d
    o_kv = o_q + Q_LORA_RANK
    o_kr = o_kv + KV_LORA_RANK
    o_gc = o_kr + QK_ROPE_DIM
    w_main = jnp.concatenate([wi[:, :3 * d], wi[:, o_gc:]], axis=1).astype(BF16)
    w_qa = wi[:, o_q:o_kv].astype(BF16)
    w_kr = wi[:, o_kr:o_gc]
    w_kv_in = jnp.concatenate([wi[:, o_kv:o_kr], w_kr, _swap_halves(w_kr)], axis=1).astype(BF16)
    wq = w_q_b[0].reshape(Q_LORA_RANK, N_HEADS, QK_HEAD_DIM)
    wq = jnp.concatenate([wq, _swap_halves(wq[..., QK_NOPE_DIM:])], axis=-1)
    wqb = wq.reshape(Q_LORA_RANK, N_HEADS * 2 * LANES).astype(BF16)
    wkv = w_kv_b[0].reshape(KV_LORA_RANK, N_HEADS, QK_NOPE_DIM + V_HEAD_DIM)
    w_k = wkv[..., :QK_NOPE_DIM].reshape(KV_LORA_RANK, -1).astype(BF16)
    w_vt = wkv[..., QK_NOPE_DIM:].reshape(KV_LORA_RANK, -1).T.astype(BF16)
    gq = _norm_gain_pair(q_norm[0])
    gk = _norm_gain_pair(k_norm[0])
    table = _rope_table(s)
    table_ctx = jnp.concatenate([jnp.ones((n_ctx, QK_ROPE_DIM), F32),
                                 jnp.zeros((n_ctx, QK_ROPE_DIM), F32)], axis=1)
    nmix = norm_mix[0][None, :]
    kvan = kv_a_norm[0][None, :]

    k_ctx, vt_ctx = _ctx_kv(ctx, mod_ctx, nmix, w_kv_in, kvan, w_k, w_vt, gk, table_ctx)
    bx, u, gc, ga, q, k_lat, vt_lat = _in_proj(
        x, mod_x, nmix, w_main, w_qa, w_kv_in, b_gate[0][None, :], q_a_norm[0][None, :], wqb,
        kvan, w_k, w_vt, gq, gk, table)
    score_bound = (QK_HEAD_DIM * SM_SCALE_LOG2 * BF16_ROUNDING_MARGIN
                   * jnp.max(jnp.abs(q_norm[0])) * jnp.max(jnp.abs(k_norm[0]))).reshape(1)
    attn = _attention(score_bound.astype(F32), q, k_ctx, vt_ctx, k_lat, vt_lat)
    x_mid = _merge(x, mod_x, bx, u, gc, ga, attn, conv_w[0], conv_b[0][None, :],
                   w_conv_out[0].astype(BF16), w_attn_o[0].astype(BF16), w_out[0].astype(BF16))
    return _ffn(x_mid, mod_x, norm_ffn[0][None, :], w_ffn_in[0].astype(BF16),
                w_ffn_out[0].astype(BF16))
```

```python
import functools

import jax
import jax.numpy as jnp
import numpy as np
from jax import lax
from jax.experimental import pallas as pl
from jax.experimental.pallas import tpu as pltpu

F32 = jnp.float32
BF16 = jnp.bfloat16

N_HEADS = 8
QK_NOPE_DIM = 128
QK_ROPE_DIM = 64
QK_HEAD_DIM = QK_NOPE_DIM + QK_ROPE_DIM
V_HEAD_DIM = 128
Q_LORA_RANK = 384
KV_LORA_RANK = 256
GRID_W = 64
ROPE_THETA = 10000.0
NORM_EPS = 1e-6
MOD_CHUNKS = 6
SM_SCALE_LOG2 = float(QK_HEAD_DIM ** -0.5 * np.log2(np.e))
ATTN_EXP2_SAFE_RANGE = 40.0
BF16_ROUNDING_MARGIN = 1.01

LANES = 128
SUBLANES = 8
CONTRACT_LAST = (((1,), (1,)), ((), ()))
VMEM_LIMIT_BYTES = 56 * 1024 * 1024

TOKEN_TILE = 512
ATTN_Q_TILE = 1024
ATTN_Q_SUB = 256
ATTN_KV_TILE = 1024
ATTN_KV_UNROLL_ONLINE = 2
ATTN_LOOKAHEAD = 2
ADALN_N_TILE = 1536
FFN_CHUNK = 256


def _rsqrt_mean(ss, n):
    return lax.rsqrt(ss * (1.0 / n) + NORM_EPS)


def _sigmoid(t):
    return 1.0 / (1.0 + jnp.exp(-t))


def _const_spec(shape):
    nd = len(shape)
    return pl.BlockSpec(shape, lambda *_: (0,) * nd, pipeline_mode=pl.Buffered(1))


def _adaln_kernel(cond_ref, w_ref, b_ref, o_ref):
    c = cond_ref[...]
    a = c * _sigmoid(c)
    o_ref[...] = jnp.dot(a, w_ref[...], preferred_element_type=F32,
                         precision=lax.Precision.HIGHEST) + b_ref[...]


def _adaln(cond, w_mod, b_mod):
    rows, d = cond.shape
    n = w_mod.shape[-1]
    return pl.pallas_call(
        _adaln_kernel,
        grid=(n // ADALN_N_TILE,),
        in_specs=[pl.BlockSpec((rows, d), lambda j: (0, 0)),
                  pl.BlockSpec((None, d, ADALN_N_TILE), lambda j: (0, 0, j)),
                  pl.BlockSpec((1, ADALN_N_TILE), lambda j: (0, j))],
        out_specs=pl.BlockSpec((rows, ADALN_N_TILE), lambda j: (0, j)),
        out_shape=jax.ShapeDtypeStruct((rows, n), F32),
        compiler_params=pltpu.CompilerParams(dimension_semantics=("arbitrary",),
                                             vmem_limit_bytes=VMEM_LIMIT_BYTES),
        name="adaln",
    )(cond, w_mod, b_mod)


def _modulated_norm(x, gain, shift, scale):
    d = x.shape[-1]
    r = _rsqrt_mean(jnp.sum(x * x, axis=-1, keepdims=True), d)
    return (x * r * gain) * (1.0 + scale) + shift


def _rope_pair(t2, table):
    w = t2 * table
    return w + pltpu.roll(w, QK_ROPE_DIM, axis=1)


def _kv_matmuls(h, w_kv_in_ref, kvan_ref, w_k_ref, w_vt_ref):
    kvp = jnp.dot(h, w_kv_in_ref[...], preferred_element_type=F32)
    kv_a = kvp[:, :KV_LORA_RANK]
    r = _rsqrt_mean(jnp.sum(kv_a * kv_a, axis=-1, keepdims=True), KV_LORA_RANK)
    kv_n = (kv_a * r * kvan_ref[...]).astype(BF16)
    k_nope = jnp.dot(kv_n, w_k_ref[...], preferred_element_type=F32)
    v_t = lax.dot_general(w_vt_ref[...], kv_n, CONTRACT_LAST, preferred_element_type=F32)
    return k_nope, v_t, kvp[:, KV_LORA_RANK:]


def _kv_finish(kv, v_t, kr2, gk_ref, table, k_ref, vt_ref):
    gk = gk_ref[...]
    ss_rope = 0.5 * jnp.sum(kr2 * kr2, axis=-1, keepdims=True)
    kr = _rope_pair(kr2 * gk[:, QK_NOPE_DIM:], table)[:, :QK_ROPE_DIM]
    for hd in range(N_HEADS):
        k_nope = kv[:, hd * QK_NOPE_DIM:(hd + 1) * QK_NOPE_DIM]
        ss = jnp.sum(k_nope * k_nope, axis=-1, keepdims=True) + ss_rope
        rh = _rsqrt_mean(ss, QK_HEAD_DIM)
        k_ref[hd, :, :QK_NOPE_DIM] = (k_nope * rh * gk[:, :QK_NOPE_DIM]).astype(BF16)
        k_ref[hd, :, QK_NOPE_DIM:] = (kr * rh).astype(BF16)
        vt_ref[hd] = v_t[hd * V_HEAD_DIM:(hd + 1) * V_HEAD_DIM, :].astype(BF16)


def _ctx_kv_kernel(ctx_ref, mod_ref, nmix_ref, w_kv_in_ref, kvan_ref, w_k_ref, w_vt_ref, gk_ref,
                   table_ref, k_ref, vt_ref):
    d = ctx_ref.shape[-1]
    mod = mod_ref[...]
    h = _modulated_norm(ctx_ref[...], nmix_ref[...], mod[:, :d], mod[:, d:2 * d]).astype(BF16)
    k_nope, v_t, kr2 = _kv_matmuls(h, w_kv_in_ref, kvan_ref, w_k_ref, w_vt_ref)
    _kv_finish(k_nope, v_t, kr2, gk_ref, table_ref[...], k_ref, vt_ref)


def _ctx_kv(ctx, mod_ctx, nmix, w_kv_in, kvan, w_k, w_vt, gk, table):
    b, n, d = ctx.shape
    return pl.pallas_call(
        _ctx_kv_kernel,
        grid=(b,),
        in_specs=[pl.BlockSpec((None, n, d), lambda i: (i, 0, 0)),
                  _const_spec(mod_ctx.shape), _const_spec(nmix.shape),
                  _const_spec(w_kv_in.shape), _const_spec(kvan.shape),
                  _const_spec(w_k.shape), _const_spec(w_vt.shape), _const_spec(gk.shape),
                  _const_spec(table.shape)],
        out_specs=[pl.BlockSpec((None, N_HEADS, n, QK_HEAD_DIM), lambda i: (i, 0, 0, 0)),
                   pl.BlockSpec((None, N_HEADS, V_HEAD_DIM, n), lambda i: (i, 0, 0, 0))],
        out_shape=[jax.ShapeDtypeStruct((b, N_HEADS, n, QK_HEAD_DIM), BF16),
                   jax.ShapeDtypeStruct((b, N_HEADS, V_HEAD_DIM, n), BF16)],
        compiler_params=pltpu.CompilerParams(dimension_semantics=("arbitrary",),
                                             vmem_limit_bytes=VMEM_LIMIT_BYTES),
        name="ctx_kv",
    )(ctx, mod_ctx, nmix, w_kv_in, kvan, w_k, w_vt, gk, table)


def _in_proj_kernel(x_ref, mod_ref, nmix_ref, w_main_ref, w_qa_ref, w_kv_in_ref, bgate_ref,
                    qan_ref, wqb_ref, kvan_ref, w_k_ref, w_vt_ref, gq_ref, gk_ref, table_ref,
                    bx_ref, u_ref, gc_ref, ga_ref, q_ref, k_ref, vt_ref):
    d = x_ref.shape[-1]
    mod = mod_ref[...]
    h = _modulated_norm(x_ref[...], nmix_ref[...], mod[:, :d], mod[:, d:2 * d]).astype(BF16)
    table = table_ref[...]

    def proj(j):
        return jnp.dot(h, w_main_ref[:, j * d:(j + 1) * d], preferred_element_type=F32)

    q_a = jnp.dot(h, w_qa_ref[...], preferred_element_type=F32)
    r = _rsqrt_mean(jnp.sum(q_a * q_a, axis=-1, keepdims=True), Q_LORA_RANK)
    q_n = (q_a * r * qan_ref[...]).astype(BF16)
    q = jnp.dot(q_n, wqb_ref[...], preferred_element_type=F32)
    k_nope, v_t, kr2 = _kv_matmuls(h, w_kv_in_ref, kvan_ref, w_k_ref, w_vt_ref)

    bx_ref[...] = proj(0).astype(BF16)
    u_ref[...] = (proj(1) * proj(2)).astype(BF16)
    bgate = bgate_ref[...]
    gc_ref[...] = _sigmoid(proj(3) + bgate[:, :d]).astype(BF16)
    ga_ref[...] = _sigmoid(proj(4) + bgate[:, d:]).astype(BF16)

    gq = gq_ref[...]
    table_q = table * gq[:, QK_NOPE_DIM:]
    sm_scale = SM_SCALE_LOG2
    hw = 2 * LANES
    for hd in range(N_HEADS):
        q_nope = q[:, hd * hw:hd * hw + QK_NOPE_DIM]
        q_r2 = q[:, hd * hw + QK_NOPE_DIM:(hd + 1) * hw]
        ss = jnp.sum(q_nope * q_nope + 0.5 * (q_r2 * q_r2), axis=-1, keepdims=True)
        rh = _rsqrt_mean(ss, QK_HEAD_DIM) * sm_scale
        q_ref[hd, :, :QK_NOPE_DIM] = (q_nope * rh * gq[:, :QK_NOPE_DIM]).astype(BF16)
        q_ref[hd, :, QK_NOPE_DIM:] = (_rope_pair(q_r2 * rh, table_q)[:, :QK_ROPE_DIM]).astype(BF16)

    _kv_finish(k_nope, v_t, kr2, gk_ref, table, k_ref, vt_ref)


def _in_proj(x, mod_x, nmix, w_main, w_qa, w_kv_in, bgate, qan, wqb, kvan, w_k, w_vt, gq, gk,
             table):
    b, s, d = x.shape
    tm = TOKEN_TILE
    tok = pl.BlockSpec((None, tm, d), lambda bi, i: (bi, i, 0))

    def head_spec(width):
        return pl.BlockSpec((None, N_HEADS, tm, width), lambda bi, i: (bi, 0, i, 0))

    tok_shape = jax.ShapeDtypeStruct((b, s, d), BF16)
    return pl.pallas_call(
        _in_proj_kernel,
        grid=(b, s // tm),
        in_specs=[tok,
                  pl.BlockSpec((None, 1, mod_x.shape[-1]), lambda bi, i: (bi, 0, 0)),
                  _const_spec(nmix.shape), _const_spec(w_main.shape), _const_spec(w_qa.shape),
                  _const_spec(w_kv_in.shape), _const_spec(bgate.shape), _const_spec(qan.shape),
                  _const_spec(wqb.shape), _const_spec(kvan.shape), _const_spec(w_k.shape),
                  _const_spec(w_vt.shape), _const_spec(gq.shape), _const_spec(gk.shape),
                  pl.BlockSpec((tm, 2 * QK_ROPE_DIM), lambda bi, i: (i, 0))],
        out_specs=[tok, tok, tok, tok,
                   head_spec(QK_HEAD_DIM), head_spec(QK_HEAD_DIM),
                   pl.BlockSpec((None, N_HEADS, V_HEAD_DIM, tm), lambda bi, i: (bi, 0, 0, i))],
        out_shape=[tok_shape, tok_shape, tok_shape, tok_shape,
                   jax.ShapeDtypeStruct((b, N_HEADS, s, QK_HEAD_DIM), BF16),
                   jax.ShapeDtypeStruct((b, N_HEADS, s, QK_HEAD_DIM), BF16),
                   jax.ShapeDtypeStruct((b, N_HEADS, V_HEAD_DIM, s), BF16)],
        compiler_params=pltpu.CompilerParams(dimension_semantics=("arbitrary", "arbitrary"),
                                             vmem_limit_bytes=VMEM_LIMIT_BYTES),
        name="in_proj",
    )(x, mod_x, nmix, w_main, w_qa, w_kv_in, bgate, qan, wqb, kvan, w_k, w_vt, gq, gk, table)


def _attention_kernel(bound_ref, q_ref, kc_ref, vct_ref, kl_ref, vlt_ref, o_ref,
                      lt_ref, acct_ref, m_ref, l_ref, acc_ref):
    tq = q_ref.shape[0]

    def scores_t(unit):
        k_ref, _, k0, tk = unit
        return lax.dot_general(k_ref[k0:k0 + tk, :], q_ref[...], CONTRACT_LAST,
                               preferred_element_type=F32)

    def finish_bounded(unit, st):
        _, vt_ref, k0, tk = unit
        pt = jnp.exp2(st)
        lt_ref[...] += pt.reshape(tk // SUBLANES, SUBLANES, tq).sum(axis=0)
        acct_ref[...] += jnp.dot(vt_ref[:, k0:k0 + tk], pt.astype(BF16),
                                 preferred_element_type=F32)

    def bounded_path():
        lt_ref[...] = jnp.zeros(lt_ref.shape, F32)
        acct_ref[...] = jnp.zeros(acct_ref.shape, F32)
        tk = ATTN_KV_TILE
        units = [(kc_ref, vct_ref, 0, kc_ref.shape[0])]
        units += [(kl_ref, vlt_ref, k0, tk) for k0 in range(0, kl_ref.shape[0], tk)]
        pending = [scores_t(u) for u in units[:ATTN_LOOKAHEAD]]
        for i, unit in enumerate(units):
            if i + ATTN_LOOKAHEAD < len(units):
                pending.append(scores_t(units[i + ATTN_LOOKAHEAD]))
            finish_bounded(unit, pending[i])
        l = jnp.sum(lt_ref[...], axis=0, keepdims=True)
        o_ref[...] = (acct_ref[...] / l).T.astype(o_ref.dtype)

    def scores(unit):
        k_ref, _, k0, tk, r0 = unit
        return lax.dot_general(q_ref[r0:r0 + ATTN_Q_SUB, :], k_ref[pl.ds(k0, tk), :],
                               CONTRACT_LAST, preferred_element_type=F32)

    def finish_online(unit, s):
        _, vt_ref, k0, tk, r0 = unit
        rows = slice(r0, r0 + ATTN_Q_SUB)
        cols = [s[:, c:c + LANES] for c in range(0, tk, LANES)]
        m_old = m_ref[rows, :]
        row_max = jnp.max(functools.reduce(jnp.maximum, cols), axis=-1, keepdims=True)
        m_new = jnp.maximum(m_old, row_max)
        alpha = jnp.exp2(m_old - m_new)
        ps = [jnp.exp2(col - m_new) for col in cols]
        l_ref[rows, :] = alpha * l_ref[rows, :] + functools.reduce(jnp.add, ps)
        p = jnp.concatenate([x.astype(BF16) for x in ps], axis=1)
        pv = lax.dot_general(p, vt_ref[:, pl.ds(k0, tk)], CONTRACT_LAST,
                             preferred_element_type=F32)
        acc_ref[rows, :] = alpha * acc_ref[rows, :] + pv
        m_ref[rows, :] = m_new

    def run_online(units):
        pending = [scores(u) for u in units[:ATTN_LOOKAHEAD]]
        for i, unit in enumerate(units):
            if i + ATTN_LOOKAHEAD < len(units):
                pending.append(scores(units[i + ATTN_LOOKAHEAD]))
            finish_online(unit, pending[i])

    def units_of(k_ref, vt_ref, starts, tk):
        return [(k_ref, vt_ref, k0, tk, r0) for k0 in starts for r0 in range(0, tq, ATTN_Q_SUB)]

    def online_path():
        m_ref[...] = jnp.full(m_ref.shape, -jnp.inf, F32)
        l_ref[...] = jnp.zeros(l_ref.shape, F32)
        acc_ref[...] = jnp.zeros(acc_ref.shape, F32)
        tk = ATTN_KV_TILE
        span = tk * ATTN_KV_UNROLL_ONLINE
        run_online(units_of(kc_ref, vct_ref, [0], kc_ref.shape[0]))

        def body(j, carry):
            base = pl.multiple_of(j * span, span)
            run_online(units_of(kl_ref, vlt_ref,
                                [base + u * tk for u in range(ATTN_KV_UNROLL_ONLINE)], tk))
            return carry

        lax.fori_loop(0, kl_ref.shape[0] // span, body, 0)
        l = jnp.sum(l_ref[...], axis=-1, keepdims=True)
        o_ref[...] = (acc_ref[...] / l).astype(o_ref.dtype)

    bounded = bound_ref[0] <= ATTN_EXP2_SAFE_RANGE
    pl.when(bounded)(bounded_path)
    pl.when(jnp.logical_not(bounded))(online_path)


def _attention(score_bound, q, k_ctx, vt_ctx, k_lat, vt_lat):
    b, nh, s, dk = q.shape
    nc = k_ctx.shape[2]
    dv = vt_lat.shape[2]
    assert dv == LANES
    tq = ATTN_Q_TILE

    def per_head(rows, cols):
        return pl.BlockSpec((None, None, rows, cols), lambda bi, hi, i: (bi, hi, 0, 0))

    return pl.pallas_call(
        _attention_kernel,
        grid=(b, nh, s // tq),
        in_specs=[pl.BlockSpec(memory_space=pltpu.SMEM),
                  pl.BlockSpec((None, None, tq, dk), lambda bi, hi, i: (bi, hi, i, 0)),
                  per_head(nc, dk), per_head(dv, nc), per_head(s, dk), per_head(dv, s)],
        out_specs=pl.BlockSpec((None, tq, dv), lambda bi, hi, i: (bi, i, hi)),
        out_shape=jax.ShapeDtypeStruct((b, s, nh * dv), BF16),
        scratch_shapes=[pltpu.VMEM((SUBLANES, tq), F32), pltpu.VMEM((dv, tq), F32),
                        pltpu.VMEM((tq, LANES), F32), pltpu.VMEM((tq, LANES), F32),
                        pltpu.VMEM((tq, dv), F32)],
        compiler_params=pltpu.CompilerParams(
            dimension_semantics=("arbitrary", "arbitrary", "arbitrary"),
            vmem_limit_bytes=VMEM_LIMIT_BYTES),
        name="attention",
    )(score_bound, q, k_ctx, vt_ctx, k_lat, vt_lat)


def _merge_kernel(x_ref, mod_ref, bx_ref, u_ref, u_prev_ref, u_next_ref, gc_ref, ga_ref,
                  attn_ref, convw_ref, convb_ref, w_conv_out_ref, w_attn_o_ref, w_out_ref,
                  o_ref):
    i = pl.program_id(1)
    tm, d = x_ref.shape
    y_attn = jnp.dot(attn_ref[...], w_attn_o_ref[...], preferred_element_type=F32)
    u = u_ref[...].astype(F32)
    prev_row = u_prev_ref[SUBLANES - 1:SUBLANES, :].astype(F32) * (i > 0).astype(F32)
    next_row = u_next_ref[0:1, :].astype(F32) * (i < pl.num_programs(1) - 1).astype(F32)
    row = lax.broadcasted_iota(jnp.int32, (tm, d), 0)
    u_m1 = jnp.where(row == 0, prev_row, pltpu.roll(u, 1, axis=0))
    u_p1 = jnp.where(row == tm - 1, next_row, pltpu.roll(u, tm - 1, axis=0))
    cw = convw_ref[...]
    conv = u_m1 * cw[0:1, :] + u * cw[1:2, :] + u_p1 * cw[2:3, :] + convb_ref[...]
    z = (bx_ref[...].astype(F32) * conv).astype(BF16)
    y_conv = jnp.dot(z, w_conv_out_ref[...], preferred_element_type=F32)
    merged = (gc_ref[...].astype(F32) * y_conv + ga_ref[...].astype(F32) * y_attn).astype(BF16)
    y = jnp.dot(merged, w_out_ref[...], preferred_element_type=F32)
    g1 = mod_ref[:, 2 * d:3 * d]
    o_ref[...] = x_ref[...] + g1 * y


def _merge(x, mod_x, bx, u, gc, ga, attn, conv_w, conv_b, w_conv_out, w_attn_o, w_out):
    b, s, d = x.shape
    tm = TOKEN_TILE
    hb = tm // SUBLANES
    last_hb = s // SUBLANES - 1
    tok = pl.BlockSpec((None, tm, d), lambda bi, i: (bi, i, 0))
    prev = pl.BlockSpec((None, SUBLANES, d), lambda bi, i: (bi, jnp.maximum(i * hb - 1, 0), 0))
    nxt = pl.BlockSpec((None, SUBLANES, d),
                       lambda bi, i: (bi, jnp.minimum((i + 1) * hb, last_hb), 0))
    return pl.pallas_call(
        _merge_kernel,
        grid=(b, s // tm),
        in_specs=[tok,
                  pl.BlockSpec((None, 1, mod_x.shape[-1]), lambda bi, i: (bi, 0, 0)),
                  tok, tok, prev, nxt, tok, tok, tok,
                  _const_spec(conv_w.shape), _const_spec(conv_b.shape),
                  _const_spec(w_conv_out.shape), _const_spec(w_attn_o.shape),
                  _const_spec(w_out.shape)],
        out_specs=tok,
        out_shape=jax.ShapeDtypeStruct((b, s, d), F32),
        compiler_params=pltpu.CompilerParams(dimension_semantics=("arbitrary", "arbitrary"),
                                             vmem_limit_bytes=VMEM_LIMIT_BYTES),
        name="merge",
    )(x, mod_x, bx, u, u, u, gc, ga, attn, conv_w, conv_b, w_conv_out, w_attn_o, w_out)


def _ffn_kernel(x_ref, mod_ref, nffn_ref, w_in_ref, w_out_ref, o_ref, act_ref):
    d = x_ref.shape[-1]
    d_ff = w_out_ref.shape[0]
    x = x_ref[...]
    mod = mod_ref[...]
    h = _modulated_norm(x, nffn_ref[...], mod[:, 3 * d:4 * d], mod[:, 4 * d:5 * d]).astype(BF16)
    for j in range(d_ff // FFN_CHUNK):
        c0 = j * FFN_CHUNK
        gate = jnp.dot(h, w_in_ref[:, c0:c0 + FFN_CHUNK], preferred_element_type=F32)
        up = jnp.dot(h, w_in_ref[:, d_ff + c0:d_ff + c0 + FFN_CHUNK],
                     preferred_element_type=F32)
        act_ref[:, c0:c0 + FFN_CHUNK] = (gate * _sigmoid(gate) * up).astype(BF16)
    y = jnp.dot(act_ref[...], w_out_ref[...], preferred_element_type=F32)
    o_ref[...] = x + mod[:, 5 * d:6 * d] * y


def _ffn(x, mod_x, nffn, w_ffn_in, w_ffn_out):
    b, s, d = x.shape
    tm = TOKEN_TILE
    d_ff = w_ffn_out.shape[0]
    tok = pl.BlockSpec((None, tm, d), lambda bi, i: (bi, i, 0))
    return pl.pallas_call(
        _ffn_kernel,
        grid=(b, s // tm),
        in_specs=[tok,
                  pl.BlockSpec((None, 1, mod_x.shape[-1]), lambda bi, i: (bi, 0, 0)),
                  _const_spec(nffn.shape), _const_spec(w_ffn_in.shape),
                  _const_spec(w_ffn_out.shape)],
        out_specs=tok,
        out_shape=jax.ShapeDtypeStruct((b, s, d), F32),
        scratch_shapes=[pltpu.VMEM((tm, d_ff), BF16)],
        compiler_params=pltpu.CompilerParams(dimension_semantics=("arbitrary", "arbitrary"),
                                             vmem_limit_bytes=VMEM_LIMIT_BYTES),
        name="ffn",
    )(x, mod_x, nffn, w_ffn_in, w_ffn_out)


def _rope_table(seq_len):
    quarter = QK_ROPE_DIM // 4
    freqs = ROPE_THETA ** (-np.arange(quarter, dtype=np.float64) / quarter)
    t = np.arange(seq_len)
    ang = [(t // GRID_W)[:, None] * freqs, (t % GRID_W)[:, None] * freqs]
    cos = [np.cos(a) for a in ang]
    sin = [np.sin(a) for a in ang]
    c = np.concatenate([cos[0], cos[0], cos[1], cos[1]], axis=-1)
    s = np.concatenate([-sin[0], sin[0], -sin[1], sin[1]], axis=-1)
    return jnp.asarray(np.concatenate([c, s], axis=-1), F32)


def _swap_halves(t):
    q = QK_ROPE_DIM // 4
    return jnp.concatenate([t[..., q:2 * q], t[..., :q], t[..., 3 * q:], t[..., 2 * q:3 * q]],
                           axis=-1)


def _norm_gain_pair(g):
    rope = g[QK_NOPE_DIM:]
    return jnp.concatenate([g[:QK_NOPE_DIM], rope, _swap_halves(rope)])[None, :].astype(F32)


def kernel(x, c, ctx, c_ctx, w_mod, b_mod, norm_mix, norm_ffn, w_in, b_gate, conv_w, conv_b,
           w_conv_out, q_a_norm, w_q_b, kv_a_norm, w_kv_b, q_norm, k_norm, w_attn_o, w_out,
           w_ffn_in, w_ffn_out):
    depth = w_mod.shape[0]
    assert depth == 1, "context stream update is only needed between layers"
    b, s, d = x.shape
    n_ctx = ctx.shape[1]
    assert s % TOKEN_TILE == 0 and s % ATTN_Q_TILE == 0 and s % ATTN_KV_TILE == 0
    assert b + 1 <= SUBLANES

    cond = jnp.zeros((SUBLANES, d), F32).at[:b].set(c).at[b].set(c_ctx)
    mod = _adaln(cond, w_mod, b_mod)
    mod_x = mod[:b, None, :]
    mod_ctx = mod[b:b + 1]

    wi = w_in[0]
    o_q = 3 * d
    o_kv = o_q + Q_LORA_RANK
    o_kr = o_kv + KV_LORA_RANK
    o_gc = o_kr + QK_ROPE_DIM
    w_main = jnp.concatenate([wi[:, :3 * d], wi[:, o_gc:]], axis=1).astype(BF16)
    w_qa = wi[:, o_q:o_kv].astype(BF16)
    w_kr = wi[:, o_kr:o_gc]
    w_kv_in = jnp.concatenate([wi[:, o_kv:o_kr], w_kr, _swap_halves(w_kr)], axis=1).astype(BF16)
    wq = w_q_b[0].reshape(Q_LORA_RANK, N_HEADS, QK_HEAD_DIM)
    wq = jnp.concatenate([wq, _swap_halves(wq[..., QK_NOPE_DIM:])], axis=-1)
    wqb = wq.reshape(Q_LORA_RANK, N_HEADS * 2 * LANES).astype(BF16)
    wkv = w_kv_b[0].reshape(KV_LORA_RANK, N_HEADS, QK_NOPE_DIM + V_HEAD_DIM)
    w_k = wkv[..., :QK_NOPE_DIM].reshape(KV_LORA_RANK, -1).astype(BF16)
    w_vt = wkv[..., QK_NOPE_DIM:].reshape(KV_LORA_RANK, -1).T.astype(BF16)
    gq = _norm_gain_pair(q_norm[0])
    gk = _norm_gain_pair(k_norm[0])
    table = _rope_table(s)
    table_ctx = jnp.concatenate([jnp.ones((n_ctx, QK_ROPE_DIM), F32),
                                 jnp.zeros((n_ctx, QK_ROPE_DIM), F32)], axis=1)
    nmix = norm_mix[0][None, :]
    kvan = kv_a_norm[0][None, :]

    k_ctx, vt_ctx = _ctx_kv(ctx, mod_ctx, nmix, w_kv_in, kvan, w_k, w_vt, gk, table_ctx)
    bx, u, gc, ga, q, k_lat, vt_lat = _in_proj(
        x, mod_x, nmix, w_main, w_qa, w_kv_in, b_gate[0][None, :], q_a_norm[0][None, :], wqb,
        kvan, w_k, w_vt, gq, gk, table)
    score_bound = (QK_HEAD_DIM * SM_SCALE_LOG2 * BF16_ROUNDING_MARGIN
                   * jnp.max(jnp.abs(q_norm[0])) * jnp.max(jnp.abs(k_norm[0]))).reshape(1)
    attn = _attention(score_bound.astype(F32), q, k_ctx, vt_ctx, k_lat, vt_lat)
    x_mid = _merge(x, mod_x, bx, u, gc, ga, attn, conv_w[0], conv_b[0][None, :],
                   w_conv_out[0].astype(BF16), w_attn_o[0].astype(BF16), w_out[0].astype(BF16))
    return _ffn(x_mid, mod_x, norm_ffn[0][None, :], w_ffn_in[0].astype(BF16),
                w_ffn_out[0].astype(BF16))
```

```python
import functools

import jax
import jax.numpy as jnp
import numpy as np
from jax import lax
from jax.experimental import pallas as pl
from jax.experimental.pallas import tpu as pltpu

F32 = jnp.float32
BF16 = jnp.bfloat16

N_HEADS = 8
QK_NOPE_DIM = 128
QK_ROPE_DIM = 64
QK_HEAD_DIM = QK_NOPE_DIM + QK_ROPE_DIM
V_HEAD_DIM = 128
Q_LORA_RANK = 384
KV_LORA_RANK = 256
GRID_W = 64
ROPE_THETA = 10000.0
NORM_EPS = 1e-6
MOD_CHUNKS = 6
SM_SCALE_LOG2 = float(QK_HEAD_DIM ** -0.5 * np.log2(np.e))
ATTN_EXP2_SAFE_RANGE = 40.0
BF16_ROUNDING_MARGIN = 1.01

LANES = 128
SUBLANES = 8
CONTRACT_LAST = (((1,), (1,)), ((), ()))
VMEM_LIMIT_BYTES = 56 * 1024 * 1024

TOKEN_TILE = 512
ATTN_Q_TILE = 1024
ATTN_Q_SUB = 256
ATTN_KV_TILE = 2048
ATTN_KV_TILE_ONLINE = 512
ATTN_KV_UNROLL_ONLINE = 2
ATTN_LOOKAHEAD = 2
ADALN_N_TILE = 1536
FFN_CHUNK = 256


def _rsqrt_mean(ss, n):
    return lax.rsqrt(ss * (1.0 / n) + NORM_EPS)


def _sigmoid(t):
    return 1.0 / (1.0 + jnp.exp(-t))


def _const_spec(shape):
    nd = len(shape)
    return pl.BlockSpec(shape, lambda *_: (0,) * nd, pipeline_mode=pl.Buffered(1))


def _adaln_kernel(cond_ref, w_ref, b_ref, o_ref):
    c = cond_ref[...]
    a = c * _sigmoid(c)
    o_ref[...] = jnp.dot(a, w_ref[...], preferred_element_type=F32,
                         precision=lax.Precision.HIGHEST) + b_ref[...]


def _adaln(cond, w_mod, b_mod):
    rows, d = cond.shape
    n = w_mod.shape[-1]
    return pl.pallas_call(
        _adaln_kernel,
        grid=(n // ADALN_N_TILE,),
        in_specs=[pl.BlockSpec((rows, d), lambda j: (0, 0)),
                  pl.BlockSpec((None, d, ADALN_N_TILE), lambda j: (0, 0, j)),
                  pl.BlockSpec((1, ADALN_N_TILE), lambda j: (0, j))],
        out_specs=pl.BlockSpec((rows, ADALN_N_TILE), lambda j: (0, j)),
        out_shape=jax.ShapeDtypeStruct((rows, n), F32),
        compiler_params=pltpu.CompilerParams(dimension_semantics=("arbitrary",),
                                             vmem_limit_bytes=VMEM_LIMIT_BYTES),
        name="adaln",
    )(cond, w_mod, b_mod)


def _modulated_norm(x, gain, shift, scale):
    d = x.shape[-1]
    r = _rsqrt_mean(jnp.sum(x * x, axis=-1, keepdims=True), d)
    return (x * r * gain) * (1.0 + scale) + shift


def _rope_pair(t2, table):
    w = t2 * table
    return w + pltpu.roll(w, QK_ROPE_DIM, axis=1)


def _kv_matmuls(h, w_kv_in_ref, kvan_ref, w_k_ref, w_vt_ref):
    kvp = jnp.dot(h, w_kv_in_ref[...], preferred_element_type=F32)
    kv_a = kvp[:, :KV_LORA_RANK]
    r = _rsqrt_mean(jnp.sum(kv_a * kv_a, axis=-1, keepdims=True), KV_LORA_RANK)
    kv_n = (kv_a * r * kvan_ref[...]).astype(BF16)
    k_nope = jnp.dot(kv_n, w_k_ref[...], preferred_element_type=F32)
    v_t = lax.dot_general(w_vt_ref[...], kv_n, CONTRACT_LAST, preferred_element_type=F32)
    return k_nope, v_t, kvp[:, KV_LORA_RANK:]


def _kv_finish(kv, v_t, kr2, gk_ref, table, k_ref, vt_ref):
    gk = gk_ref[...]
    ss_rope = 0.5 * jnp.sum(kr2 * kr2, axis=-1, keepdims=True)
    kr = _rope_pair(kr2 * gk[:, QK_NOPE_DIM:], table)[:, :QK_ROPE_DIM]
    for hd in range(N_HEADS):
        k_nope = kv[:, hd * QK_NOPE_DIM:(hd + 1) * QK_NOPE_DIM]
        ss = jnp.sum(k_nope * k_nope, axis=-1, keepdims=True) + ss_rope
        rh = _rsqrt_mean(ss, QK_HEAD_DIM)
        k_ref[hd, :, :QK_NOPE_DIM] = (k_nope * rh * gk[:, :QK_NOPE_DIM]).astype(BF16)
        k_ref[hd, :, QK_NOPE_DIM:] = (kr * rh).astype(BF16)
        vt_ref[hd] = v_t[hd * V_HEAD_DIM:(hd + 1) * V_HEAD_DIM, :].astype(BF16)


def _ctx_kv_kernel(ctx_ref, mod_ref, nmix_ref, w_kv_in_ref, kvan_ref, w_k_ref, w_vt_ref, gk_ref,
                   table_ref, k_ref, vt_ref):
    d = ctx_ref.shape[-1]
    mod = mod_ref[...]
    h = _modulated_norm(ctx_ref[...], nmix_ref[...], mod[:, :d], mod[:, d:2 * d]).astype(BF16)
    k_nope, v_t, kr2 = _kv_matmuls(h, w_kv_in_ref, kvan_ref, w_k_ref, w_vt_ref)
    _kv_finish(k_nope, v_t, kr2, gk_ref, table_ref[...], k_ref, vt_ref)


def _ctx_kv(ctx, mod_ctx, nmix, w_kv_in, kvan, w_k, w_vt, gk, table):
    b, n, d = ctx.shape
    return pl.pallas_call(
        _ctx_kv_kernel,
        grid=(b,),
        in_specs=[pl.BlockSpec((None, n, d), lambda i: (i, 0, 0)),
                  _const_spec(mod_ctx.shape), _const_spec(nmix.shape),
                  _const_spec(w_kv_in.shape), _const_spec(kvan.shape),
                  _const_spec(w_k.shape), _const_spec(w_vt.shape), _const_spec(gk.shape),
                  _const_spec(table.shape)],
        out_specs=[pl.BlockSpec((None, N_HEADS, n, QK_HEAD_DIM), lambda i: (i, 0, 0, 0)),
                   pl.BlockSpec((None, N_HEADS, V_HEAD_DIM, n), lambda i: (i, 0, 0, 0))],
        out_shape=[jax.ShapeDtypeStruct((b, N_HEADS, n, QK_HEAD_DIM), BF16),
                   jax.ShapeDtypeStruct((b, N_HEADS, V_HEAD_DIM, n), BF16)],
        compiler_params=pltpu.CompilerParams(dimension_semantics=("arbitrary",),
                                             vmem_limit_bytes=VMEM_LIMIT_BYTES),
        name="ctx_kv",
    )(ctx, mod_ctx, nmix, w_kv_in, kvan, w_k, w_vt, gk, table)


def _in_proj_kernel(x_ref, mod_ref, nmix_ref, w_all_ref, w_gate_ref, w_kv_in_ref, bgate_ref,
                    qan_ref, wqb_ref, kvan_ref, w_k_ref, w_vt_ref, gq_ref, gk_ref, table_ref,
                    bx_ref, u_ref, gc_ref, ga_ref, q_ref, k_ref, vt_ref):
    d = x_ref.shape[-1]
    mod = mod_ref[...]
    h = _modulated_norm(x_ref[...], nmix_ref[...], mod[:, :d], mod[:, d:2 * d]).astype(BF16)
    table = table_ref[...]

    def proj(w_ref, j):
        return jnp.dot(h, w_ref[:, j * d:(j + 1) * d], preferred_element_type=F32)

    q_a = jnp.dot(h, w_all_ref[:, 3 * d:3 * d + Q_LORA_RANK], preferred_element_type=F32)
    r = _rsqrt_mean(jnp.sum(q_a * q_a, axis=-1, keepdims=True), Q_LORA_RANK)
    q_n = (q_a * r * qan_ref[...]).astype(BF16)
    q = jnp.dot(q_n, wqb_ref[...], preferred_element_type=F32)
    k_nope, v_t, kr2 = _kv_matmuls(h, w_kv_in_ref, kvan_ref, w_k_ref, w_vt_ref)

    bx_ref[...] = proj(w_all_ref, 0).astype(BF16)
    u_ref[...] = (proj(w_all_ref, 1) * proj(w_all_ref, 2)).astype(BF16)
    bgate = bgate_ref[...]
    gc_ref[...] = _sigmoid(proj(w_gate_ref, 0) + bgate[:, :d]).astype(BF16)
    ga_ref[...] = _sigmoid(proj(w_gate_ref, 1) + bgate[:, d:]).astype(BF16)

    gq = gq_ref[...]
    table_q = table * gq[:, QK_NOPE_DIM:]
    sm_scale = SM_SCALE_LOG2
    hw = 2 * LANES
    for hd in range(N_HEADS):
        q_nope = q[:, hd * hw:hd * hw + QK_NOPE_DIM]
        q_r2 = q[:, hd * hw + QK_NOPE_DIM:(hd + 1) * hw]
        ss = jnp.sum(q_nope * q_nope + 0.5 * (q_r2 * q_r2), axis=-1, keepdims=True)
        rh = _rsqrt_mean(ss, QK_HEAD_DIM) * sm_scale
        q_ref[hd, :, :QK_NOPE_DIM] = (q_nope * rh * gq[:, :QK_NOPE_DIM]).astype(BF16)
        q_ref[hd, :, QK_NOPE_DIM:] = (_rope_pair(q_r2 * rh, table_q)[:, :QK_ROPE_DIM]).astype(BF16)

    _kv_finish(k_nope, v_t, kr2, gk_ref, table, k_ref, vt_ref)


def _in_proj(x, mod_x, nmix, w_all, w_gate, w_kv_in, bgate, qan, wqb, kvan, w_k, w_vt, gq, gk,
             table):
    b, s, d = x.shape
    tm = TOKEN_TILE
    tok = pl.BlockSpec((None, tm, d), lambda bi, i: (bi, i, 0))

    def head_spec(width):
        return pl.BlockSpec((None, N_HEADS, tm, width), lambda bi, i: (bi, 0, i, 0))

    tok_shape = jax.ShapeDtypeStruct((b, s, d), BF16)
    return pl.pallas_call(
        _in_proj_kernel,
        grid=(b, s // tm),
        in_specs=[tok,
                  pl.BlockSpec((None, 1, mod_x.shape[-1]), lambda bi, i: (bi, 0, 0)),
                  _const_spec(nmix.shape), _const_spec(w_all.shape), _const_spec(w_gate.shape),
                  _const_spec(w_kv_in.shape), _const_spec(bgate.shape), _const_spec(qan.shape),
                  _const_spec(wqb.shape), _const_spec(kvan.shape), _const_spec(w_k.shape),
                  _const_spec(w_vt.shape), _const_spec(gq.shape), _const_spec(gk.shape),
                  pl.BlockSpec((tm, 2 * QK_ROPE_DIM), lambda bi, i: (i, 0))],
        out_specs=[tok, tok, tok, tok,
                   head_spec(QK_HEAD_DIM), head_spec(QK_HEAD_DIM),
                   pl.BlockSpec((None, N_HEADS, V_HEAD_DIM, tm), lambda bi, i: (bi, 0, 0, i))],
        out_shape=[tok_shape, tok_shape, tok_shape, tok_shape,
                   jax.ShapeDtypeStruct((b, N_HEADS, s, QK_HEAD_DIM), BF16),
                   jax.ShapeDtypeStruct((b, N_HEADS, s, QK_HEAD_DIM), BF16),
                   jax.ShapeDtypeStruct((b, N_HEADS, V_HEAD_DIM, s), BF16)],
        compiler_params=pltpu.CompilerParams(dimension_semantics=("arbitrary", "arbitrary"),
                                             vmem_limit_bytes=VMEM_LIMIT_BYTES),
        name="in_proj",
    )(x, mod_x, nmix, w_all, w_gate, w_kv_in, bgate, qan, wqb, kvan, w_k, w_vt, gq, gk, table)


def _attention_kernel(bound_ref, q_ref, kc_ref, vct_ref, kl_ref, vlt_ref, o_ref,
                      lt_ref, acct_ref, m_ref, l_ref, acc_ref):
    tq = q_ref.shape[0]

    def scores_t(unit):
        k_ref, _, k0, tk = unit
        return lax.dot_general(k_ref[k0:k0 + tk, :], q_ref[...], CONTRACT_LAST,
                               preferred_element_type=F32)

    def finish_bounded(unit, st):
        _, vt_ref, k0, tk = unit
        pt = jnp.exp2(st)
        lt_ref[...] += pt.reshape(tk // SUBLANES, SUBLANES, tq).sum(axis=0)
        acct_ref[...] += jnp.dot(vt_ref[:, k0:k0 + tk], pt.astype(BF16),
                                 preferred_element_type=F32)

    def bounded_path():
        lt_ref[...] = jnp.zeros(lt_ref.shape, F32)
        acct_ref[...] = jnp.zeros(acct_ref.shape, F32)
        tk = ATTN_KV_TILE
        units = [(kc_ref, vct_ref, 0, kc_ref.shape[0])]
        units += [(kl_ref, vlt_ref, k0, tk) for k0 in range(0, kl_ref.shape[0], tk)]
        pending = [scores_t(u) for u in units[:ATTN_LOOKAHEAD]]
        for i, unit in enumerate(units):
            if i + ATTN_LOOKAHEAD < len(units):
                pending.append(scores_t(units[i + ATTN_LOOKAHEAD]))
            finish_bounded(unit, pending[i])
        l = jnp.sum(lt_ref[...], axis=0, keepdims=True)
        o_ref[...] = (acct_ref[...] / l).T.astype(o_ref.dtype)

    def scores(unit):
        k_ref, _, k0, tk, r0 = unit
        return lax.dot_general(q_ref[r0:r0 + ATTN_Q_SUB, :], k_ref[pl.ds(k0, tk), :],
                               CONTRACT_LAST, preferred_element_type=F32)

    def finish_online(unit, s):
        _, vt_ref, k0, tk, r0 = unit
        rows = slice(r0, r0 + ATTN_Q_SUB)
        cols = [s[:, c:c + LANES] for c in range(0, tk, LANES)]
        m_old = m_ref[rows, :]
        row_max = jnp.max(functools.reduce(jnp.maximum, cols), axis=-1, keepdims=True)
        m_new = jnp.maximum(m_old, row_max)
        alpha = jnp.exp2(m_old - m_new)
        ps = [jnp.exp2(col - m_new) for col in cols]
        l_ref[rows, :] = alpha * l_ref[rows, :] + functools.reduce(jnp.add, ps)
        p = jnp.concatenate([x.astype(BF16) for x in ps], axis=1)
        pv = lax.dot_general(p, vt_ref[:, pl.ds(k0, tk)], CONTRACT_LAST,
                             preferred_element_type=F32)
        acc_ref[rows, :] = alpha * acc_ref[rows, :] + pv
        m_ref[rows, :] = m_new

    def run_online(units):
        pending = [scores(u) for u in units[:ATTN_LOOKAHEAD]]
        for i, unit in enumerate(units):
            if i + ATTN_LOOKAHEAD < len(units):
                pending.append(scores(units[i + ATTN_LOOKAHEAD]))
            finish_online(unit, pending[i])

    def units_of(k_ref, vt_ref, starts, tk):
        return [(k_ref, vt_ref, k0, tk, r0) for k0 in starts for r0 in range(0, tq, ATTN_Q_SUB)]

    def online_path():
        m_ref[...] = jnp.full(m_ref.shape, -jnp.inf, F32)
        l_ref[...] = jnp.zeros(l_ref.shape, F32)
        acc_ref[...] = jnp.zeros(acc_ref.shape, F32)
        tk = ATTN_KV_TILE_ONLINE
        span = tk * ATTN_KV_UNROLL_ONLINE
        run_online(units_of(kc_ref, vct_ref, [0], kc_ref.shape[0]))

        def body(j, carry):
            base = pl.multiple_of(j * span, span)
            run_online(units_of(kl_ref, vlt_ref,
                                [base + u * tk for u in range(ATTN_KV_UNROLL_ONLINE)], tk))
            return carry

        lax.fori_loop(0, kl_ref.shape[0] // span, body, 0)
        l = jnp.sum(l_ref[...], axis=-1, keepdims=True)
        o_ref[...] = (acc_ref[...] / l).astype(o_ref.dtype)

    bounded = bound_ref[0] <= ATTN_EXP2_SAFE_RANGE
    pl.when(bounded)(bounded_path)
    pl.when(jnp.logical_not(bounded))(online_path)


def _attention(score_bound, q, k_ctx, vt_ctx, k_lat, vt_lat):
    b, nh, s, dk = q.shape
    nc = k_ctx.shape[2]
    dv = vt_lat.shape[2]
    assert dv == LANES
    tq = ATTN_Q_TILE

    def per_head(rows, cols):
        return pl.BlockSpec((None, None, rows, cols), lambda bi, hi, i: (bi, hi, 0, 0))

    return pl.pallas_call(
        _attention_kernel,
        grid=(b, nh, s // tq),
        in_specs=[pl.BlockSpec(memory_space=pltpu.SMEM),
                  pl.BlockSpec((None, None, tq, dk), lambda bi, hi, i: (bi, hi, i, 0)),
                  per_head(nc, dk), per_head(dv, nc), per_head(s, dk), per_head(dv, s)],
        out_specs=pl.BlockSpec((None, tq, dv), lambda bi, hi, i: (bi, i, hi)),
        out_shape=jax.ShapeDtypeStruct((b, s, nh * dv), BF16),
        scratch_shapes=[pltpu.VMEM((SUBLANES, tq), F32), pltpu.VMEM((dv, tq), F32),
                        pltpu.VMEM((tq, LANES), F32), pltpu.VMEM((tq, LANES), F32),
                        pltpu.VMEM((tq, dv), F32)],
        compiler_params=pltpu.CompilerParams(
            dimension_semantics=("arbitrary", "arbitrary", "arbitrary"),
            vmem_limit_bytes=VMEM_LIMIT_BYTES),
        name="attention",
    )(score_bound, q, k_ctx, vt_ctx, k_lat, vt_lat)


def _merge_kernel(x_ref, mod_ref, bx_ref, u_ref, u_prev_ref, u_next_ref, gc_ref, ga_ref,
                  attn_ref, convw_ref, convb_ref, w_conv_out_ref, w_attn_o_ref, w_out_ref,
                  o_ref):
    i = pl.program_id(1)
    tm, d = x_ref.shape
    y_attn = jnp.dot(attn_ref[...], w_attn_o_ref[...], preferred_element_type=F32)
    u = u_ref[...].astype(F32)
    prev_row = u_prev_ref[SUBLANES - 1:SUBLANES, :].astype(F32) * (i > 0).astype(F32)
    next_row = u_next_ref[0:1, :].astype(F32) * (i < pl.num_programs(1) - 1).astype(F32)
    row = lax.broadcasted_iota(jnp.int32, (tm, d), 0)
    u_m1 = jnp.where(row == 0, prev_row, pltpu.roll(u, 1, axis=0))
    u_p1 = jnp.where(row == tm - 1, next_row, pltpu.roll(u, tm - 1, axis=0))
    cw = convw_ref[...]
    conv = u_m1 * cw[0:1, :] + u * cw[1:2, :] + u_p1 * cw[2:3, :] + convb_ref[...]
    z = (bx_ref[...].astype(F32) * conv).astype(BF16)
    y_conv = jnp.dot(z, w_conv_out_ref[...], preferred_element_type=F32)
    merged = (gc_ref[...].astype(F32) * y_conv + ga_ref[...].astype(F32) * y_attn).astype(BF16)
    y = jnp.dot(merged, w_out_ref[...], preferred_element_type=F32)
    g1 = mod_ref[:, 2 * d:3 * d]
    o_ref[...] = x_ref[...] + g1 * y


def _merge(x, mod_x, bx, u, gc, ga, attn, conv_w, conv_b, w_conv_out, w_attn_o, w_out):
    b, s, d = x.shape
    tm = TOKEN_TILE
    hb = tm // SUBLANES
    last_hb = s // SUBLANES - 1
    tok = pl.BlockSpec((None, tm, d), lambda bi, i: (bi, i, 0))
    prev = pl.BlockSpec((None, SUBLANES, d), lambda bi, i: (bi, jnp.maximum(i * hb - 1, 0), 0))
    nxt = pl.BlockSpec((None, SUBLANES, d),
                       lambda bi, i: (bi, jnp.minimum((i + 1) * hb, last_hb), 0))
    return pl.pallas_call(
        _merge_kernel,
        grid=(b, s // tm),
        in_specs=[tok,
                  pl.BlockSpec((None, 1, mod_x.shape[-1]), lambda bi, i: (bi, 0, 0)),
                  tok, tok, prev, nxt, tok, tok, tok,
                  _const_spec(conv_w.shape), _const_spec(conv_b.shape),
                  _const_spec(w_conv_out.shape), _const_spec(w_attn_o.shape),
                  _const_spec(w_out.shape)],
        out_specs=tok,
        out_shape=jax.ShapeDtypeStruct((b, s, d), F32),
        compiler_params=pltpu.CompilerParams(dimension_semantics=("arbitrary", "arbitrary"),
                                             vmem_limit_bytes=VMEM_LIMIT_BYTES),
        name="merge",
    )(x, mod_x, bx, u, u, u, gc, ga, attn, conv_w, conv_b, w_conv_out, w_attn_o, w_out)


def _ffn_kernel(x_ref, mod_ref, nffn_ref, w_in_ref, w_out_ref, o_ref, act_ref):
    d = x_ref.shape[-1]
    d_ff = w_out_ref.shape[0]
    x = x_ref[...]
    mod = mod_ref[...]
    h = _modulated_norm(x, nffn_ref[...], mod[:, 3 * d:4 * d], mod[:, 4 * d:5 * d]).astype(BF16)
    for j in range(d_ff // FFN_CHUNK):
        c0 = j * FFN_CHUNK
        gate = jnp.dot(h, w_in_ref[:, c0:c0 + FFN_CHUNK], preferred_element_type=F32)
        up = jnp.dot(h, w_in_ref[:, d_ff + c0:d_ff + c0 + FFN_CHUNK],
                     preferred_element_type=F32)
        act_ref[:, c0:c0 + FFN_CHUNK] = (gate * _sigmoid(gate) * up).astype(BF16)
    y = jnp.dot(act_ref[...], w_out_ref[...], preferred_element_type=F32)
    o_ref[...] = x + mod[:, 5 * d:6 * d] * y


def _ffn(x, mod_x, nffn, w_ffn_in, w_ffn_out):
    b, s, d = x.shape
    tm = TOKEN_TILE
    d_ff = w_ffn_out.shape[0]
    tok = pl.BlockSpec((None, tm, d), lambda bi, i: (bi, i, 0))
    return pl.pallas_call(
        _ffn_kernel,
        grid=(b, s // tm),
        in_specs=[tok,
                  pl.BlockSpec((None, 1, mod_x.shape[-1]), lambda bi, i: (bi, 0, 0)),
                  _const_spec(nffn.shape), _const_spec(w_ffn_in.shape),
                  _const_spec(w_ffn_out.shape)],
        out_specs=tok,
        out_shape=jax.ShapeDtypeStruct((b, s, d), F32),
        scratch_shapes=[pltpu.VMEM((tm, d_ff), BF16)],
        compiler_params=pltpu.CompilerParams(dimension_semantics=("arbitrary", "arbitrary"),
                                             vmem_limit_bytes=VMEM_LIMIT_BYTES),
        name="ffn",
    )(x, mod_x, nffn, w_ffn_in, w_ffn_out)


def _rope_table(seq_len):
    quarter = QK_ROPE_DIM // 4
    freqs = ROPE_THETA ** (-np.arange(quarter, dtype=np.float64) / quarter)
    t = np.arange(seq_len)
    ang = [(t // GRID_W)[:, None] * freqs, (t % GRID_W)[:, None] * freqs]
    cos = [np.cos(a) for a in ang]
    sin = [np.sin(a) for a in ang]
    c = np.concatenate([cos[0], cos[0], cos[1], cos[1]], axis=-1)
    s = np.concatenate([-sin[0], sin[0], -sin[1], sin[1]], axis=-1)
    return jnp.asarray(np.concatenate([c, s], axis=-1), F32)


def _swap_halves(t):
    q = QK_ROPE_DIM // 4
    return jnp.concatenate([t[..., q:2 * q], t[..., :q], t[..., 3 * q:], t[..., 2 * q:3 * q]],
                           axis=-1)


def _norm_gain_pair(g):
    rope = g[QK_NOPE_DIM:]
    return jnp.concatenate([g[:QK_NOPE_DIM], rope, _swap_halves(rope)])[None, :].astype(F32)


def kernel(x, c, ctx, c_ctx, w_mod, b_mod, norm_mix, norm_ffn, w_in, b_gate, conv_w, conv_b,
           w_conv_out, q_a_norm, w_q_b, kv_a_norm, w_kv_b, q_norm, k_norm, w_attn_o, w_out,
           w_ffn_in, w_ffn_out):
    depth = w_mod.shape[0]
    assert depth == 1, "context stream update is only needed between layers"
    b, s, d = x.shape
    n_ctx = ctx.shape[1]
    assert s % TOKEN_TILE == 0 and s % ATTN_Q_TILE == 0 and s % ATTN_KV_TILE == 0
    assert s % (ATTN_KV_TILE_ONLINE * ATTN_KV_UNROLL_ONLINE) == 0
    assert b + 1 <= SUBLANES

    cond = jnp.zeros((SUBLANES, d), F32).at[:b].set(c).at[b].set(c_ctx)
    mod = _adaln(cond, w_mod, b_mod)
    mod_x = mod[:b, None, :]
    mod_ctx = mod[b:b + 1]

    wi = w_in[0]
    o_q = 3 * d
    o_kv = o_q + Q_LORA_RANK
    o_kr = o_kv + KV_LORA_RANK
    o_gc = o_kr + QK_ROPE_DIM
    assert o_q % LANES == 0 and o_kv % LANES == 0
    w_all = wi.astype(BF16)
    w_gate = w_all[:, o_gc:]
    w_kr = wi[:, o_kr:o_gc]
    w_kv_in = jnp.concatenate([wi[:, o_kv:o_kr], w_kr, _swap_halves(w_kr)], axis=1).astype(BF16)
    wq = w_q_b[0].reshape(Q_LORA_RANK, N_HEADS, QK_HEAD_DIM)
    wq = jnp.concatenate([wq, _swap_halves(wq[..., QK_NOPE_DIM:])], axis=-1)
    wqb = wq.reshape(Q_LORA_RANK, N_HEADS * 2 * LANES).astype(BF16)
    wkv = w_kv_b[0].reshape(KV_LORA_RANK, N_HEADS, QK_NOPE_DIM + V_HEAD_DIM)
    w_k = wkv[..., :QK_NOPE_DIM].reshape(KV_LORA_RANK, -1).astype(BF16)
    w_vt = wkv[..., QK_NOPE_DIM:].reshape(KV_LORA_RANK, -1).T.astype(BF16)
    gq = _norm_gain_pair(q_norm[0])
    gk = _norm_gain_pair(k_norm[0])
    table = _rope_table(s)
    table_ctx = jnp.concatenate([jnp.ones((n_ctx, QK_ROPE_DIM), F32),
                                 jnp.zeros((n_ctx, QK_ROPE_DIM), F32)], axis=1)
    nmix = norm_mix[0][None, :]
    kvan = kv_a_norm[0][None, :]

    k_ctx, vt_ctx = _ctx_kv(ctx, mod_ctx, nmix, w_kv_in, kvan, w_k, w_vt, gk, table_ctx)
    bx, u, gc, ga, q, k_lat, vt_lat = _in_proj(
        x, mod_x, nmix, w_all, w_gate, w_kv_in, b_gate[0][None, :], q_a_norm[0][None, :], wqb,
        kvan, w_k, w_vt, gq, gk, table)
    score_bound = (QK_HEAD_DIM * SM_SCALE_LOG2 * BF16_ROUNDING_MARGIN
                   * jnp.max(jnp.abs(q_norm[0])) * jnp.max(jnp.abs(k_norm[0]))).reshape(1)
    attn = _attention(score_bound.astype(F32), q, k_ctx, vt_ctx, k_lat, vt_lat)
    x_mid = _merge(x, mod_x, bx, u, gc, ga, attn, conv_w[0], conv_b[0][None, :],
                   w_conv_out[0].astype(BF16), w_attn_o[0].astype(BF16), w_out[0].astype(BF16))
    return _ffn(x_mid, mod_x, norm_ffn[0][None, :], w_ffn_in[0].astype(BF16),
                w_ffn_out[0].astype(BF16))
```

```python
import functools

import jax
import jax.numpy as jnp
import numpy as np
from jax import lax
from jax.experimental import pallas as pl
from jax.experimental.pallas import tpu as pltpu

F32 = jnp.float32
BF16 = jnp.bfloat16

N_HEADS = 8
QK_NOPE_DIM = 128
QK_ROPE_DIM = 64
QK_HEAD_DIM = QK_NOPE_DIM + QK_ROPE_DIM
V_HEAD_DIM = 128
Q_LORA_RANK = 384
KV_LORA_RANK = 256
GRID_W = 64
ROPE_THETA = 10000.0
NORM_EPS = 1e-6
MOD_CHUNKS = 6
SM_SCALE_LOG2 = float(QK_HEAD_DIM ** -0.5 * np.log2(np.e))
ATTN_EXP2_SAFE_RANGE = 40.0
BF16_ROUNDING_MARGIN = 1.01

LANES = 128
SUBLANES = 8
CONTRACT_LAST = (((1,), (1,)), ((), ()))
VMEM_LIMIT_BYTES = 56 * 1024 * 1024

TOKEN_TILE = 512
ATTN_Q_TILE = 1024
ATTN_Q_SUB = 256
ATTN_KV_TILE = 2048
ATTN_KV_TILE_ONLINE = 512
ATTN_KV_UNROLL_ONLINE = 2
ATTN_LOOKAHEAD = 2
ADALN_N_TILE = 1536
FFN_CHUNK = 256


def _rsqrt_mean(ss, n):
    return lax.rsqrt(ss * (1.0 / n) + NORM_EPS)


def _sigmoid(t):
    return 1.0 / (1.0 + jnp.exp(-t))


def _const_spec(shape):
    nd = len(shape)
    return pl.BlockSpec(shape, lambda *_: (0,) * nd, pipeline_mode=pl.Buffered(1))


def _adaln_kernel(cond_ref, w_ref, b_ref, o_ref):
    c = cond_ref[...]
    a = c * _sigmoid(c)
    o_ref[...] = jnp.dot(a, w_ref[...], preferred_element_type=F32,
                         precision=lax.Precision.HIGHEST) + b_ref[...]


def _adaln(cond, w_mod, b_mod):
    rows, d = cond.shape
    n = w_mod.shape[-1]
    return pl.pallas_call(
        _adaln_kernel,
        grid=(n // ADALN_N_TILE,),
        in_specs=[pl.BlockSpec((rows, d), lambda j: (0, 0)),
                  pl.BlockSpec((None, d, ADALN_N_TILE), lambda j: (0, 0, j)),
                  pl.BlockSpec((1, ADALN_N_TILE), lambda j: (0, j))],
        out_specs=pl.BlockSpec((rows, ADALN_N_TILE), lambda j: (0, j)),
        out_shape=jax.ShapeDtypeStruct((rows, n), F32),
        compiler_params=pltpu.CompilerParams(dimension_semantics=("arbitrary",),
                                             vmem_limit_bytes=VMEM_LIMIT_BYTES),
        name="adaln",
    )(cond, w_mod, b_mod)


def _modulated_norm(x, gain, shift, scale):
    d = x.shape[-1]
    r = _rsqrt_mean(jnp.sum(x * x, axis=-1, keepdims=True), d)
    return (x * r * gain) * (1.0 + scale) + shift


def _rope_pair(t2, table):
    w = t2 * table
    return w + pltpu.roll(w, QK_ROPE_DIM, axis=1)


def _kv_down(h, w_kv_in_ref):
    return jnp.dot(h, w_kv_in_ref[...], preferred_element_type=F32)


def _kv_up(kvp, kvan_ref, w_k_ref, w_vt_ref):
    kv_a = kvp[:, :KV_LORA_RANK]
    r = _rsqrt_mean(jnp.sum(kv_a * kv_a, axis=-1, keepdims=True), KV_LORA_RANK)
    kv_n = (kv_a * r * kvan_ref[...]).astype(BF16)
    k_nope = jnp.dot(kv_n, w_k_ref[...], preferred_element_type=F32)
    v_t = lax.dot_general(w_vt_ref[...], kv_n, CONTRACT_LAST, preferred_element_type=F32)
    return k_nope, v_t, kvp[:, KV_LORA_RANK:]


def _kv_finish(kv, v_t, kr2, gk_ref, table, k_ref, vt_ref):
    gk = gk_ref[...]
    ss_rope = 0.5 * jnp.sum(kr2 * kr2, axis=-1, keepdims=True)
    kr = _rope_pair(kr2 * gk[:, QK_NOPE_DIM:], table)[:, :QK_ROPE_DIM]
    for hd in range(N_HEADS):
        k_nope = kv[:, hd * QK_NOPE_DIM:(hd + 1) * QK_NOPE_DIM]
        ss = jnp.sum(k_nope * k_nope, axis=-1, keepdims=True) + ss_rope
        rh = _rsqrt_mean(ss, QK_HEAD_DIM)
        k_ref[hd, :, :QK_NOPE_DIM] = (k_nope * rh * gk[:, :QK_NOPE_DIM]).astype(BF16)
        k_ref[hd, :, QK_NOPE_DIM:] = (kr * rh).astype(BF16)
        vt_ref[hd] = v_t[hd * V_HEAD_DIM:(hd + 1) * V_HEAD_DIM, :].astype(BF16)


def _ctx_kv_kernel(ctx_ref, mod_ref, nmix_ref, w_kv_in_ref, kvan_ref, w_k_ref, w_vt_ref, gk_ref,
                   table_ref, k_ref, vt_ref):
    d = ctx_ref.shape[-1]
    mod = mod_ref[...]
    h = _modulated_norm(ctx_ref[...], nmix_ref[...], mod[:, :d], mod[:, d:2 * d]).astype(BF16)
    k_nope, v_t, kr2 = _kv_up(_kv_down(h, w_kv_in_ref), kvan_ref, w_k_ref, w_vt_ref)
    _kv_finish(k_nope, v_t, kr2, gk_ref, table_ref[...], k_ref, vt_ref)


def _ctx_kv(ctx, mod_ctx, nmix, w_kv_in, kvan, w_k, w_vt, gk, table):
    b, n, d = ctx.shape
    return pl.pallas_call(
        _ctx_kv_kernel,
        grid=(b,),
        in_specs=[pl.BlockSpec((None, n, d), lambda i: (i, 0, 0)),
                  _const_spec(mod_ctx.shape), _const_spec(nmix.shape),
                  _const_spec(w_kv_in.shape), _const_spec(kvan.shape),
                  _const_spec(w_k.shape), _const_spec(w_vt.shape), _const_spec(gk.shape),
                  _const_spec(table.shape)],
        out_specs=[pl.BlockSpec((None, N_HEADS, n, QK_HEAD_DIM), lambda i: (i, 0, 0, 0)),
                   pl.BlockSpec((None, N_HEADS, V_HEAD_DIM, n), lambda i: (i, 0, 0, 0))],
        out_shape=[jax.ShapeDtypeStruct((b, N_HEADS, n, QK_HEAD_DIM), BF16),
                   jax.ShapeDtypeStruct((b, N_HEADS, V_HEAD_DIM, n), BF16)],
        compiler_params=pltpu.CompilerParams(dimension_semantics=("arbitrary",),
                                             vmem_limit_bytes=VMEM_LIMIT_BYTES),
        name="ctx_kv",
    )(ctx, mod_ctx, nmix, w_kv_in, kvan, w_k, w_vt, gk, table)


def _in_proj_kernel(x_ref, mod_ref, nmix_ref, w_all_ref, w_gate_ref, w_kv_in_ref, bgate_ref,
                    qan_ref, wqb_ref, kvan_ref, w_k_ref, w_vt_ref, gq_ref, gk_ref, table_ref,
                    bx_ref, u_ref, gc_ref, ga_ref, q_ref, k_ref, vt_ref):
    d = x_ref.shape[-1]
    mod = mod_ref[...]
    h = _modulated_norm(x_ref[...], nmix_ref[...], mod[:, :d], mod[:, d:2 * d]).astype(BF16)
    table = table_ref[...]

    def proj(w_ref, j):
        return jnp.dot(h, w_ref[:, j * d:(j + 1) * d], preferred_element_type=F32)

    q_a = jnp.dot(h, w_all_ref[:, 3 * d:3 * d + Q_LORA_RANK], preferred_element_type=F32)
    kvp = _kv_down(h, w_kv_in_ref)
    bgate = bgate_ref[...]
    gc_ref[...] = _sigmoid(proj(w_gate_ref, 0) + bgate[:, :d]).astype(BF16)
    r = _rsqrt_mean(jnp.sum(q_a * q_a, axis=-1, keepdims=True), Q_LORA_RANK)
    q_n = (q_a * r * qan_ref[...]).astype(BF16)
    q = jnp.dot(q_n, wqb_ref[...], preferred_element_type=F32)
    k_nope, v_t, kr2 = _kv_up(kvp, kvan_ref, w_k_ref, w_vt_ref)
    ga_ref[...] = _sigmoid(proj(w_gate_ref, 1) + bgate[:, d:]).astype(BF16)

    gq = gq_ref[...]
    table_q = table * gq[:, QK_NOPE_DIM:]
    sm_scale = SM_SCALE_LOG2
    hw = 2 * LANES
    for hd in range(N_HEADS):
        q_nope = q[:, hd * hw:hd * hw + QK_NOPE_DIM]
        q_r2 = q[:, hd * hw + QK_NOPE_DIM:(hd + 1) * hw]
        ss = jnp.sum(q_nope * q_nope + 0.5 * (q_r2 * q_r2), axis=-1, keepdims=True)
        rh = _rsqrt_mean(ss, QK_HEAD_DIM) * sm_scale
        q_ref[hd, :, :QK_NOPE_DIM] = (q_nope * rh * gq[:, :QK_NOPE_DIM]).astype(BF16)
        q_ref[hd, :, QK_NOPE_DIM:] = (_rope_pair(q_r2 * rh, table_q)[:, :QK_ROPE_DIM]).astype(BF16)

    u_ref[...] = (proj(w_all_ref, 1) * proj(w_all_ref, 2)).astype(BF16)
    _kv_finish(k_nope, v_t, kr2, gk_ref, table, k_ref, vt_ref)
    bx_ref[...] = proj(w_all_ref, 0).astype(BF16)


def _in_proj(x, mod_x, nmix, w_all, w_gate, w_kv_in, bgate, qan, wqb, kvan, w_k, w_vt, gq, gk,
             table):
    b, s, d = x.shape
    tm = TOKEN_TILE
    tok = pl.BlockSpec((None, tm, d), lambda bi, i: (bi, i, 0))

    def head_spec(width):
        return pl.BlockSpec((None, N_HEADS, tm, width), lambda bi, i: (bi, 0, i, 0))

    tok_shape = jax.ShapeDtypeStruct((b, s, d), BF16)
    return pl.pallas_call(
        _in_proj_kernel,
        grid=(b, s // tm),
        in_specs=[tok,
                  pl.BlockSpec((None, 1, mod_x.shape[-1]), lambda bi, i: (bi, 0, 0)),
                  _const_spec(nmix.shape),
                  _const_spec((d, 3 * d + Q_LORA_RANK)), _const_spec(w_gate.shape),
                  _const_spec(w_kv_in.shape), _const_spec(bgate.shape), _const_spec(qan.shape),
                  _const_spec(wqb.shape), _const_spec(kvan.shape), _const_spec(w_k.shape),
                  _const_spec(w_vt.shape), _const_spec(gq.shape), _const_spec(gk.shape),
                  pl.BlockSpec((tm, 2 * QK_ROPE_DIM), lambda bi, i: (i, 0))],
        out_specs=[tok, tok, tok, tok,
                   head_spec(QK_HEAD_DIM), head_spec(QK_HEAD_DIM),
                   pl.BlockSpec((None, N_HEADS, V_HEAD_DIM, tm), lambda bi, i: (bi, 0, 0, i))],
        out_shape=[tok_shape, tok_shape, tok_shape, tok_shape,
                   jax.ShapeDtypeStruct((b, N_HEADS, s, QK_HEAD_DIM), BF16),
                   jax.ShapeDtypeStruct((b, N_HEADS, s, QK_HEAD_DIM), BF16),
                   jax.ShapeDtypeStruct((b, N_HEADS, V_HEAD_DIM, s), BF16)],
        compiler_params=pltpu.CompilerParams(dimension_semantics=("arbitrary", "arbitrary"),
                                             vmem_limit_bytes=VMEM_LIMIT_BYTES),
        name="in_proj",
    )(x, mod_x, nmix, w_all, w_gate, w_kv_in, bgate, qan, wqb, kvan, w_k, w_vt, gq, gk, table)


def _attention_kernel(bound_ref, q_ref, kc_ref, vct_ref, kl_ref, vlt_ref, o_ref,
                      lt_ref, acct_ref, m_ref, l_ref, acc_ref):
    tq = q_ref.shape[0]

    def scores_t(unit):
        k_ref, _, k0, tk = unit
        return lax.dot_general(k_ref[k0:k0 + tk, :], q_ref[...], CONTRACT_LAST,
                               preferred_element_type=F32)

    def finish_bounded(unit, st):
        _, vt_ref, k0, tk = unit
        pt = jnp.exp2(st)
        lt_ref[...] += pt.reshape(tk // SUBLANES, SUBLANES, tq).sum(axis=0)
        acct_ref[...] += jnp.dot(vt_ref[:, k0:k0 + tk], pt.astype(BF16),
                                 preferred_element_type=F32)

    def bounded_path():
        lt_ref[...] = jnp.zeros(lt_ref.shape, F32)
        acct_ref[...] = jnp.zeros(acct_ref.shape, F32)
        tk = ATTN_KV_TILE
        units = [(kc_ref, vct_ref, 0, kc_ref.shape[0])]
        units += [(kl_ref, vlt_ref, k0, tk) for k0 in range(0, kl_ref.shape[0], tk)]
        pending = [scores_t(u) for u in units[:ATTN_LOOKAHEAD]]
        for i, unit in enumerate(units):
            if i + ATTN_LOOKAHEAD < len(units):
                pending.append(scores_t(units[i + ATTN_LOOKAHEAD]))
            finish_bounded(unit, pending[i])
        l = jnp.sum(lt_ref[...], axis=0, keepdims=True)
        o_ref[...] = (acct_ref[...] / l).T.astype(o_ref.dtype)

    def scores(unit):
        k_ref, _, k0, tk, r0 = unit
        return lax.dot_general(q_ref[r0:r0 + ATTN_Q_SUB, :], k_ref[pl.ds(k0, tk), :],
                               CONTRACT_LAST, preferred_element_type=F32)

    def finish_online(unit, s):
        _, vt_ref, k0, tk, r0 = unit
        rows = slice(r0, r0 + ATTN_Q_SUB)
        cols = [s[:, c:c + LANES] for c in range(0, tk, LANES)]
        m_old = m_ref[rows, :]
        row_max = jnp.max(functools.reduce(jnp.maximum, cols), axis=-1, keepdims=True)
        m_new = jnp.maximum(m_old, row_max)
        alpha = jnp.exp2(m_old - m_new)
        ps = [jnp.exp2(col - m_new) for col in cols]
        l_ref[rows, :] = alpha * l_ref[rows, :] + functools.reduce(jnp.add, ps)
        p = jnp.concatenate([x.astype(BF16) for x in ps], axis=1)
        pv = lax.dot_general(p, vt_ref[:, pl.ds(k0, tk)], CONTRACT_LAST,
                             preferred_element_type=F32)
        acc_ref[rows, :] = alpha * acc_ref[rows, :] + pv
        m_ref[rows, :] = m_new

    def run_online(units):
        pending = [scores(u) for u in units[:ATTN_LOOKAHEAD]]
        for i, unit in enumerate(units):
            if i + ATTN_LOOKAHEAD < len(units):
                pending.append(scores(units[i + ATTN_LOOKAHEAD]))
            finish_online(unit, pending[i])

    def units_of(k_ref, vt_ref, starts, tk):
        return [(k_ref, vt_ref, k0, tk, r0) for k0 in starts for r0 in range(0, tq, ATTN_Q_SUB)]

    def online_path():
        m_ref[...] = jnp.full(m_ref.shape, -jnp.inf, F32)
        l_ref[...] = jnp.zeros(l_ref.shape, F32)
        acc_ref[...] = jnp.zeros(acc_ref.shape, F32)
        tk = ATTN_KV_TILE_ONLINE
        span = tk * ATTN_KV_UNROLL_ONLINE
        run_online(units_of(kc_ref, vct_ref, [0], kc_ref.shape[0]))

        def body(j, carry):
            base = pl.multiple_of(j * span, span)
            run_online(units_of(kl_ref, vlt_ref,
                                [base + u * tk for u in range(ATTN_KV_UNROLL_ONLINE)], tk))
            return carry

        lax.fori_loop(0, kl_ref.shape[0] // span, body, 0)
        l = jnp.sum(l_ref[...], axis=-1, keepdims=True)
        o_ref[...] = (acc_ref[...] / l).astype(o_ref.dtype)

    bounded = bound_ref[0] <= ATTN_EXP2_SAFE_RANGE
    pl.when(bounded)(bounded_path)
    pl.when(jnp.logical_not(bounded))(online_path)


def _attention(score_bound, q, k_ctx, vt_ctx, k_lat, vt_lat):
    b, nh, s, dk = q.shape
    nc = k_ctx.shape[2]
    dv = vt_lat.shape[2]
    assert dv == LANES
    tq = ATTN_Q_TILE

    def per_head(rows, cols):
        return pl.BlockSpec((None, None, rows, cols), lambda bi, hi, i: (bi, hi, 0, 0))

    return pl.pallas_call(
        _attention_kernel,
        grid=(b, nh, s // tq),
        in_specs=[pl.BlockSpec(memory_space=pltpu.SMEM),
                  pl.BlockSpec((None, None, tq, dk), lambda bi, hi, i: (bi, hi, i, 0)),
                  per_head(nc, dk), per_head(dv, nc), per_head(s, dk), per_head(dv, s)],
        out_specs=pl.BlockSpec((None, tq, dv), lambda bi, hi, i: (bi, i, hi)),
        out_shape=jax.ShapeDtypeStruct((b, s, nh * dv), BF16),
        scratch_shapes=[pltpu.VMEM((SUBLANES, tq), F32), pltpu.VMEM((dv, tq), F32),
                        pltpu.VMEM((tq, LANES), F32), pltpu.VMEM((tq, LANES), F32),
                        pltpu.VMEM((tq, dv), F32)],
        compiler_params=pltpu.CompilerParams(
            dimension_semantics=("arbitrary", "arbitrary", "arbitrary"),
            vmem_limit_bytes=VMEM_LIMIT_BYTES),
        name="attention",
    )(score_bound, q, k_ctx, vt_ctx, k_lat, vt_lat)


def _merge_kernel(x_ref, mod_ref, bx_ref, u_ref, u_prev_ref, u_next_ref, gc_ref, ga_ref,
                  attn_ref, convw_ref, convb_ref, w_conv_out_ref, w_attn_o_ref, w_out_ref,
                  o_ref):
    i = pl.program_id(1)
    tm, d = x_ref.shape
    y_attn = jnp.dot(attn_ref[...], w_attn_o_ref[...], preferred_element_type=F32)
    u = u_ref[...].astype(F32)
    prev_row = u_prev_ref[SUBLANES - 1:SUBLANES, :].astype(F32) * (i > 0).astype(F32)
    next_row = u_next_ref[0:1, :].astype(F32) * (i < pl.num_programs(1) - 1).astype(F32)
    row = lax.broadcasted_iota(jnp.int32, (tm, d), 0)
    u_m1 = jnp.where(row == 0, prev_row, pltpu.roll(u, 1, axis=0))
    u_p1 = jnp.where(row == tm - 1, next_row, pltpu.roll(u, tm - 1, axis=0))
    cw = convw_ref[...]
    conv = u_m1 * cw[0:1, :] + u * cw[1:2, :] + u_p1 * cw[2:3, :] + convb_ref[...]
    z = (bx_ref[...].astype(F32) * conv).astype(BF16)
    y_conv = jnp.dot(z, w_conv_out_ref[...], preferred_element_type=F32)
    merged = (gc_ref[...].astype(F32) * y_conv + ga_ref[...].astype(F32) * y_attn).astype(BF16)
    y = jnp.dot(merged, w_out_ref[...], preferred_element_type=F32)
    g1 = mod_ref[:, 2 * d:3 * d]
    o_ref[...] = x_ref[...] + g1 * y


def _merge(x, mod_x, bx, u, gc, ga, attn, conv_w, conv_b, w_conv_out, w_attn_o, w_out):
    b, s, d = x.shape
    tm = TOKEN_TILE
    hb = tm // SUBLANES
    last_hb = s // SUBLANES - 1
    tok = pl.BlockSpec((None, tm, d), lambda bi, i: (bi, i, 0))
    prev = pl.BlockSpec((None, SUBLANES, d), lambda bi, i: (bi, jnp.maximum(i * hb - 1, 0), 0))
    nxt = pl.BlockSpec((None, SUBLANES, d),
                       lambda bi, i: (bi, jnp.minimum((i + 1) * hb, last_hb), 0))
    return pl.pallas_call(
        _merge_kernel,
        grid=(b, s // tm),
        in_specs=[tok,
                  pl.BlockSpec((None, 1, mod_x.shape[-1]), lambda bi, i: (bi, 0, 0)),
                  tok, tok, prev, nxt, tok, tok, tok,
                  _const_spec(conv_w.shape), _const_spec(conv_b.shape),
                  _const_spec(w_conv_out.shape), _const_spec(w_attn_o.shape),
                  _const_spec(w_out.shape)],
        out_specs=tok,
        out_shape=jax.ShapeDtypeStruct((b, s, d), F32),
        compiler_params=pltpu.CompilerParams(dimension_semantics=("arbitrary", "arbitrary"),
                                             vmem_limit_bytes=VMEM_LIMIT_BYTES),
        name="merge",
    )(x, mod_x, bx, u, u, u, gc, ga, attn, conv_w, conv_b, w_conv_out, w_attn_o, w_out)


def _ffn_kernel(x_ref, mod_ref, nffn_ref, w_in_ref, w_out_ref, o_ref, act_ref):
    d = x_ref.shape[-1]
    d_ff = w_out_ref.shape[0]
    x = x_ref[...]
    mod = mod_ref[...]
    h = _modulated_norm(x, nffn_ref[...], mod[:, 3 * d:4 * d], mod[:, 4 * d:5 * d]).astype(BF16)
    for j in range(d_ff // FFN_CHUNK):
        c0 = j * FFN_CHUNK
        gate = jnp.dot(h, w_in_ref[:, c0:c0 + FFN_CHUNK], preferred_element_type=F32)
        up = jnp.dot(h, w_in_ref[:, d_ff + c0:d_ff + c0 + FFN_CHUNK],
                     preferred_element_type=F32)
        act_ref[:, c0:c0 + FFN_CHUNK] = (gate * _sigmoid(gate) * up).astype(BF16)
    y = jnp.dot(act_ref[...], w_out_ref[...], preferred_element_type=F32)
    o_ref[...] = x + mod[:, 5 * d:6 * d] * y


def _ffn(x, mod_x, nffn, w_ffn_in, w_ffn_out):
    b, s, d = x.shape
    tm = TOKEN_TILE
    d_ff = w_ffn_out.shape[0]
    tok = pl.BlockSpec((None, tm, d), lambda bi, i: (bi, i, 0))
    return pl.pallas_call(
        _ffn_kernel,
        grid=(b, s // tm),
        in_specs=[tok,
                  pl.BlockSpec((None, 1, mod_x.shape[-1]), lambda bi, i: (bi, 0, 0)),
                  _const_spec(nffn.shape), _const_spec(w_ffn_in.shape),
                  _const_spec(w_ffn_out.shape)],
        out_specs=tok,
        out_shape=jax.ShapeDtypeStruct((b, s, d), F32),
        scratch_shapes=[pltpu.VMEM((tm, d_ff), BF16)],
        compiler_params=pltpu.CompilerParams(dimension_semantics=("arbitrary", "arbitrary"),
                                             vmem_limit_bytes=VMEM_LIMIT_BYTES),
        name="ffn",
    )(x, mod_x, nffn, w_ffn_in, w_ffn_out)


def _rope_table(seq_len):
    quarter = QK_ROPE_DIM // 4
    freqs = ROPE_THETA ** (-np.arange(quarter, dtype=np.float64) / quarter)
    t = np.arange(seq_len)
    ang = [(t // GRID_W)[:, None] * freqs, (t % GRID_W)[:, None] * freqs]
    cos = [np.cos(a) for a in ang]
    sin = [np.sin(a) for a in ang]
    c = np.concatenate([cos[0], cos[0], cos[1], cos[1]], axis=-1)
    s = np.concatenate([-sin[0], sin[0], -sin[1], sin[1]], axis=-1)
    return jnp.asarray(np.concatenate([c, s], axis=-1), F32)


def _swap_halves(t):
    q = QK_ROPE_DIM // 4
    return jnp.concatenate([t[..., q:2 * q], t[..., :q], t[..., 3 * q:], t[..., 2 * q:3 * q]],
                           axis=-1)


def _norm_gain_pair(g):
    rope = g[QK_NOPE_DIM:]
    return jnp.concatenate([g[:QK_NOPE_DIM], rope, _swap_halves(rope)])[None, :].astype(F32)


def kernel(x, c, ctx, c_ctx, w_mod, b_mod, norm_mix, norm_ffn, w_in, b_gate, conv_w, conv_b,
           w_conv_out, q_a_norm, w_q_b, kv_a_norm, w_kv_b, q_norm, k_norm, w_attn_o, w_out,
           w_ffn_in, w_ffn_out):
    depth = w_mod.shape[0]
    assert depth == 1, "context stream update is only needed between layers"
    b, s, d = x.shape
    n_ctx = ctx.shape[1]
    assert s % TOKEN_TILE == 0 and s % ATTN_Q_TILE == 0 and s % ATTN_KV_TILE == 0
    assert s % (ATTN_KV_TILE_ONLINE * ATTN_KV_UNROLL_ONLINE) == 0
    assert b + 1 <= SUBLANES

    cond = jnp.zeros((SUBLANES, d), F32).at[:b].set(c).at[b].set(c_ctx)
    mod = _adaln(cond, w_mod, b_mod)
    mod_x = mod[:b, None, :]
    mod_ctx = mod[b:b + 1]

    wi = w_in[0]
    o_q = 3 * d
    o_kv = o_q + Q_LORA_RANK
    o_kr = o_kv + KV_LORA_RANK
    o_gc = o_kr + QK_ROPE_DIM
    assert o_q % LANES == 0 and o_kv % LANES == 0
    w_all = wi.astype(BF16)
    w_gate = w_all[:, o_gc:]
    w_kr = wi[:, o_kr:o_gc]
    w_kv_in = jnp.concatenate([wi[:, o_kv:o_kr], w_kr, _swap_halves(w_kr)], axis=1).astype(BF16)
    wq = w_q_b[0].reshape(Q_LORA_RANK, N_HEADS, QK_HEAD_DIM)
    wq = jnp.concatenate([wq, _swap_halves(wq[..., QK_NOPE_DIM:])], axis=-1)
    wqb = wq.reshape(Q_LORA_RANK, N_HEADS * 2 * LANES).astype(BF16)
    wkv = w_kv_b[0].reshape(KV_LORA_RANK, N_HEADS, QK_NOPE_DIM + V_HEAD_DIM)
    w_k = wkv[..., :QK_NOPE_DIM].reshape(KV_LORA_RANK, -1).astype(BF16)
    w_vt = wkv[..., QK_NOPE_DIM:].reshape(KV_LORA_RANK, -1).T.astype(BF16)
    gq = _norm_gain_pair(q_norm[0])
    gk = _norm_gain_pair(k_norm[0])
    table = _rope_table(s)
    table_ctx = jnp.concatenate([jnp.ones((n_ctx, QK_ROPE_DIM), F32),
                                 jnp.zeros((n_ctx, QK_ROPE_DIM), F32)], axis=1)
    nmix = norm_mix[0][None, :]
    kvan = kv_a_norm[0][None, :]

    k_ctx, vt_ctx = _ctx_kv(ctx, mod_ctx, nmix, w_kv_in, kvan, w_k, w_vt, gk, table_ctx)
    bx, u, gc, ga, q, k_lat, vt_lat = _in_proj(
        x, mod_x, nmix, w_all, w_gate, w_kv_in, b_gate[0][None, :], q_a_norm[0][None, :], wqb,
        kvan, w_k, w_vt, gq, gk, table)
    score_bound = (QK_HEAD_DIM * SM_SCALE_LOG2 * BF16_ROUNDING_MARGIN
                   * jnp.max(jnp.abs(q_norm[0])) * jnp.max(jnp.abs(k_norm[0]))).reshape(1)
    attn = _attention(score_bound.astype(F32), q, k_ctx, vt_ctx, k_lat, vt_lat)
    x_mid = _merge(x, mod_x, bx, u, gc, ga, attn, conv_w[0], conv_b[0][None, :],
                   w_conv_out[0].astype(BF16), w_attn_o[0].astype(BF16), w_out[0].astype(BF16))
    return _ffn(x_mid, mod_x, norm_ffn[0][None, :], w_ffn_in[0].astype(BF16),
                w_ffn_out[0].astype(BF16))
```

```python
import functools

import jax
import jax.numpy as jnp
import numpy as np
from jax import lax
from jax.experimental import pallas as pl
from jax.experimental.pallas import tpu as pltpu

F32 = jnp.float32
BF16 = jnp.bfloat16

N_HEADS = 8
QK_NOPE_DIM = 128
QK_ROPE_DIM = 64
QK_HEAD_DIM = QK_NOPE_DIM + QK_ROPE_DIM
V_HEAD_DIM = 128
Q_LORA_RANK = 384
KV_LORA_RANK = 256
GRID_W = 64
ROPE_THETA = 10000.0
NORM_EPS = 1e-6
MOD_CHUNKS = 6
SM_SCALE_LOG2 = float(QK_HEAD_DIM ** -0.5 * np.log2(np.e))
ATTN_EXP2_SAFE_RANGE = 40.0
BF16_ROUNDING_MARGIN = 1.01

LANES = 128
SUBLANES = 8
CONTRACT_LAST = (((1,), (1,)), ((), ()))
VMEM_LIMIT_BYTES = 56 * 1024 * 1024

TOKEN_TILE = 512
ATTN_Q_TILE = 1024
ATTN_Q_SUB = 256
ATTN_KV_TILE = 2048
ATTN_KV_TILE_ONLINE = 512
ATTN_KV_UNROLL_ONLINE = 2
ATTN_LOOKAHEAD = 2
ADALN_N_TILE = 1536
W_PREP_ROWS = 128
FFN_CHUNK = 256


def _rsqrt_mean(ss, n):
    return lax.rsqrt(ss * (1.0 / n) + NORM_EPS)


def _sigmoid(t):
    return 1.0 / (1.0 + jnp.exp(-t))


def _const_spec(shape):
    nd = len(shape)
    return pl.BlockSpec(shape, lambda *_: (0,) * nd, pipeline_mode=pl.Buffered(1))


def _adaln_kernel(cond_ref, w_ref, b_ref, o_ref):
    c = cond_ref[...]
    a = c * _sigmoid(c)
    o_ref[...] = jnp.dot(a, w_ref[...], preferred_element_type=F32,
                         precision=lax.Precision.HIGHEST) + b_ref[...]


def _adaln(cond, w_mod, b_mod):
    rows, d = cond.shape
    n = w_mod.shape[-1]
    return pl.pallas_call(
        _adaln_kernel,
        grid=(n // ADALN_N_TILE,),
        in_specs=[pl.BlockSpec((rows, d), lambda j: (0, 0)),
                  pl.BlockSpec((None, d, ADALN_N_TILE), lambda j: (0, 0, j)),
                  pl.BlockSpec((1, ADALN_N_TILE), lambda j: (0, j))],
        out_specs=pl.BlockSpec((rows, ADALN_N_TILE), lambda j: (0, j)),
        out_shape=jax.ShapeDtypeStruct((rows, n), F32),
        compiler_params=pltpu.CompilerParams(dimension_semantics=("arbitrary",),
                                             vmem_limit_bytes=VMEM_LIMIT_BYTES),
        name="adaln",
    )(cond, w_mod, b_mod)


def _w_in_prep_kernel(w_ref, w_all_ref, w_gate_ref):
    n_all = w_all_ref.shape[-1]
    n_gate = w_gate_ref.shape[-1]
    w = w_ref[...]
    w_all_ref[...] = w[:, :n_all].astype(BF16)
    w_gate_ref[...] = w[:, w.shape[-1] - n_gate:].astype(BF16)


def _w_in_prep(w_in, n_all, n_gate):
    _, d, n = w_in.shape
    rows = W_PREP_ROWS
    return pl.pallas_call(
        _w_in_prep_kernel,
        grid=(d // rows,),
        in_specs=[pl.BlockSpec((None, rows, n), lambda i: (0, i, 0))],
        out_specs=[pl.BlockSpec((rows, n_all), lambda i: (i, 0)),
                   pl.BlockSpec((rows, n_gate), lambda i: (i, 0))],
        out_shape=[jax.ShapeDtypeStruct((d, n_all), BF16),
                   jax.ShapeDtypeStruct((d, n_gate), BF16)],
        compiler_params=pltpu.CompilerParams(dimension_semantics=("arbitrary",),
                                             vmem_limit_bytes=VMEM_LIMIT_BYTES),
        name="w_in_prep",
    )(w_in)


def _modulated_norm(x, gain, shift, scale):
    d = x.shape[-1]
    r = _rsqrt_mean(jnp.sum(x * x, axis=-1, keepdims=True), d)
    return (x * r * gain) * (1.0 + scale) + shift


def _rope_pair(t2, table):
    w = t2 * table
    return w + pltpu.roll(w, QK_ROPE_DIM, axis=1)


def _kv_down(h, w_kv_in_ref):
    return jnp.dot(h, w_kv_in_ref[...], preferred_element_type=F32)


def _kv_up(kvp, kvan_ref, w_k_ref, w_vt_ref):
    kv_a = kvp[:, :KV_LORA_RANK]
    r = _rsqrt_mean(jnp.sum(kv_a * kv_a, axis=-1, keepdims=True), KV_LORA_RANK)
    kv_n = (kv_a * r * kvan_ref[...]).astype(BF16)
    k_nope = jnp.dot(kv_n, w_k_ref[...], preferred_element_type=F32)
    v_t = lax.dot_general(w_vt_ref[...], kv_n, CONTRACT_LAST, preferred_element_type=F32)
    return k_nope, v_t, kvp[:, KV_LORA_RANK:]


def _kv_finish(kv, v_t, kr2, gk_ref, table, k_ref, vt_ref):
    gk = gk_ref[...]
    ss_rope = 0.5 * jnp.sum(kr2 * kr2, axis=-1, keepdims=True)
    kr = _rope_pair(kr2 * gk[:, QK_NOPE_DIM:], table)[:, :QK_ROPE_DIM]
    for hd in range(N_HEADS):
        k_nope = kv[:, hd * QK_NOPE_DIM:(hd + 1) * QK_NOPE_DIM]
        ss = jnp.sum(k_nope * k_nope, axis=-1, keepdims=True) + ss_rope
        rh = _rsqrt_mean(ss, QK_HEAD_DIM)
        k_ref[hd, :, :QK_NOPE_DIM] = (k_nope * rh * gk[:, :QK_NOPE_DIM]).astype(BF16)
        k_ref[hd, :, QK_NOPE_DIM:] = (kr * rh).astype(BF16)
        vt_ref[hd] = v_t[hd * V_HEAD_DIM:(hd + 1) * V_HEAD_DIM, :].astype(BF16)


def _ctx_kv_kernel(ctx_ref, mod_ref, nmix_ref, w_kv_in_ref, kvan_ref, w_k_ref, w_vt_ref, gk_ref,
                   table_ref, k_ref, vt_ref):
    d = ctx_ref.shape[-1]
    mod = mod_ref[...]
    h = _modulated_norm(ctx_ref[...], nmix_ref[...], mod[:, :d], mod[:, d:2 * d]).astype(BF16)
    k_nope, v_t, kr2 = _kv_up(_kv_down(h, w_kv_in_ref), kvan_ref, w_k_ref, w_vt_ref)
    _kv_finish(k_nope, v_t, kr2, gk_ref, table_ref[...], k_ref, vt_ref)


def _ctx_kv(ctx, mod_ctx, nmix, w_kv_in, kvan, w_k, w_vt, gk, table):
    b, n, d = ctx.shape
    return pl.pallas_call(
        _ctx_kv_kernel,
        grid=(b,),
        in_specs=[pl.BlockSpec((None, n, d), lambda i: (i, 0, 0)),
                  _const_spec(mod_ctx.shape), _const_spec(nmix.shape),
                  _const_spec(w_kv_in.shape), _const_spec(kvan.shape),
                  _const_spec(w_k.shape), _const_spec(w_vt.shape), _const_spec(gk.shape),
                  _const_spec(table.shape)],
        out_specs=[pl.BlockSpec((None, N_HEADS, n, QK_HEAD_DIM), lambda i: (i, 0, 0, 0)),
                   pl.BlockSpec((None, N_HEADS, V_HEAD_DIM, n), lambda i: (i, 0, 0, 0))],
        out_shape=[jax.ShapeDtypeStruct((b, N_HEADS, n, QK_HEAD_DIM), BF16),
                   jax.ShapeDtypeStruct((b, N_HEADS, V_HEAD_DIM, n), BF16)],
        compiler_params=pltpu.CompilerParams(dimension_semantics=("arbitrary",),
                                             vmem_limit_bytes=VMEM_LIMIT_BYTES),
        name="ctx_kv",
    )(ctx, mod_ctx, nmix, w_kv_in, kvan, w_k, w_vt, gk, table)


def _in_proj_kernel(x_ref, mod_ref, nmix_ref, w_all_ref, w_gate_ref, w_kv_in_ref, bgate_ref,
                    qan_ref, wqb_ref, kvan_ref, w_k_ref, w_vt_ref, gq_ref, gk_ref, table_ref,
                    bx_ref, u_ref, gc_ref, ga_ref, q_ref, k_ref, vt_ref):
    d = x_ref.shape[-1]
    mod = mod_ref[...]
    h = _modulated_norm(x_ref[...], nmix_ref[...], mod[:, :d], mod[:, d:2 * d]).astype(BF16)
    table = table_ref[...]

    def proj(w_ref, j):
        return jnp.dot(h, w_ref[:, j * d:(j + 1) * d], preferred_element_type=F32)

    q_a = jnp.dot(h, w_all_ref[:, 3 * d:3 * d + Q_LORA_RANK], preferred_element_type=F32)
    kvp = _kv_down(h, w_kv_in_ref)
    bgate = bgate_ref[...]
    gc_ref[...] = _sigmoid(proj(w_gate_ref, 0) + bgate[:, :d]).astype(BF16)
    r = _rsqrt_mean(jnp.sum(q_a * q_a, axis=-1, keepdims=True), Q_LORA_RANK)
    q_n = (q_a * r * qan_ref[...]).astype(BF16)
    q = jnp.dot(q_n, wqb_ref[...], preferred_element_type=F32)
    k_nope, v_t, kr2 = _kv_up(kvp, kvan_ref, w_k_ref, w_vt_ref)
    ga_ref[...] = _sigmoid(proj(w_gate_ref, 1) + bgate[:, d:]).astype(BF16)

    gq = gq_ref[...]
    table_q = table * gq[:, QK_NOPE_DIM:]
    sm_scale = SM_SCALE_LOG2
    hw = 2 * LANES
    for hd in range(N_HEADS):
        q_nope = q[:, hd * hw:hd * hw + QK_NOPE_DIM]
        q_r2 = q[:, hd * hw + QK_NOPE_DIM:(hd + 1) * hw]
        ss = jnp.sum(q_nope * q_nope + 0.5 * (q_r2 * q_r2), axis=-1, keepdims=True)
        rh = _rsqrt_mean(ss, QK_HEAD_DIM) * sm_scale
        q_ref[hd, :, :QK_NOPE_DIM] = (q_nope * rh * gq[:, :QK_NOPE_DIM]).astype(BF16)
        q_ref[hd, :, QK_NOPE_DIM:] = (_rope_pair(q_r2 * rh, table_q)[:, :QK_ROPE_DIM]).astype(BF16)

    u_ref[...] = (proj(w_all_ref, 1) * proj(w_all_ref, 2)).astype(BF16)
    _kv_finish(k_nope, v_t, kr2, gk_ref, table, k_ref, vt_ref)
    bx_ref[...] = proj(w_all_ref, 0).astype(BF16)


def _in_proj(x, mod_x, nmix, w_all, w_gate, w_kv_in, bgate, qan, wqb, kvan, w_k, w_vt, gq, gk,
             table):
    b, s, d = x.shape
    tm = TOKEN_TILE
    tok = pl.BlockSpec((None, tm, d), lambda bi, i: (bi, i, 0))

    def head_spec(width):
        return pl.BlockSpec((None, N_HEADS, tm, width), lambda bi, i: (bi, 0, i, 0))

    tok_shape = jax.ShapeDtypeStruct((b, s, d), BF16)
    return pl.pallas_call(
        _in_proj_kernel,
        grid=(b, s // tm),
        in_specs=[tok,
                  pl.BlockSpec((None, 1, mod_x.shape[-1]), lambda bi, i: (bi, 0, 0)),
                  _const_spec(nmix.shape),
                  _const_spec(w_all.shape), _const_spec(w_gate.shape),
                  _const_spec(w_kv_in.shape), _const_spec(bgate.shape), _const_spec(qan.shape),
                  _const_spec(wqb.shape), _const_spec(kvan.shape), _const_spec(w_k.shape),
                  _const_spec(w_vt.shape), _const_spec(gq.shape), _const_spec(gk.shape),
                  pl.BlockSpec((tm, 2 * QK_ROPE_DIM), lambda bi, i: (i, 0))],
        out_specs=[tok, tok, tok, tok,
                   head_spec(QK_HEAD_DIM), head_spec(QK_HEAD_DIM),
                   pl.BlockSpec((None, N_HEADS, V_HEAD_DIM, tm), lambda bi, i: (bi, 0, 0, i))],
        out_shape=[tok_shape, tok_shape, tok_shape, tok_shape,
                   jax.ShapeDtypeStruct((b, N_HEADS, s, QK_HEAD_DIM), BF16),
                   jax.ShapeDtypeStruct((b, N_HEADS, s, QK_HEAD_DIM), BF16),
                   jax.ShapeDtypeStruct((b, N_HEADS, V_HEAD_DIM, s), BF16)],
        compiler_params=pltpu.CompilerParams(dimension_semantics=("arbitrary", "arbitrary"),
                                             vmem_limit_bytes=VMEM_LIMIT_BYTES),
        name="in_proj",
    )(x, mod_x, nmix, w_all, w_gate, w_kv_in, bgate, qan, wqb, kvan, w_k, w_vt, gq, gk, table)


def _attention_kernel(bound_ref, q_ref, kc_ref, vct_ref, kl_ref, vlt_ref, o_ref,
                      lt_ref, acct_ref, m_ref, l_ref, acc_ref):
    tq = q_ref.shape[0]

    def scores_t(unit):
        k_ref, _, k0, tk = unit
        return lax.dot_general(k_ref[k0:k0 + tk, :], q_ref[...], CONTRACT_LAST,
                               preferred_element_type=F32)

    def finish_bounded(unit, st):
        _, vt_ref, k0, tk = unit
        pt = jnp.exp2(st)
        lt_ref[...] += pt.reshape(tk // SUBLANES, SUBLANES, tq).sum(axis=0)
        acct_ref[...] += jnp.dot(vt_ref[:, k0:k0 + tk], pt.astype(BF16),
                                 preferred_element_type=F32)

    def bounded_path():
        lt_ref[...] = jnp.zeros(lt_ref.shape, F32)
        acct_ref[...] = jnp.zeros(acct_ref.shape, F32)
        tk = ATTN_KV_TILE
        units = [(kc_ref, vct_ref, 0, kc_ref.shape[0])]
        units += [(kl_ref, vlt_ref, k0, tk) for k0 in range(0, kl_ref.shape[0], tk)]
        pending = [scores_t(u) for u in units[:ATTN_LOOKAHEAD]]
        for i, unit in enumerate(units):
            if i + ATTN_LOOKAHEAD < len(units):
                pending.append(scores_t(units[i + ATTN_LOOKAHEAD]))
            finish_bounded(unit, pending[i])
        l = jnp.sum(lt_ref[...], axis=0, keepdims=True)
        o_ref[...] = (acct_ref[...] / l).T.astype(o_ref.dtype)

    def scores(unit):
        k_ref, _, k0, tk, r0 = unit
        return lax.dot_general(q_ref[r0:r0 + ATTN_Q_SUB, :], k_ref[pl.ds(k0, tk), :],
                               CONTRACT_LAST, preferred_element_type=F32)

    def finish_online(unit, s):
        _, vt_ref, k0, tk, r0 = unit
        rows = slice(r0, r0 + ATTN_Q_SUB)
        cols = [s[:, c:c + LANES] for c in range(0, tk, LANES)]
        m_old = m_ref[rows, :]
        row_max = jnp.max(functools.reduce(jnp.maximum, cols), axis=-1, keepdims=True)
        m_new = jnp.maximum(m_old, row_max)
        alpha = jnp.exp2(m_old - m_new)
        ps = [jnp.exp2(col - m_new) for col in cols]
        l_ref[rows, :] = alpha * l_ref[rows, :] + functools.reduce(jnp.add, ps)
        p = jnp.concatenate([x.astype(BF16) for x in ps], axis=1)
        pv = lax.dot_general(p, vt_ref[:, pl.ds(k0, tk)], CONTRACT_LAST,
                             preferred_element_type=F32)
        acc_ref[rows, :] = alpha * acc_ref[rows, :] + pv
        m_ref[rows, :] = m_new

    def run_online(units):
        pending = [scores(u) for u in units[:ATTN_LOOKAHEAD]]
        for i, unit in enumerate(units):
            if i + ATTN_LOOKAHEAD < len(units):
                pending.append(scores(units[i + ATTN_LOOKAHEAD]))
            finish_online(unit, pending[i])

    def units_of(k_ref, vt_ref, starts, tk):
        return [(k_ref, vt_ref, k0, tk, r0) for k0 in starts for r0 in range(0, tq, ATTN_Q_SUB)]

    def online_path():
        m_ref[...] = jnp.full(m_ref.shape, -jnp.inf, F32)
        l_ref[...] = jnp.zeros(l_ref.shape, F32)
        acc_ref[...] = jnp.zeros(acc_ref.shape, F32)
        tk = ATTN_KV_TILE_ONLINE
        span = tk * ATTN_KV_UNROLL_ONLINE
        run_online(units_of(kc_ref, vct_ref, [0], kc_ref.shape[0]))

        def body(j, carry):
            base = pl.multiple_of(j * span, span)
            run_online(units_of(kl_ref, vlt_ref,
                                [base + u * tk for u in range(ATTN_KV_UNROLL_ONLINE)], tk))
            return carry

        lax.fori_loop(0, kl_ref.shape[0] // span, body, 0)
        l = jnp.sum(l_ref[...], axis=-1, keepdims=True)
        o_ref[...] = (acc_ref[...] / l).astype(o_ref.dtype)

    bounded = bound_ref[0] <= ATTN_EXP2_SAFE_RANGE
    pl.when(bounded)(bounded_path)
    pl.when(jnp.logical_not(bounded))(online_path)


def _attention(score_bound, q, k_ctx, vt_ctx, k_lat, vt_lat):
    b, nh, s, dk = q.shape
    nc = k_ctx.shape[2]
    dv = vt_lat.shape[2]
    assert dv == LANES
    tq = ATTN_Q_TILE

    def per_head(rows, cols):
        return pl.BlockSpec((None, None, rows, cols), lambda bi, hi, i: (bi, hi, 0, 0))

    return pl.pallas_call(
        _attention_kernel,
        grid=(b, nh, s // tq),
        in_specs=[pl.BlockSpec(memory_space=pltpu.SMEM),
                  pl.BlockSpec((None, None, tq, dk), lambda bi, hi, i: (bi, hi, i, 0)),
                  per_head(nc, dk), per_head(dv, nc), per_head(s, dk), per_head(dv, s)],
        out_specs=pl.BlockSpec((None, tq, dv), lambda bi, hi, i: (bi, i, hi)),
        out_shape=jax.ShapeDtypeStruct((b, s, nh * dv), BF16),
        scratch_shapes=[pltpu.VMEM((SUBLANES, tq), F32), pltpu.VMEM((dv, tq), F32),
                        pltpu.VMEM((tq, LANES), F32), pltpu.VMEM((tq, LANES), F32),
                        pltpu.VMEM((tq, dv), F32)],
        compiler_params=pltpu.CompilerParams(
            dimension_semantics=("arbitrary", "arbitrary", "arbitrary"),
            vmem_limit_bytes=VMEM_LIMIT_BYTES),
        name="attention",
    )(score_bound, q, k_ctx, vt_ctx, k_lat, vt_lat)


def _merge_kernel(x_ref, mod_ref, bx_ref, u_ref, u_prev_ref, u_next_ref, gc_ref, ga_ref,
                  attn_ref, convw_ref, convb_ref, w_conv_out_ref, w_attn_o_ref, w_out_ref,
                  o_ref):
    i = pl.program_id(1)
    tm, d = x_ref.shape
    y_attn = jnp.dot(attn_ref[...], w_attn_o_ref[...], preferred_element_type=F32)
    u = u_ref[...].astype(F32)
    prev_row = u_prev_ref[SUBLANES - 1:SUBLANES, :].astype(F32) * (i > 0).astype(F32)
    next_row = u_next_ref[0:1, :].astype(F32) * (i < pl.num_programs(1) - 1).astype(F32)
    row = lax.broadcasted_iota(jnp.int32, (tm, d), 0)
    u_m1 = jnp.where(row == 0, prev_row, pltpu.roll(u, 1, axis=0))
    u_p1 = jnp.where(row == tm - 1, next_row, pltpu.roll(u, tm - 1, axis=0))
    cw = convw_ref[...]
    conv = u_m1 * cw[0:1, :] + u * cw[1:2, :] + u_p1 * cw[2:3, :] + convb_ref[...]
    z = (bx_ref[...].astype(F32) * conv).astype(BF16)
    y_conv = jnp.dot(z, w_conv_out_ref[...], preferred_element_type=F32)
    merged = (gc_ref[...].astype(F32) * y_conv + ga_ref[...].astype(F32) * y_attn).astype(BF16)
    y = jnp.dot(merged, w_out_ref[...], preferred_element_type=F32)
    g1 = mod_ref[:, 2 * d:3 * d]
    o_ref[...] = x_ref[...] + g1 * y


def _merge(x, mod_x, bx, u, gc, ga, attn, conv_w, conv_b, w_conv_out, w_attn_o, w_out):
    b, s, d = x.shape
    tm = TOKEN_TILE
    hb = tm // SUBLANES
    last_hb = s // SUBLANES - 1
    tok = pl.BlockSpec((None, tm, d), lambda bi, i: (bi, i, 0))
    prev = pl.BlockSpec((None, SUBLANES, d), lambda bi, i: (bi, jnp.maximum(i * hb - 1, 0), 0))
    nxt = pl.BlockSpec((None, SUBLANES, d),
                       lambda bi, i: (bi, jnp.minimum((i + 1) * hb, last_hb), 0))
    return pl.pallas_call(
        _merge_kernel,
        grid=(b, s // tm),
        in_specs=[tok,
                  pl.BlockSpec((None, 1, mod_x.shape[-1]), lambda bi, i: (bi, 0, 0)),
                  tok, tok, prev, nxt, tok, tok, tok,
                  _const_spec(conv_w.shape), _const_spec(conv_b.shape),
                  _const_spec(w_conv_out.shape), _const_spec(w_attn_o.shape),
                  _const_spec(w_out.shape)],
        out_specs=tok,
        out_shape=jax.ShapeDtypeStruct((b, s, d), F32),
        compiler_params=pltpu.CompilerParams(dimension_semantics=("arbitrary", "arbitrary"),
                                             vmem_limit_bytes=VMEM_LIMIT_BYTES),
        name="merge",
    )(x, mod_x, bx, u, u, u, gc, ga, attn, conv_w, conv_b, w_conv_out, w_attn_o, w_out)


def _ffn_kernel(x_ref, mod_ref, nffn_ref, w_in_ref, w_out_ref, o_ref, act_ref):
    d = x_ref.shape[-1]
    d_ff = w_out_ref.shape[0]
    x = x_ref[...]
    mod = mod_ref[...]
    h = _modulated_norm(x, nffn_ref[...], mod[:, 3 * d:4 * d], mod[:, 4 * d:5 * d]).astype(BF16)
    for j in range(d_ff // FFN_CHUNK):
        c0 = j * FFN_CHUNK
        gate = jnp.dot(h, w_in_ref[:, c0:c0 + FFN_CHUNK], preferred_element_type=F32)
        up = jnp.dot(h, w_in_ref[:, d_ff + c0:d_ff + c0 + FFN_CHUNK],
                     preferred_element_type=F32)
        act_ref[:, c0:c0 + FFN_CHUNK] = (gate * _sigmoid(gate) * up).astype(BF16)
    y = jnp.dot(act_ref[...], w_out_ref[...], preferred_element_type=F32)
    o_ref[...] = x + mod[:, 5 * d:6 * d] * y


def _ffn(x, mod_x, nffn, w_ffn_in, w_ffn_out):
    b, s, d = x.shape
    tm = TOKEN_TILE
    d_ff = w_ffn_out.shape[0]
    tok = pl.BlockSpec((None, tm, d), lambda bi, i: (bi, i, 0))
    return pl.pallas_call(
        _ffn_kernel,
        grid=(b, s // tm),
        in_specs=[tok,
                  pl.BlockSpec((None, 1, mod_x.shape[-1]), lambda bi, i: (bi, 0, 0)),
                  _const_spec(nffn.shape), _const_spec(w_ffn_in.shape),
                  _const_spec(w_ffn_out.shape)],
        out_specs=tok,
        out_shape=jax.ShapeDtypeStruct((b, s, d), F32),
        scratch_shapes=[pltpu.VMEM((tm, d_ff), BF16)],
        compiler_params=pltpu.CompilerParams(dimension_semantics=("arbitrary", "arbitrary"),
                                             vmem_limit_bytes=VMEM_LIMIT_BYTES),
        name="ffn",
    )(x, mod_x, nffn, w_ffn_in, w_ffn_out)


def _rope_table(seq_len):
    quarter = QK_ROPE_DIM // 4
    freqs = ROPE_THETA ** (-np.arange(quarter, dtype=np.float64) / quarter)
    t = np.arange(seq_len)
    ang = [(t // GRID_W)[:, None] * freqs, (t % GRID_W)[:, None] * freqs]
    cos = [np.cos(a) for a in ang]
    sin = [np.sin(a) for a in ang]
    c = np.concatenate([cos[0], cos[0], cos[1], cos[1]], axis=-1)
    s = np.concatenate([-sin[0], sin[0], -sin[1], sin[1]], axis=-1)
    return jnp.asarray(np.concatenate([c, s], axis=-1), F32)


def _swap_halves(t):
    q = QK_ROPE_DIM // 4
    return jnp.concatenate([t[..., q:2 * q], t[..., :q], t[..., 3 * q:], t[..., 2 * q:3 * q]],
                           axis=-1)


def _norm_gain_pair(g):
    rope = g[QK_NOPE_DIM:]
    return jnp.concatenate([g[:QK_NOPE_DIM], rope, _swap_halves(rope)])[None, :].astype(F32)


def kernel(x, c, ctx, c_ctx, w_mod, b_mod, norm_mix, norm_ffn, w_in, b_gate, conv_w, conv_b,
           w_conv_out, q_a_norm, w_q_b, kv_a_norm, w_kv_b, q_norm, k_norm, w_attn_o, w_out,
           w_ffn_in, w_ffn_out):
    depth = w_mod.shape[0]
    assert depth == 1, "context stream update is only needed between layers"
    b, s, d = x.shape
    n_ctx = ctx.shape[1]
    assert s % TOKEN_TILE == 0 and s % ATTN_Q_TILE == 0 and s % ATTN_KV_TILE == 0
    assert s % (ATTN_KV_TILE_ONLINE * ATTN_KV_UNROLL_ONLINE) == 0
    assert b + 1 <= SUBLANES

    cond = jnp.zeros((SUBLANES, d), F32).at[:b].set(c).at[b].set(c_ctx)
    mod = _adaln(cond, w_mod, b_mod)
    mod_x = mod[:b, None, :]
    mod_ctx = mod[b:b + 1]

    wi = w_in[0]
    o_q = 3 * d
    o_kv = o_q + Q_LORA_RANK
    o_kr = o_kv + KV_LORA_RANK
    o_gc = o_kr + QK_ROPE_DIM
    assert o_q % LANES == 0 and o_kv % LANES == 0
    w_all, w_gate = _w_in_prep(w_in, o_kv, 2 * d)
    w_kr = wi[:, o_kr:o_gc]
    w_kv_in = jnp.concatenate([wi[:, o_kv:o_kr], w_kr, _swap_halves(w_kr)], axis=1).astype(BF16)
    wq = w_q_b[0].reshape(Q_LORA_RANK, N_HEADS, QK_HEAD_DIM)
    wq = jnp.concatenate([wq, _swap_halves(wq[..., QK_NOPE_DIM:])], axis=-1)
    wqb = wq.reshape(Q_LORA_RANK, N_HEADS * 2 * LANES).astype(BF16)
    wkv = w_kv_b[0].reshape(KV_LORA_RANK, N_HEADS, QK_NOPE_DIM + V_HEAD_DIM)
    w_k = wkv[..., :QK_NOPE_DIM].reshape(KV_LORA_RANK, -1).astype(BF16)
    w_vt = wkv[..., QK_NOPE_DIM:].reshape(KV_LORA_RANK, -1).T.astype(BF16)
    gq = _norm_gain_pair(q_norm[0])
    gk = _norm_gain_pair(k_norm[0])
    table = _rope_table(s)
    table_ctx = jnp.concatenate([jnp.ones((n_ctx, QK_ROPE_DIM), F32),
                                 jnp.zeros((n_ctx, QK_ROPE_DIM), F32)], axis=1)
    nmix = norm_mix[0][None, :]
    kvan = kv_a_norm[0][None, :]

    k_ctx, vt_ctx = _ctx_kv(ctx, mod_ctx, nmix, w_kv_in, kvan, w_k, w_vt, gk, table_ctx)
    bx, u, gc, ga, q, k_lat, vt_lat = _in_proj(
        x, mod_x, nmix, w_all, w_gate, w_kv_in, b_gate[0][None, :], q_a_norm[0][None, :], wqb,
        kvan, w_k, w_vt, gq, gk, table)
    score_bound = (QK_HEAD_DIM * SM_SCALE_LOG2 * BF16_ROUNDING_MARGIN
                   * jnp.max(jnp.abs(q_norm[0])) * jnp.max(jnp.abs(k_norm[0]))).reshape(1)
    attn = _attention(score_bound.astype(F32), q, k_ctx, vt_ctx, k_lat, vt_lat)
    x_mid = _merge(x, mod_x, bx, u, gc, ga, attn, conv_w[0], conv_b[0][None, :],
                   w_conv_out[0].astype(BF16), w_attn_o[0].astype(BF16), w_out[0].astype(BF16))
    return _ffn(x_mid, mod_x, norm_ffn[0][None, :], w_ffn_in[0].astype(BF16),
                w_ffn_out[0].astype(BF16))
```

```python
import functools

import jax
import jax.numpy as jnp
import numpy as np
from jax import lax
from jax.experimental import pallas as pl
from jax.experimental.pallas import tpu as pltpu

F32 = jnp.float32
BF16 = jnp.bfloat16

N_HEADS = 8
QK_NOPE_DIM = 128
QK_ROPE_DIM = 64
QK_HEAD_DIM = QK_NOPE_DIM + QK_ROPE_DIM
V_HEAD_DIM = 128
Q_LORA_RANK = 384
KV_LORA_RANK = 256
GRID_W = 64
ROPE_THETA = 10000.0
NORM_EPS = 1e-6
MOD_CHUNKS = 6
SM_SCALE_LOG2 = float(QK_HEAD_DIM ** -0.5 * np.log2(np.e))
ATTN_EXP2_SAFE_RANGE = 40.0
BF16_ROUNDING_MARGIN = 1.01

LANES = 128
SUBLANES = 8
CONTRACT_LAST = (((1,), (1,)), ((), ()))
VMEM_LIMIT_BYTES = 56 * 1024 * 1024

TOKEN_TILE = 512
ATTN_Q_TILE = 2048
ATTN_Q_SUB = 256
ATTN_KV_TILE = 1024
ATTN_KV_TILE_ONLINE = 512
ATTN_KV_UNROLL_ONLINE = 2
ATTN_LOOKAHEAD = 2
ADALN_N_TILE = 1536
FFN_CHUNK = 256


def _rsqrt_mean(ss, n):
    return lax.rsqrt(ss * (1.0 / n) + NORM_EPS)


def _sigmoid(t):
    return 1.0 / (1.0 + jnp.exp(-t))


def _const_spec(shape):
    nd = len(shape)
    return pl.BlockSpec(shape, lambda *_: (0,) * nd, pipeline_mode=pl.Buffered(1))


def _adaln_kernel(cond_ref, w_ref, b_ref, o_ref):
    c = cond_ref[...]
    a = c * _sigmoid(c)
    o_ref[...] = jnp.dot(a, w_ref[...], preferred_element_type=F32,
                         precision=lax.Precision.HIGHEST) + b_ref[...]


def _adaln(cond, w_mod, b_mod):
    rows, d = cond.shape
    n = w_mod.shape[-1]
    return pl.pallas_call(
        _adaln_kernel,
        grid=(n // ADALN_N_TILE,),
        in_specs=[pl.BlockSpec((rows, d), lambda j: (0, 0)),
                  pl.BlockSpec((None, d, ADALN_N_TILE), lambda j: (0, 0, j)),
                  pl.BlockSpec((1, ADALN_N_TILE), lambda j: (0, j))],
        out_specs=pl.BlockSpec((rows, ADALN_N_TILE), lambda j: (0, j)),
        out_shape=jax.ShapeDtypeStruct((rows, n), F32),
        compiler_params=pltpu.CompilerParams(dimension_semantics=("arbitrary",),
                                             vmem_limit_bytes=VMEM_LIMIT_BYTES),
        name="adaln",
    )(cond, w_mod, b_mod)


def _modulated_norm(x, gain, shift, scale):
    d = x.shape[-1]
    r = _rsqrt_mean(jnp.sum(x * x, axis=-1, keepdims=True), d)
    return (x * r * gain) * (1.0 + scale) + shift


def _rope_pair(t2, table):
    w = t2 * table
    return w + pltpu.roll(w, QK_ROPE_DIM, axis=1)


def _kv_down(h, w_kv_in_ref):
    return jnp.dot(h, w_kv_in_ref[...], preferred_element_type=F32)


def _kv_up(kvp, kvan_ref, w_k_ref, w_vt_ref):
    kv_a = kvp[:, :KV_LORA_RANK]
    r = _rsqrt_mean(jnp.sum(kv_a * kv_a, axis=-1, keepdims=True), KV_LORA_RANK)
    kv_n = (kv_a * r * kvan_ref[...]).astype(BF16)
    k_nope = jnp.dot(kv_n, w_k_ref[...], preferred_element_type=F32)
    v_t = lax.dot_general(w_vt_ref[...], kv_n, CONTRACT_LAST, preferred_element_type=F32)
    return k_nope, v_t, kvp[:, KV_LORA_RANK:]


def _kv_finish(kv, v_t, kr2, gk_ref, table, k_ref, vt_ref):
    gk = gk_ref[...]
    ss_rope = 0.5 * jnp.sum(kr2 * kr2, axis=-1, keepdims=True)
    kr = _rope_pair(kr2 * gk[:, QK_NOPE_DIM:], table)[:, :QK_ROPE_DIM]
    for hd in range(N_HEADS):
        k_nope = kv[:, hd * QK_NOPE_DIM:(hd + 1) * QK_NOPE_DIM]
        ss = jnp.sum(k_nope * k_nope, axis=-1, keepdims=True) + ss_rope
        rh = _rsqrt_mean(ss, QK_HEAD_DIM)
        k_ref[hd, :, :QK_NOPE_DIM] = (k_nope * rh * gk[:, :QK_NOPE_DIM]).astype(BF16)
        k_ref[hd, :, QK_NOPE_DIM:] = (kr * rh).astype(BF16)
        vt_ref[hd] = v_t[hd * V_HEAD_DIM:(hd + 1) * V_HEAD_DIM, :].astype(BF16)


def _ctx_kv_kernel(ctx_ref, mod_ref, nmix_ref, w_kv_in_ref, kvan_ref, w_k_ref, w_vt_ref, gk_ref,
                   table_ref, k_ref, vt_ref):
    d = ctx_ref.shape[-1]
    mod = mod_ref[...]
    h = _modulated_norm(ctx_ref[...], nmix_ref[...], mod[:, :d], mod[:, d:2 * d]).astype(BF16)
    k_nope, v_t, kr2 = _kv_up(_kv_down(h, w_kv_in_ref), kvan_ref, w_k_ref, w_vt_ref)
    _kv_finish(k_nope, v_t, kr2, gk_ref, table_ref[...], k_ref, vt_ref)


def _ctx_kv(ctx, mod_ctx, nmix, w_kv_in, kvan, w_k, w_vt, gk, table):
    b, n, d = ctx.shape
    return pl.pallas_call(
        _ctx_kv_kernel,
        grid=(b,),
        in_specs=[pl.BlockSpec((None, n, d), lambda i: (i, 0, 0)),
                  _const_spec(mod_ctx.shape), _const_spec(nmix.shape),
                  _const_spec(w_kv_in.shape), _const_spec(kvan.shape),
                  _const_spec(w_k.shape), _const_spec(w_vt.shape), _const_spec(gk.shape),
                  _const_spec(table.shape)],
        out_specs=[pl.BlockSpec((None, N_HEADS, n, QK_HEAD_DIM), lambda i: (i, 0, 0, 0)),
                   pl.BlockSpec((None, N_HEADS, V_HEAD_DIM, n), lambda i: (i, 0, 0, 0))],
        out_shape=[jax.ShapeDtypeStruct((b, N_HEADS, n, QK_HEAD_DIM), BF16),
                   jax.ShapeDtypeStruct((b, N_HEADS, V_HEAD_DIM, n), BF16)],
        compiler_params=pltpu.CompilerParams(dimension_semantics=("arbitrary",),
                                             vmem_limit_bytes=VMEM_LIMIT_BYTES),
        name="ctx_kv",
    )(ctx, mod_ctx, nmix, w_kv_in, kvan, w_k, w_vt, gk, table)


def _in_proj_kernel(x_ref, mod_ref, nmix_ref, w_all_ref, w_gate_ref, w_kv_in_ref, bgate_ref,
                    qan_ref, wqb_ref, kvan_ref, w_k_ref, w_vt_ref, gq_ref, gk_ref, table_ref,
                    bx_ref, u_ref, gc_ref, ga_ref, q_ref, k_ref, vt_ref):
    d = x_ref.shape[-1]
    mod = mod_ref[...]
    h = _modulated_norm(x_ref[...], nmix_ref[...], mod[:, :d], mod[:, d:2 * d]).astype(BF16)
    table = table_ref[...]

    def proj(w_ref, j):
        return jnp.dot(h, w_ref[:, j * d:(j + 1) * d], preferred_element_type=F32)

    q_a = jnp.dot(h, w_all_ref[:, 3 * d:3 * d + Q_LORA_RANK], preferred_element_type=F32)
    kvp = _kv_down(h, w_kv_in_ref)
    bgate = bgate_ref[...]
    gc_ref[...] = _sigmoid(proj(w_gate_ref, 0) + bgate[:, :d]).astype(BF16)
    r = _rsqrt_mean(jnp.sum(q_a * q_a, axis=-1, keepdims=True), Q_LORA_RANK)
    q_n = (q_a * r * qan_ref[...]).astype(BF16)
    q = jnp.dot(q_n, wqb_ref[...], preferred_element_type=F32)
    k_nope, v_t, kr2 = _kv_up(kvp, kvan_ref, w_k_ref, w_vt_ref)
    ga_ref[...] = _sigmoid(proj(w_gate_ref, 1) + bgate[:, d:]).astype(BF16)

    gq = gq_ref[...]
    table_q = table * gq[:, QK_NOPE_DIM:]
    sm_scale = SM_SCALE_LOG2
    hw = 2 * LANES
    for hd in range(N_HEADS):
        q_nope = q[:, hd * hw:hd * hw + QK_NOPE_DIM]
        q_r2 = q[:, hd * hw + QK_NOPE_DIM:(hd + 1) * hw]
        ss = jnp.sum(q_nope * q_nope + 0.5 * (q_r2 * q_r2), axis=-1, keepdims=True)
        rh = _rsqrt_mean(ss, QK_HEAD_DIM) * sm_scale
        q_ref[hd, :, :QK_NOPE_DIM] = (q_nope * rh * gq[:, :QK_NOPE_DIM]).astype(BF16)
        q_ref[hd, :, QK_NOPE_DIM:] = (_rope_pair(q_r2 * rh, table_q)[:, :QK_ROPE_DIM]).astype(BF16)

    u_ref[...] = (proj(w_all_ref, 1) * proj(w_all_ref, 2)).astype(BF16)
    _kv_finish(k_nope, v_t, kr2, gk_ref, table, k_ref, vt_ref)
    bx_ref[...] = proj(w_all_ref, 0).astype(BF16)


def _in_proj(x, mod_x, nmix, w_all, w_gate, w_kv_in, bgate, qan, wqb, kvan, w_k, w_vt, gq, gk,
             table):
    b, s, d = x.shape
    tm = TOKEN_TILE
    tok = pl.BlockSpec((None, tm, d), lambda bi, i: (bi, i, 0))

    def head_spec(width):
        return pl.BlockSpec((None, N_HEADS, tm, width), lambda bi, i: (bi, 0, i, 0))

    tok_shape = jax.ShapeDtypeStruct((b, s, d), BF16)
    return pl.pallas_call(
        _in_proj_kernel,
        grid=(b, s // tm),
        in_specs=[tok,
                  pl.BlockSpec((None, 1, mod_x.shape[-1]), lambda bi, i: (bi, 0, 0)),
                  _const_spec(nmix.shape),
                  _const_spec((d, 3 * d + Q_LORA_RANK)), _const_spec(w_gate.shape),
                  _const_spec(w_kv_in.shape), _const_spec(bgate.shape), _const_spec(qan.shape),
                  _const_spec(wqb.shape), _const_spec(kvan.shape), _const_spec(w_k.shape),
                  _const_spec(w_vt.shape), _const_spec(gq.shape), _const_spec(gk.shape),
                  pl.BlockSpec((tm, 2 * QK_ROPE_DIM), lambda bi, i: (i, 0))],
        out_specs=[tok, tok, tok, tok,
                   head_spec(QK_HEAD_DIM), head_spec(QK_HEAD_DIM),
                   pl.BlockSpec((None, N_HEADS, V_HEAD_DIM, tm), lambda bi, i: (bi, 0, 0, i))],
        out_shape=[tok_shape, tok_shape, tok_shape, tok_shape,
                   jax.ShapeDtypeStruct((b, N_HEADS, s, QK_HEAD_DIM), BF16),
                   jax.ShapeDtypeStruct((b, N_HEADS, s, QK_HEAD_DIM), BF16),
                   jax.ShapeDtypeStruct((b, N_HEADS, V_HEAD_DIM, s), BF16)],
        compiler_params=pltpu.CompilerParams(dimension_semantics=("arbitrary", "arbitrary"),
                                             vmem_limit_bytes=VMEM_LIMIT_BYTES),
        name="in_proj",
    )(x, mod_x, nmix, w_all, w_gate, w_kv_in, bgate, qan, wqb, kvan, w_k, w_vt, gq, gk, table)


def _attention_kernel(bound_ref, q_ref, kc_ref, vct_ref, kl_ref, vlt_ref, o_ref,
                      lt_ref, acct_ref, m_ref, l_ref, acc_ref):
    tq = q_ref.shape[0]

    def scores_t(unit):
        k_ref, _, k0, tk = unit
        return lax.dot_general(k_ref[k0:k0 + tk, :], q_ref[...], CONTRACT_LAST,
                               preferred_element_type=F32)

    def finish_bounded(unit, st):
        _, vt_ref, k0, tk = unit
        pt = jnp.exp2(st)
        lt_ref[...] += pt.reshape(tk // SUBLANES, SUBLANES, tq).sum(axis=0)
        acct_ref[...] += jnp.dot(vt_ref[:, k0:k0 + tk], pt.astype(BF16),
                                 preferred_element_type=F32)

    def bounded_path():
        lt_ref[...] = jnp.zeros(lt_ref.shape, F32)
        acct_ref[...] = jnp.zeros(acct_ref.shape, F32)
        tk = ATTN_KV_TILE
        units = [(kc_ref, vct_ref, 0, kc_ref.shape[0])]
        units += [(kl_ref, vlt_ref, k0, tk) for k0 in range(0, kl_ref.shape[0], tk)]
        pending = [scores_t(u) for u in units[:ATTN_LOOKAHEAD]]
        for i, unit in enumerate(units):
            if i + ATTN_LOOKAHEAD < len(units):
                pending.append(scores_t(units[i + ATTN_LOOKAHEAD]))
            finish_bounded(unit, pending[i])
        l = jnp.sum(lt_ref[...], axis=0, keepdims=True)
        o_ref[...] = (acct_ref[...] / l).T.astype(o_ref.dtype)

    def scores(unit):
        k_ref, _, k0, tk, r0 = unit
        return lax.dot_general(q_ref[r0:r0 + ATTN_Q_SUB, :], k_ref[pl.ds(k0, tk), :],
                               CONTRACT_LAST, preferred_element_type=F32)

    def finish_online(unit, s):
        _, vt_ref, k0, tk, r0 = unit
        rows = slice(r0, r0 + ATTN_Q_SUB)
        cols = [s[:, c:c + LANES] for c in range(0, tk, LANES)]
        m_old = m_ref[rows, :]
        row_max = jnp.max(functools.reduce(jnp.maximum, cols), axis=-1, keepdims=True)
        m_new = jnp.maximum(m_old, row_max)
        alpha = jnp.exp2(m_old - m_new)
        ps = [jnp.exp2(col - m_new) for col in cols]
        l_ref[rows, :] = alpha * l_ref[rows, :] + functools.reduce(jnp.add, ps)
        p = jnp.concatenate([x.astype(BF16) for x in ps], axis=1)
        pv = lax.dot_general(p, vt_ref[:, pl.ds(k0, tk)], CONTRACT_LAST,
                             preferred_element_type=F32)
        acc_ref[rows, :] = alpha * acc_ref[rows, :] + pv
        m_ref[rows, :] = m_new

    def run_online(units):
        pending = [scores(u) for u in units[:ATTN_LOOKAHEAD]]
        for i, unit in enumerate(units):
            if i + ATTN_LOOKAHEAD < len(units):
                pending.append(scores(units[i + ATTN_LOOKAHEAD]))
            finish_online(unit, pending[i])

    def units_of(k_ref, vt_ref, starts, tk):
        return [(k_ref, vt_ref, k0, tk, r0) for k0 in starts for r0 in range(0, tq, ATTN_Q_SUB)]

    def online_path():
        m_ref[...] = jnp.full(m_ref.shape, -jnp.inf, F32)
        l_ref[...] = jnp.zeros(l_ref.shape, F32)
        acc_ref[...] = jnp.zeros(acc_ref.shape, F32)
        tk = ATTN_KV_TILE_ONLINE
        span = tk * ATTN_KV_UNROLL_ONLINE
        run_online(units_of(kc_ref, vct_ref, [0], kc_ref.shape[0]))

        def body(j, carry):
            base = pl.multiple_of(j * span, span)
            run_online(units_of(kl_ref, vlt_ref,
                                [base + u * tk for u in range(ATTN_KV_UNROLL_ONLINE)], tk))
            return carry

        lax.fori_loop(0, kl_ref.shape[0] // span, body, 0)
        l = jnp.sum(l_ref[...], axis=-1, keepdims=True)
        o_ref[...] = (acc_ref[...] / l).astype(o_ref.dtype)

    bounded = bound_ref[0] <= ATTN_EXP2_SAFE_RANGE
    pl.when(bounded)(bounded_path)
    pl.when(jnp.logical_not(bounded))(online_path)


def _attention(score_bound, q, k_ctx, vt_ctx, k_lat, vt_lat):
    b, nh, s, dk = q.shape
    nc = k_ctx.shape[2]
    dv = vt_lat.shape[2]
    assert dv == LANES
    tq = ATTN_Q_TILE

    def per_head(rows, cols):
        return pl.BlockSpec((None, None, rows, cols), lambda bi, hi, i: (bi, hi, 0, 0))

    return pl.pallas_call(
        _attention_kernel,
        grid=(b, nh, s // tq),
        in_specs=[pl.BlockSpec(memory_space=pltpu.SMEM),
                  pl.BlockSpec((None, None, tq, dk), lambda bi, hi, i: (bi, hi, i, 0)),
                  per_head(nc, dk), per_head(dv, nc), per_head(s, dk), per_head(dv, s)],
        out_specs=pl.BlockSpec((None, tq, dv), lambda bi, hi, i: (bi, i, hi)),
        out_shape=jax.ShapeDtypeStruct((b, s, nh * dv), BF16),
        scratch_shapes=[pltpu.VMEM((SUBLANES, tq), F32), pltpu.VMEM((dv, tq), F32),
                        pltpu.VMEM((tq, LANES), F32), pltpu.VMEM((tq, LANES), F32),
                        pltpu.VMEM((tq, dv), F32)],
        compiler_params=pltpu.CompilerParams(
            dimension_semantics=("arbitrary", "arbitrary", "arbitrary"),
            vmem_limit_bytes=VMEM_LIMIT_BYTES),
        name="attention",
    )(score_bound, q, k_ctx, vt_ctx, k_lat, vt_lat)


def _merge_kernel(x_ref, mod_ref, bx_ref, u_ref, u_prev_ref, u_next_ref, gc_ref, ga_ref,
                  attn_ref, convw_ref, convb_ref, w_conv_out_ref, w_attn_o_ref, w_out_ref,
                  o_ref):
    i = pl.program_id(1)
    tm, d = x_ref.shape
    y_attn = jnp.dot(attn_ref[...], w_attn_o_ref[...], preferred_element_type=F32)
    u = u_ref[...].astype(F32)
    prev_row = u_prev_ref[SUBLANES - 1:SUBLANES, :].astype(F32) * (i > 0).astype(F32)
    next_row = u_next_ref[0:1, :].astype(F32) * (i < pl.num_programs(1) - 1).astype(F32)
    row = lax.broadcasted_iota(jnp.int32, (tm, d), 0)
    u_m1 = jnp.where(row == 0, prev_row, pltpu.roll(u, 1, axis=0))
    u_p1 = jnp.where(row == tm - 1, next_row, pltpu.roll(u, tm - 1, axis=0))
    cw = convw_ref[...]
    conv = u_m1 * cw[0:1, :] + u * cw[1:2, :] + u_p1 * cw[2:3, :] + convb_ref[...]
    z = (bx_ref[...].astype(F32) * conv).astype(BF16)
    y_conv = jnp.dot(z, w_conv_out_ref[...], preferred_element_type=F32)
    merged = (gc_ref[...].astype(F32) * y_conv + ga_ref[...].astype(F32) * y_attn).astype(BF16)
    y = jnp.dot(merged, w_out_ref[...], preferred_element_type=F32)
    g1 = mod_ref[:, 2 * d:3 * d]
    o_ref[...] = x_ref[...] + g1 * y


def _merge(x, mod_x, bx, u, gc, ga, attn, conv_w, conv_b, w_conv_out, w_attn_o, w_out):
    b, s, d = x.shape
    tm = TOKEN_TILE
    hb = tm // SUBLANES
    last_hb = s // SUBLANES - 1
    tok = pl.BlockSpec((None, tm, d), lambda bi, i: (bi, i, 0))
    prev = pl.BlockSpec((None, SUBLANES, d), lambda bi, i: (bi, jnp.maximum(i * hb - 1, 0), 0))
    nxt = pl.BlockSpec((None, SUBLANES, d),
                       lambda bi, i: (bi, jnp.minimum((i + 1) * hb, last_hb), 0))
    return pl.pallas_call(
        _merge_kernel,
        grid=(b, s // tm),
        in_specs=[tok,
                  pl.BlockSpec((None, 1, mod_x.shape[-1]), lambda bi, i: (bi, 0, 0)),
                  tok, tok, prev, nxt, tok, tok, tok,
                  _const_spec(conv_w.shape), _const_spec(conv_b.shape),
                  _const_spec(w_conv_out.shape), _const_spec(w_attn_o.shape),
                  _const_spec(w_out.shape)],
        out_specs=tok,
        out_shape=jax.ShapeDtypeStruct((b, s, d), F32),
        compiler_params=pltpu.CompilerParams(dimension_semantics=("arbitrary", "arbitrary"),
                                             vmem_limit_bytes=VMEM_LIMIT_BYTES),
        name="merge",
    )(x, mod_x, bx, u, u, u, gc, ga, attn, conv_w, conv_b, w_conv_out, w_attn_o, w_out)


def _ffn_kernel(x_ref, mod_ref, nffn_ref, w_in_ref, w_out_ref, o_ref, act_ref):
    d = x_ref.shape[-1]
    d_ff = w_out_ref.shape[0]
    x = x_ref[...]
    mod = mod_ref[...]
    h = _modulated_norm(x, nffn_ref[...], mod[:, 3 * d:4 * d], mod[:, 4 * d:5 * d]).astype(BF16)
    for j in range(d_ff // FFN_CHUNK):
        c0 = j * FFN_CHUNK
        gate = jnp.dot(h, w_in_ref[:, c0:c0 + FFN_CHUNK], preferred_element_type=F32)
        up = jnp.dot(h, w_in_ref[:, d_ff + c0:d_ff + c0 + FFN_CHUNK],
                     preferred_element_type=F32)
        act_ref[:, c0:c0 + FFN_CHUNK] = (gate * _sigmoid(gate) * up).astype(BF16)
    y = jnp.dot(act_ref[...], w_out_ref[...], preferred_element_type=F32)
    o_ref[...] = x + mod[:, 5 * d:6 * d] * y


def _ffn(x, mod_x, nffn, w_ffn_in, w_ffn_out):
    b, s, d = x.shape
    tm = TOKEN_TILE
    d_ff = w_ffn_out.shape[0]
    tok = pl.BlockSpec((None, tm, d), lambda bi, i: (bi, i, 0))
    return pl.pallas_call(
        _ffn_kernel,
        grid=(b, s // tm),
        in_specs=[tok,
                  pl.BlockSpec((None, 1, mod_x.shape[-1]), lambda bi, i: (bi, 0, 0)),
                  _const_spec(nffn.shape), _const_spec(w_ffn_in.shape),
                  _const_spec(w_ffn_out.shape)],
        out_specs=tok,
        out_shape=jax.ShapeDtypeStruct((b, s, d), F32),
        scratch_shapes=[pltpu.VMEM((tm, d_ff), BF16)],
        compiler_params=pltpu.CompilerParams(dimension_semantics=("arbitrary", "arbitrary"),
                                             vmem_limit_bytes=VMEM_LIMIT_BYTES),
        name="ffn",
    )(x, mod_x, nffn, w_ffn_in, w_ffn_out)


def _rope_table(seq_len):
    quarter = QK_ROPE_DIM // 4
    freqs = ROPE_THETA ** (-np.arange(quarter, dtype=np.float64) / quarter)
    t = np.arange(seq_len)
    ang = [(t // GRID_W)[:, None] * freqs, (t % GRID_W)[:, None] * freqs]
    cos = [np.cos(a) for a in ang]
    sin = [np.sin(a) for a in ang]
    c = np.concatenate([cos[0], cos[0], cos[1], cos[1]], axis=-1)
    s = np.concatenate([-sin[0], sin[0], -sin[1], sin[1]], axis=-1)
    return jnp.asarray(np.concatenate([c, s], axis=-1), F32)


def _swap_halves(t):
    q = QK_ROPE_DIM // 4
    return jnp.concatenate([t[..., q:2 * q], t[..., :q], t[..., 3 * q:], t[..., 2 * q:3 * q]],
                           axis=-1)


def _norm_gain_pair(g):
    rope = g[QK_NOPE_DIM:]
    return jnp.concatenate([g[:QK_NOPE_DIM], rope, _swap_halves(rope)])[None, :].astype(F32)


def kernel(x, c, ctx, c_ctx, w_mod, b_mod, norm_mix, norm_ffn, w_in, b_gate, conv_w, conv_b,
           w_conv_out, q_a_norm, w_q_b, kv_a_norm, w_kv_b, q_norm, k_norm, w_attn_o, w_out,
           w_ffn_in, w_ffn_out):
    depth = w_mod.shape[0]
    assert depth == 1, "context stream update is only needed between layers"
    b, s, d = x.shape
    n_ctx = ctx.shape[1]
    assert s % TOKEN_TILE == 0 and s % ATTN_Q_TILE == 0 and s % ATTN_KV_TILE == 0
    assert s % (ATTN_KV_TILE_ONLINE * ATTN_KV_UNROLL_ONLINE) == 0
    assert b + 1 <= SUBLANES

    cond = jnp.zeros((SUBLANES, d), F32).at[:b].set(c).at[b].set(c_ctx)
    mod = _adaln(cond, w_mod, b_mod)
    mod_x = mod[:b, None, :]
    mod_ctx = mod[b:b + 1]

    wi = w_in[0]
    o_q = 3 * d
    o_kv = o_q + Q_LORA_RANK
    o_kr = o_kv + KV_LORA_RANK
    o_gc = o_kr + QK_ROPE_DIM
    assert o_q % LANES == 0 and o_kv % LANES == 0
    w_all = wi.astype(BF16)
    w_gate = w_all[:, o_gc:]
    w_kr = w_all[:, o_kr:o_gc]
    w_kv_in = jnp.concatenate([w_all[:, o_kv:o_kr], w_kr, _swap_halves(w_kr)], axis=1)
    wq = w_q_b[0].reshape(Q_LORA_RANK, N_HEADS, QK_HEAD_DIM)
    wq = jnp.concatenate([wq, _swap_halves(wq[..., QK_NOPE_DIM:])], axis=-1)
    wqb = wq.reshape(Q_LORA_RANK, N_HEADS * 2 * LANES).astype(BF16)
    wkv = w_kv_b[0].reshape(KV_LORA_RANK, N_HEADS, QK_NOPE_DIM + V_HEAD_DIM)
    w_k = wkv[..., :QK_NOPE_DIM].reshape(KV_LORA_RANK, -1).astype(BF16)
    w_vt = wkv[..., QK_NOPE_DIM:].reshape(KV_LORA_RANK, -1).T.astype(BF16)
    gq = _norm_gain_pair(q_norm[0])
    gk = _norm_gain_pair(k_norm[0])
    table = _rope_table(s)
    table_ctx = jnp.concatenate([jnp.ones((n_ctx, QK_ROPE_DIM), F32),
                                 jnp.zeros((n_ctx, QK_ROPE_DIM), F32)], axis=1)
    nmix = norm_mix[0][None, :]
    kvan = kv_a_norm[0][None, :]

    k_ctx, vt_ctx = _ctx_kv(ctx, mod_ctx, nmix, w_kv_in, kvan, w_k, w_vt, gk, table_ctx)
    bx, u, gc, ga, q, k_lat, vt_lat = _in_proj(
        x, mod_x, nmix, w_all, w_gate, w_kv_in, b_gate[0][None, :], q_a_norm[0][None, :], wqb,
        kvan, w_k, w_vt, gq, gk, table)
    score_bound = (QK_HEAD_DIM * SM_SCALE_LOG2 * BF16_ROUNDING_MARGIN
                   * jnp.max(jnp.abs(q_norm[0])) * jnp.max(jnp.abs(k_norm[0]))).reshape(1)
    attn = _attention(score_bound.astype(F32), q, k_ctx, vt_ctx, k_lat, vt_lat)
    x_mid = _merge(x, mod_x, bx, u, gc, ga, attn, conv_w[0], conv_b[0][None, :],
                   w_conv_out[0].astype(BF16), w_attn_o[0].astype(BF16), w_out[0].astype(BF16))
    return _ffn(x_mid, mod_x, norm_ffn[0][None, :], w_ffn_in[0].astype(BF16),
                w_ffn_out[0].astype(BF16))
```

```python
import functools

import jax
import jax.numpy as jnp
import numpy as np
from jax import lax
from jax.experimental import pallas as pl
from jax.experimental.pallas import tpu as pltpu

F32 = jnp.float32
BF16 = jnp.bfloat16

N_HEADS = 8
QK_NOPE_DIM = 128
QK_ROPE_DIM = 64
QK_HEAD_DIM = QK_NOPE_DIM + QK_ROPE_DIM
V_HEAD_DIM = 128
Q_LORA_RANK = 384
KV_LORA_RANK = 256
GRID_W = 64
ROPE_THETA = 10000.0
NORM_EPS = 1e-6
MOD_CHUNKS = 6
SM_SCALE_LOG2 = float(QK_HEAD_DIM ** -0.5 * np.log2(np.e))
ATTN_EXP2_SAFE_RANGE = 40.0
BF16_ROUNDING_MARGIN = 1.01

LANES = 128
SUBLANES = 8
CONTRACT_LAST = (((1,), (1,)), ((), ()))
VMEM_LIMIT_BYTES = 56 * 1024 * 1024

TOKEN_TILE = 512
ATTN_Q_TILE = 2048
ATTN_Q_SUB = 256
ATTN_KV_TILE = 1024
ATTN_KV_TILE_ONLINE = 512
ATTN_KV_UNROLL_ONLINE = 2
ATTN_LOOKAHEAD = 2
ADALN_N_TILE = 1536
FFN_CHUNK = 256


def _rsqrt_mean(ss, n):
    return lax.rsqrt(ss * (1.0 / n) + NORM_EPS)


def _sigmoid(t):
    return 0.5 * jnp.tanh(0.5 * t) + 0.5


def _const_spec(shape):
    nd = len(shape)
    return pl.BlockSpec(shape, lambda *_: (0,) * nd, pipeline_mode=pl.Buffered(1))


def _adaln_kernel(cond_ref, w_ref, b_ref, o_ref):
    c = cond_ref[...]
    a = c * _sigmoid(c)
    o_ref[...] = jnp.dot(a, w_ref[...], preferred_element_type=F32,
                         precision=lax.Precision.HIGHEST) + b_ref[...]


def _adaln(cond, w_mod, b_mod):
    rows, d = cond.shape
    n = w_mod.shape[-1]
    return pl.pallas_call(
        _adaln_kernel,
        grid=(n // ADALN_N_TILE,),
        in_specs=[pl.BlockSpec((rows, d), lambda j: (0, 0)),
                  pl.BlockSpec((None, d, ADALN_N_TILE), lambda j: (0, 0, j)),
                  pl.BlockSpec((1, ADALN_N_TILE), lambda j: (0, j))],
        out_specs=pl.BlockSpec((rows, ADALN_N_TILE), lambda j: (0, j)),
        out_shape=jax.ShapeDtypeStruct((rows, n), F32),
        compiler_params=pltpu.CompilerParams(dimension_semantics=("arbitrary",),
                                             vmem_limit_bytes=VMEM_LIMIT_BYTES),
        name="adaln",
    )(cond, w_mod, b_mod)


def _modulated_norm(x, gain, shift, scale):
    d = x.shape[-1]
    r = _rsqrt_mean(jnp.sum(x * x, axis=-1, keepdims=True), d)
    return (x * r) * (gain * (1.0 + scale)) + shift


def _rope_pair(t2, table):
    w = t2 * table
    return w + pltpu.roll(w, QK_ROPE_DIM, axis=1)


def _kv_down(h, w_kv_in_ref):
    return jnp.dot(h, w_kv_in_ref[...], preferred_element_type=F32)


def _kv_up(kvp, kvan_ref, w_k_ref, w_vt_ref):
    kv_a = kvp[:, :KV_LORA_RANK]
    r = _rsqrt_mean(jnp.sum(kv_a * kv_a, axis=-1, keepdims=True), KV_LORA_RANK)
    kv_n = (kv_a * r * kvan_ref[...]).astype(BF16)
    k_nope = jnp.dot(kv_n, w_k_ref[...], preferred_element_type=F32)
    v_t = lax.dot_general(w_vt_ref[...], kv_n, CONTRACT_LAST, preferred_element_type=F32)
    return k_nope, v_t, kvp[:, KV_LORA_RANK:]


def _kv_finish(kv, v_t, kr2, gk_ref, table, k_ref, vt_ref):
    gk = gk_ref[...]
    ss_rope = 0.5 * jnp.sum(kr2 * kr2, axis=-1, keepdims=True)
    kr = _rope_pair(kr2 * gk[:, QK_NOPE_DIM:], table)[:, :QK_ROPE_DIM]
    for hd in range(N_HEADS):
        k_nope = kv[:, hd * QK_NOPE_DIM:(hd + 1) * QK_NOPE_DIM]
        ss = jnp.sum(k_nope * k_nope, axis=-1, keepdims=True) + ss_rope
        rh = _rsqrt_mean(ss, QK_HEAD_DIM)
        k_ref[hd, :, :QK_NOPE_DIM] = (k_nope * rh * gk[:, :QK_NOPE_DIM]).astype(BF16)
        k_ref[hd, :, QK_NOPE_DIM:] = (kr * rh).astype(BF16)
        vt_ref[hd] = v_t[hd * V_HEAD_DIM:(hd + 1) * V_HEAD_DIM, :].astype(BF16)


def _ctx_kv_kernel(ctx_ref, mod_ref, nmix_ref, w_kv_in_ref, kvan_ref, w_k_ref, w_vt_ref, gk_ref,
                   table_ref, k_ref, vt_ref):
    d = ctx_ref.shape[-1]
    mod = mod_ref[...]
    h = _modulated_norm(ctx_ref[...], nmix_ref[...], mod[:, :d], mod[:, d:2 * d]).astype(BF16)
    k_nope, v_t, kr2 = _kv_up(_kv_down(h, w_kv_in_ref), kvan_ref, w_k_ref, w_vt_ref)
    _kv_finish(k_nope, v_t, kr2, gk_ref, table_ref[...], k_ref, vt_ref)


def _ctx_kv(ctx, mod_ctx, nmix, w_kv_in, kvan, w_k, w_vt, gk, table):
    b, n, d = ctx.shape
    return pl.pallas_call(
        _ctx_kv_kernel,
        grid=(b,),
        in_specs=[pl.BlockSpec((None, n, d), lambda i: (i, 0, 0)),
                  _const_spec(mod_ctx.shape), _const_spec(nmix.shape),
                  _const_spec(w_kv_in.shape), _const_spec(kvan.shape),
                  _const_spec(w_k.shape), _const_spec(w_vt.shape), _const_spec(gk.shape),
                  _const_spec(table.shape)],
        out_specs=[pl.BlockSpec((None, N_HEADS, n, QK_HEAD_DIM), lambda i: (i, 0, 0, 0)),
                   pl.BlockSpec((None, N_HEADS, V_HEAD_DIM, n), lambda i: (i, 0, 0, 0))],
        out_shape=[jax.ShapeDtypeStruct((b, N_HEADS, n, QK_HEAD_DIM), BF16),
                   jax.ShapeDtypeStruct((b, N_HEADS, V_HEAD_DIM, n), BF16)],
        compiler_params=pltpu.CompilerParams(dimension_semantics=("arbitrary",),
                                             vmem_limit_bytes=VMEM_LIMIT_BYTES),
        name="ctx_kv",
    )(ctx, mod_ctx, nmix, w_kv_in, kvan, w_k, w_vt, gk, table)


def _in_proj_kernel(x_ref, mod_ref, nmix_ref, w_all_ref, w_gate_ref, w_kv_in_ref, bgate_ref,
                    qan_ref, wqb_ref, kvan_ref, w_k_ref, w_vt_ref, gq_ref, gk_ref, table_ref,
                    bx_ref, u_ref, gc_ref, ga_ref, q_ref, k_ref, vt_ref):
    d = x_ref.shape[-1]
    mod = mod_ref[...]
    h = _modulated_norm(x_ref[...], nmix_ref[...], mod[:, :d], mod[:, d:2 * d]).astype(BF16)
    table = table_ref[...]

    def proj(w_ref, j):
        return jnp.dot(h, w_ref[:, j * d:(j + 1) * d], preferred_element_type=F32)

    q_a = jnp.dot(h, w_all_ref[:, 3 * d:3 * d + Q_LORA_RANK], preferred_element_type=F32)
    kvp = _kv_down(h, w_kv_in_ref)
    bgate = bgate_ref[...]
    gc_ref[...] = _sigmoid(proj(w_gate_ref, 0) + bgate[:, :d]).astype(BF16)
    r = _rsqrt_mean(jnp.sum(q_a * q_a, axis=-1, keepdims=True), Q_LORA_RANK)
    q_n = (q_a * r * qan_ref[...]).astype(BF16)
    q = jnp.dot(q_n, wqb_ref[...], preferred_element_type=F32)
    k_nope, v_t, kr2 = _kv_up(kvp, kvan_ref, w_k_ref, w_vt_ref)
    ga_ref[...] = _sigmoid(proj(w_gate_ref, 1) + bgate[:, d:]).astype(BF16)

    gq = gq_ref[...]
    table_q = table * gq[:, QK_NOPE_DIM:]
    sm_scale = SM_SCALE_LOG2
    hw = 2 * LANES
    for hd in range(N_HEADS):
        q_nope = q[:, hd * hw:hd * hw + QK_NOPE_DIM]
        q_r2 = q[:, hd * hw + QK_NOPE_DIM:(hd + 1) * hw]
        ss = jnp.sum(q_nope * q_nope + 0.5 * (q_r2 * q_r2), axis=-1, keepdims=True)
        rh = _rsqrt_mean(ss, QK_HEAD_DIM) * sm_scale
        q_ref[hd, :, :QK_NOPE_DIM] = (q_nope * rh * gq[:, :QK_NOPE_DIM]).astype(BF16)
        q_ref[hd, :, QK_NOPE_DIM:] = (_rope_pair(q_r2 * rh, table_q)[:, :QK_ROPE_DIM]).astype(BF16)

    u_ref[...] = (proj(w_all_ref, 1) * proj(w_all_ref, 2)).astype(BF16)
    _kv_finish(k_nope, v_t, kr2, gk_ref, table, k_ref, vt_ref)
    bx_ref[...] = proj(w_all_ref, 0).astype(BF16)


def _in_proj(x, mod_x, nmix, w_all, w_gate, w_kv_in, bgate, qan, wqb, kvan, w_k, w_vt, gq, gk,
             table):
    b, s, d = x.shape
    tm = TOKEN_TILE
    tok = pl.BlockSpec((None, tm, d), lambda bi, i: (bi, i, 0))

    def head_spec(width):
        return pl.BlockSpec((None, N_HEADS, tm, width), lambda bi, i: (bi, 0, i, 0))

    tok_shape = jax.ShapeDtypeStruct((b, s, d), BF16)
    return pl.pallas_call(
        _in_proj_kernel,
        grid=(b, s // tm),
        in_specs=[tok,
                  pl.BlockSpec((None, 1, mod_x.shape[-1]), lambda bi, i: (bi, 0, 0)),
                  _const_spec(nmix.shape),
                  _const_spec((d, 3 * d + Q_LORA_RANK)), _const_spec(w_gate.shape),
                  _const_spec(w_kv_in.shape), _const_spec(bgate.shape), _const_spec(qan.shape),
                  _const_spec(wqb.shape), _const_spec(kvan.shape), _const_spec(w_k.shape),
                  _const_spec(w_vt.shape), _const_spec(gq.shape), _const_spec(gk.shape),
                  pl.BlockSpec((tm, 2 * QK_ROPE_DIM), lambda bi, i: (i, 0))],
        out_specs=[tok, tok, tok, tok,
                   head_spec(QK_HEAD_DIM), head_spec(QK_HEAD_DIM),
                   pl.BlockSpec((None, N_HEADS, V_HEAD_DIM, tm), lambda bi, i: (bi, 0, 0, i))],
        out_shape=[tok_shape, tok_shape, tok_shape, tok_shape,
                   jax.ShapeDtypeStruct((b, N_HEADS, s, QK_HEAD_DIM), BF16),
                   jax.ShapeDtypeStruct((b, N_HEADS, s, QK_HEAD_DIM), BF16),
                   jax.ShapeDtypeStruct((b, N_HEADS, V_HEAD_DIM, s), BF16)],
        compiler_params=pltpu.CompilerParams(dimension_semantics=("arbitrary", "arbitrary"),
                                             vmem_limit_bytes=VMEM_LIMIT_BYTES),
        name="in_proj",
    )(x, mod_x, nmix, w_all, w_gate, w_kv_in, bgate, qan, wqb, kvan, w_k, w_vt, gq, gk, table)


def _attention_kernel(bound_ref, q_ref, kc_ref, vct_ref, kl_ref, vlt_ref, o_ref,
                      lt_ref, acct_ref, m_ref, l_ref, acc_ref):
    tq = q_ref.shape[0]

    def scores_t(unit):
        k_ref, _, k0, tk = unit
        return lax.dot_general(k_ref[k0:k0 + tk, :], q_ref[...], CONTRACT_LAST,
                               preferred_element_type=F32)

    def finish_bounded(unit, st):
        _, vt_ref, k0, tk = unit
        pt = jnp.exp2(st)
        lt_ref[...] += pt.reshape(tk // SUBLANES, SUBLANES, tq).sum(axis=0)
        acct_ref[...] += jnp.dot(vt_ref[:, k0:k0 + tk], pt.astype(BF16),
                                 preferred_element_type=F32)

    def bounded_path():
        lt_ref[...] = jnp.zeros(lt_ref.shape, F32)
        acct_ref[...] = jnp.zeros(acct_ref.shape, F32)
        tk = ATTN_KV_TILE
        units = [(kc_ref, vct_ref, 0, kc_ref.shape[0])]
        units += [(kl_ref, vlt_ref, k0, tk) for k0 in range(0, kl_ref.shape[0], tk)]
        pending = [scores_t(u) for u in units[:ATTN_LOOKAHEAD]]
        for i, unit in enumerate(units):
            if i + ATTN_LOOKAHEAD < len(units):
                pending.append(scores_t(units[i + ATTN_LOOKAHEAD]))
            finish_bounded(unit, pending[i])
        l = jnp.sum(lt_ref[...], axis=0, keepdims=True)
        o_ref[...] = (acct_ref[...] / l).T.astype(o_ref.dtype)

    def scores(unit):
        k_ref, _, k0, tk, r0 = unit
        return lax.dot_general(q_ref[r0:r0 + ATTN_Q_SUB, :], k_ref[pl.ds(k0, tk), :],
                               CONTRACT_LAST, preferred_element_type=F32)

    def finish_online(unit, s):
        _, vt_ref, k0, tk, r0 = unit
        rows = slice(r0, r0 + ATTN_Q_SUB)
        cols = [s[:, c:c + LANES] for c in range(0, tk, LANES)]
        m_old = m_ref[rows, :]
        row_max = jnp.max(functools.reduce(jnp.maximum, cols), axis=-1, keepdims=True)
        m_new = jnp.maximum(m_old, row_max)
        alpha = jnp.exp2(m_old - m_new)
        ps = [jnp.exp2(col - m_new) for col in cols]
        l_ref[rows, :] = alpha * l_ref[rows, :] + functools.reduce(jnp.add, ps)
        p = jnp.concatenate([x.astype(BF16) for x in ps], axis=1)
        pv = lax.dot_general(p, vt_ref[:, pl.ds(k0, tk)], CONTRACT_LAST,
                             preferred_element_type=F32)
        acc_ref[rows, :] = alpha * acc_ref[rows, :] + pv
        m_ref[rows, :] = m_new

    def run_online(units):
        pending = [scores(u) for u in units[:ATTN_LOOKAHEAD]]
        for i, unit in enumerate(units):
            if i + ATTN_LOOKAHEAD < len(units):
                pending.append(scores(units[i + ATTN_LOOKAHEAD]))
            finish_online(unit, pending[i])

    def units_of(k_ref, vt_ref, starts, tk):
        return [(k_ref, vt_ref, k0, tk, r0) for k0 in starts for r0 in range(0, tq, ATTN_Q_SUB)]

    def online_path():
        m_ref[...] = jnp.full(m_ref.shape, -jnp.inf, F32)
        l_ref[...] = jnp.zeros(l_ref.shape, F32)
        acc_ref[...] = jnp.zeros(acc_ref.shape, F32)
        tk = ATTN_KV_TILE_ONLINE
        span = tk * ATTN_KV_UNROLL_ONLINE
        run_online(units_of(kc_ref, vct_ref, [0], kc_ref.shape[0]))

        def body(j, carry):
            base = pl.multiple_of(j * span, span)
            run_online(units_of(kl_ref, vlt_ref,
                                [base + u * tk for u in range(ATTN_KV_UNROLL_ONLINE)], tk))
            return carry

        lax.fori_loop(0, kl_ref.shape[0] // span, body, 0)
        l = jnp.sum(l_ref[...], axis=-1, keepdims=True)
        o_ref[...] = (acc_ref[...] / l).astype(o_ref.dtype)

    bounded = bound_ref[0] <= ATTN_EXP2_SAFE_RANGE
    pl.when(bounded)(bounded_path)
    pl.when(jnp.logical_not(bounded))(online_path)


def _attention(score_bound, q, k_ctx, vt_ctx, k_lat, vt_lat):
    b, nh, s, dk = q.shape
    nc = k_ctx.shape[2]
    dv = vt_lat.shape[2]
    assert dv == LANES
    tq = ATTN_Q_TILE

    def per_head(rows, cols):
        return pl.BlockSpec((None, None, rows, cols), lambda bi, hi, i: (bi, hi, 0, 0))

    return pl.pallas_call(
        _attention_kernel,
        grid=(b, nh, s // tq),
        in_specs=[pl.BlockSpec(memory_space=pltpu.SMEM),
                  pl.BlockSpec((None, None, tq, dk), lambda bi, hi, i: (bi, hi, i, 0)),
                  per_head(nc, dk), per_head(dv, nc), per_head(s, dk), per_head(dv, s)],
        out_specs=pl.BlockSpec((None, tq, dv), lambda bi, hi, i: (bi, i, hi)),
        out_shape=jax.ShapeDtypeStruct((b, s, nh * dv), BF16),
        scratch_shapes=[pltpu.VMEM((SUBLANES, tq), F32), pltpu.VMEM((dv, tq), F32),
                        pltpu.VMEM((tq, LANES), F32), pltpu.VMEM((tq, LANES), F32),
                        pltpu.VMEM((tq, dv), F32)],
        compiler_params=pltpu.CompilerParams(
            dimension_semantics=("arbitrary", "arbitrary", "arbitrary"),
            vmem_limit_bytes=VMEM_LIMIT_BYTES),
        name="attention",
    )(score_bound, q, k_ctx, vt_ctx, k_lat, vt_lat)


def _merge_kernel(x_ref, mod_ref, bx_ref, u_ref, u_prev_ref, u_next_ref, gc_ref, ga_ref,
                  attn_ref, convw_ref, convb_ref, w_conv_out_ref, w_attn_o_ref, w_out_ref,
                  o_ref):
    i = pl.program_id(1)
    tm, d = x_ref.shape
    y_attn = jnp.dot(attn_ref[...], w_attn_o_ref[...], preferred_element_type=F32)
    u = u_ref[...].astype(F32)
    prev_row = u_prev_ref[SUBLANES - 1:SUBLANES, :].astype(F32) * (i > 0).astype(F32)
    next_row = u_next_ref[0:1, :].astype(F32) * (i < pl.num_programs(1) - 1).astype(F32)
    row = lax.broadcasted_iota(jnp.int32, (tm, d), 0)
    u_m1 = jnp.where(row == 0, prev_row, pltpu.roll(u, 1, axis=0))
    u_p1 = jnp.where(row == tm - 1, next_row, pltpu.roll(u, tm - 1, axis=0))
    cw = convw_ref[...]
    conv = u_m1 * cw[0:1, :] + u * cw[1:2, :] + u_p1 * cw[2:3, :] + convb_ref[...]
    z = (bx_ref[...].astype(F32) * conv).astype(BF16)
    y_conv = jnp.dot(z, w_conv_out_ref[...], preferred_element_type=F32)
    merged = (gc_ref[...].astype(F32) * y_conv + ga_ref[...].astype(F32) * y_attn).astype(BF16)
    y = jnp.dot(merged, w_out_ref[...], preferred_element_type=F32)
    g1 = mod_ref[:, 2 * d:3 * d]
    o_ref[...] = x_ref[...] + g1 * y


def _merge(x, mod_x, bx, u, gc, ga, attn, conv_w, conv_b, w_conv_out, w_attn_o, w_out):
    b, s, d = x.shape
    tm = TOKEN_TILE
    hb = tm // SUBLANES
    last_hb = s // SUBLANES - 1
    tok = pl.BlockSpec((None, tm, d), lambda bi, i: (bi, i, 0))
    prev = pl.BlockSpec((None, SUBLANES, d), lambda bi, i: (bi, jnp.maximum(i * hb - 1, 0), 0))
    nxt = pl.BlockSpec((None, SUBLANES, d),
                       lambda bi, i: (bi, jnp.minimum((i + 1) * hb, last_hb), 0))
    return pl.pallas_call(
        _merge_kernel,
        grid=(b, s // tm),
        in_specs=[tok,
                  pl.BlockSpec((None, 1, mod_x.shape[-1]), lambda bi, i: (bi, 0, 0)),
                  tok, tok, prev, nxt, tok, tok, tok,
                  _const_spec(conv_w.shape), _const_spec(conv_b.shape),
                  _const_spec(w_conv_out.shape), _const_spec(w_attn_o.shape),
                  _const_spec(w_out.shape)],
        out_specs=tok,
        out_shape=jax.ShapeDtypeStruct((b, s, d), F32),
        compiler_params=pltpu.CompilerParams(dimension_semantics=("arbitrary", "arbitrary"),
                                             vmem_limit_bytes=VMEM_LIMIT_BYTES),
        name="merge",
    )(x, mod_x, bx, u, u, u, gc, ga, attn, conv_w, conv_b, w_conv_out, w_attn_o, w_out)


def _ffn_kernel(x_ref, mod_ref, nffn_ref, w_in_ref, w_out_ref, o_ref, act_ref):
    d = x_ref.shape[-1]
    d_ff = w_out_ref.shape[0]
    x = x_ref[...]
    mod = mod_ref[...]
    h = _modulated_norm(x, nffn_ref[...], mod[:, 3 * d:4 * d], mod[:, 4 * d:5 * d]).astype(BF16)
    for j in range(d_ff // FFN_CHUNK):
        c0 = j * FFN_CHUNK
        gate = jnp.dot(h, w_in_ref[:, c0:c0 + FFN_CHUNK], preferred_element_type=F32)
        up = jnp.dot(h, w_in_ref[:, d_ff + c0:d_ff + c0 + FFN_CHUNK],
                     preferred_element_type=F32)
        act_ref[:, c0:c0 + FFN_CHUNK] = (gate * _sigmoid(gate) * up).astype(BF16)
    y = jnp.dot(act_ref[...], w_out_ref[...], preferred_element_type=F32)
    o_ref[...] = x + mod[:, 5 * d:6 * d] * y


def _ffn(x, mod_x, nffn, w_ffn_in, w_ffn_out):
    b, s, d = x.shape
    tm = TOKEN_TILE
    d_ff = w_ffn_out.shape[0]
    tok = pl.BlockSpec((None, tm, d), lambda bi, i: (bi, i, 0))
    return pl.pallas_call(
        _ffn_kernel,
        grid=(b, s // tm),
        in_specs=[tok,
                  pl.BlockSpec((None, 1, mod_x.shape[-1]), lambda bi, i: (bi, 0, 0)),
                  _const_spec(nffn.shape), _const_spec(w_ffn_in.shape),
                  _const_spec(w_ffn_out.shape)],
        out_specs=tok,
        out_shape=jax.ShapeDtypeStruct((b, s, d), F32),
        scratch_shapes=[pltpu.VMEM((tm, d_ff), BF16)],
        compiler_params=pltpu.CompilerParams(dimension_semantics=("arbitrary", "arbitrary"),
                                             vmem_limit_bytes=VMEM_LIMIT_BYTES),
        name="ffn",
    )(x, mod_x, nffn, w_ffn_in, w_ffn_out)


def _rope_table(seq_len):
    quarter = QK_ROPE_DIM // 4
    freqs = ROPE_THETA ** (-np.arange(quarter, dtype=np.float64) / quarter)
    t = np.arange(seq_len)
    ang = [(t // GRID_W)[:, None] * freqs, (t % GRID_W)[:, None] * freqs]
    cos = [np.cos(a) for a in ang]
    sin = [np.sin(a) for a in ang]
    c = np.concatenate([cos[0], cos[0], cos[1], cos[1]], axis=-1)
    s = np.concatenate([-sin[0], sin[0], -sin[1], sin[1]], axis=-1)
    return jnp.asarray(np.concatenate([c, s], axis=-1), F32)


def _swap_halves(t):
    q = QK_ROPE_DIM // 4
    return jnp.concatenate([t[..., q:2 * q], t[..., :q], t[..., 3 * q:], t[..., 2 * q:3 * q]],
                           axis=-1)


def _norm_gain_pair(g):
    rope = g[QK_NOPE_DIM:]
    return jnp.concatenate([g[:QK_NOPE_DIM], rope, _swap_halves(rope)])[None, :].astype(F32)


def kernel(x, c, ctx, c_ctx, w_mod, b_mod, norm_mix, norm_ffn, w_in, b_gate, conv_w, conv_b,
           w_conv_out, q_a_norm, w_q_b, kv_a_norm, w_kv_b, q_norm, k_norm, w_attn_o, w_out,
           w_ffn_in, w_ffn_out):
    depth = w_mod.shape[0]
    assert depth == 1, "context stream update is only needed between layers"
    b, s, d = x.shape
    n_ctx = ctx.shape[1]
    assert s % TOKEN_TILE == 0 and s % ATTN_Q_TILE == 0 and s % ATTN_KV_TILE == 0
    assert s % (ATTN_KV_TILE_ONLINE * ATTN_KV_UNROLL_ONLINE) == 0
    assert b + 1 <= SUBLANES
    assert w_mod.shape[-1] == MOD_CHUNKS * d and w_ffn_out.shape[1] % FFN_CHUNK == 0

    cond = jnp.zeros((SUBLANES, d), F32).at[:b].set(c).at[b].set(c_ctx)
    mod = _adaln(cond, w_mod, b_mod)
    mod_x = mod[:b, None, :]
    mod_ctx = mod[b:b + 1]

    wi = w_in[0]
    o_q = 3 * d
    o_kv = o_q + Q_LORA_RANK
    o_kr = o_kv + KV_LORA_RANK
    o_gc = o_kr + QK_ROPE_DIM
    assert o_q % LANES == 0 and o_kv % LANES == 0
    w_all = wi.astype(BF16)
    w_gate = w_all[:, o_gc:]
    w_kr = w_all[:, o_kr:o_gc]
    w_kv_in = jnp.concatenate([w_all[:, o_kv:o_kr], w_kr, _swap_halves(w_kr)], axis=1)
    wq = w_q_b[0].reshape(Q_LORA_RANK, N_HEADS, QK_HEAD_DIM)
    wq = jnp.concatenate([wq, _swap_halves(wq[..., QK_NOPE_DIM:])], axis=-1)
    wqb = wq.reshape(Q_LORA_RANK, N_HEADS * 2 * LANES).astype(BF16)
    wkv = w_kv_b[0].reshape(KV_LORA_RANK, N_HEADS, QK_NOPE_DIM + V_HEAD_DIM)
    w_k = wkv[..., :QK_NOPE_DIM].reshape(KV_LORA_RANK, -1).astype(BF16)
    w_vt = wkv[..., QK_NOPE_DIM:].reshape(KV_LORA_RANK, -1).T.astype(BF16)
    gq = _norm_gain_pair(q_norm[0])
    gk = _norm_gain_pair(k_norm[0])
    table = _rope_table(s)
    table_ctx = jnp.concatenate([jnp.ones((n_ctx, QK_ROPE_DIM), F32),
                                 jnp.zeros((n_ctx, QK_ROPE_DIM), F32)], axis=1)
    nmix = norm_mix[0][None, :]
    kvan = kv_a_norm[0][None, :]

    k_ctx, vt_ctx = _ctx_kv(ctx, mod_ctx, nmix, w_kv_in, kvan, w_k, w_vt, gk, table_ctx)
    bx, u, gc, ga, q, k_lat, vt_lat = _in_proj(
        x, mod_x, nmix, w_all, w_gate, w_kv_in, b_gate[0][None, :], q_a_norm[0][None, :], wqb,
        kvan, w_k, w_vt, gq, gk, table)
    score_bound = (QK_HEAD_DIM * SM_SCALE_LOG2 * BF16_ROUNDING_MARGIN
                   * jnp.max(jnp.abs(q_norm[0])) * jnp.max(jnp.abs(k_norm[0]))).reshape(1)
    attn = _attention(score_bound.astype(F32), q, k_ctx, vt_ctx, k_lat, vt_lat)
    x_mid = _merge(x, mod_x, bx, u, gc, ga, attn, conv_w[0], conv_b[0][None, :],
                   w_conv_out[0].astype(BF16), w_attn_o[0].astype(BF16), w_out[0].astype(BF16))
    return _ffn(x_mid, mod_x, norm_ffn[0][None, :], w_ffn_in[0].astype(BF16),
                w_ffn_out[0].astype(BF16))
```

```python
import functools

import jax
import jax.numpy as jnp
import numpy as np
from jax import lax
from jax.experimental import pallas as pl
from jax.experimental.pallas import tpu as pltpu

F32 = jnp.float32
BF16 = jnp.bfloat16

N_HEADS = 8
QK_NOPE_DIM = 128
QK_ROPE_DIM = 64
QK_HEAD_DIM = QK_NOPE_DIM + QK_ROPE_DIM
V_HEAD_DIM = 128
Q_LORA_RANK = 384
KV_LORA_RANK = 256
GRID_W = 64
ROPE_THETA = 10000.0
NORM_EPS = 1e-6
MOD_CHUNKS = 6
SM_SCALE_LOG2 = float(QK_HEAD_DIM ** -0.5 * np.log2(np.e))
ATTN_EXP2_SAFE_RANGE = 40.0
BF16_ROUNDING_MARGIN = 1.01

LANES = 128
SUBLANES = 8
BF16_SUBLANES = 16
CONTRACT_LAST = (((1,), (1,)), ((), ()))
VMEM_LIMIT_BYTES = 56 * 1024 * 1024

TOKEN_TILE = 512
ATTN_Q_TILE = 2048
ATTN_Q_SUB = 256
ATTN_KV_TILE = 1024
ATTN_KV_TILE_ONLINE = 512
ATTN_KV_UNROLL_ONLINE = 2
ATTN_LOOKAHEAD = 2
ADALN_N_TILE = 1536
FFN_CHUNK = 256


def _rsqrt_mean(ss, n):
    return lax.rsqrt(ss * (1.0 / n) + NORM_EPS)


def _sigmoid(t):
    return 0.5 * jnp.tanh(0.5 * t) + 0.5


def _const_spec(shape):
    nd = len(shape)
    return pl.BlockSpec(shape, lambda *_: (0,) * nd, pipeline_mode=pl.Buffered(1))


def _adaln_kernel(cond_ref, w_ref, b_ref, o_ref):
    c = cond_ref[...]
    a = c * _sigmoid(c)
    o_ref[...] = jnp.dot(a, w_ref[...], preferred_element_type=F32,
                         precision=lax.Precision.HIGHEST) + b_ref[...]


def _adaln(cond, w_mod, b_mod):
    rows, d = cond.shape
    n = w_mod.shape[-1]
    return pl.pallas_call(
        _adaln_kernel,
        grid=(n // ADALN_N_TILE,),
        in_specs=[pl.BlockSpec((rows, d), lambda j: (0, 0)),
                  pl.BlockSpec((None, d, ADALN_N_TILE), lambda j: (0, 0, j)),
                  pl.BlockSpec((1, ADALN_N_TILE), lambda j: (0, j))],
        out_specs=pl.BlockSpec((rows, ADALN_N_TILE), lambda j: (0, j)),
        out_shape=jax.ShapeDtypeStruct((rows, n), F32),
        compiler_params=pltpu.CompilerParams(dimension_semantics=("arbitrary",),
                                             vmem_limit_bytes=VMEM_LIMIT_BYTES),
        name="adaln",
    )(cond, w_mod, b_mod)


def _modulated_norm(x, gain, shift, scale):
    d = x.shape[-1]
    r = _rsqrt_mean(jnp.sum(x * x, axis=-1, keepdims=True), d)
    return (x * r) * (gain * (1.0 + scale)) + shift


def _rope_pair(t2, table):
    w = t2 * table
    return w + pltpu.roll(w, QK_ROPE_DIM, axis=1)


def _kv_down(h, w_kv_in_ref):
    return jnp.dot(h, w_kv_in_ref[...], preferred_element_type=F32)


def _kv_up(kvp, kvan_ref, w_k_ref, w_vt_ref):
    kv_a = kvp[:, :KV_LORA_RANK]
    r = _rsqrt_mean(jnp.sum(kv_a * kv_a, axis=-1, keepdims=True), KV_LORA_RANK)
    kv_n = (kv_a * r * kvan_ref[...]).astype(BF16)
    k_nope = jnp.dot(kv_n, w_k_ref[...], preferred_element_type=F32)
    v_t = lax.dot_general(w_vt_ref[...], kv_n, CONTRACT_LAST, preferred_element_type=F32)
    return k_nope, v_t, kvp[:, KV_LORA_RANK:]


def _kv_finish(kv, v_t, kr2, gk_ref, table, k_ref, vt_ref):
    gk = gk_ref[...]
    ss_rope = 0.5 * jnp.sum(kr2 * kr2, axis=-1, keepdims=True)
    kr = _rope_pair(kr2 * gk[:, QK_NOPE_DIM:], table)[:, :QK_ROPE_DIM]
    for hd in range(N_HEADS):
        k_nope = kv[:, hd * QK_NOPE_DIM:(hd + 1) * QK_NOPE_DIM]
        ss = jnp.sum(k_nope * k_nope, axis=-1, keepdims=True) + ss_rope
        rh = _rsqrt_mean(ss, QK_HEAD_DIM)
        k_ref[hd, :, :QK_NOPE_DIM] = (k_nope * rh * gk[:, :QK_NOPE_DIM]).astype(BF16)
        k_ref[hd, :, QK_NOPE_DIM:] = (kr * rh).astype(BF16)
        vt_ref[hd] = v_t[hd * V_HEAD_DIM:(hd + 1) * V_HEAD_DIM, :].astype(BF16)


def _ctx_kv_kernel(ctx_ref, mod_ref, nmix_ref, w_kv_in_ref, kvan_ref, w_k_ref, w_vt_ref, gk_ref,
                   table_ref, k_ref, vt_ref):
    d = ctx_ref.shape[-1]
    mod = mod_ref[...]
    h = _modulated_norm(ctx_ref[...], nmix_ref[...], mod[:, :d], mod[:, d:2 * d]).astype(BF16)
    k_nope, v_t, kr2 = _kv_up(_kv_down(h, w_kv_in_ref), kvan_ref, w_k_ref, w_vt_ref)
    _kv_finish(k_nope, v_t, kr2, gk_ref, table_ref[...], k_ref, vt_ref)


def _ctx_kv(ctx, mod_ctx, nmix, w_kv_in, kvan, w_k, w_vt, gk, table):
    b, n, d = ctx.shape
    return pl.pallas_call(
        _ctx_kv_kernel,
        grid=(b,),
        in_specs=[pl.BlockSpec((None, n, d), lambda i: (i, 0, 0)),
                  _const_spec(mod_ctx.shape), _const_spec(nmix.shape),
                  _const_spec(w_kv_in.shape), _const_spec(kvan.shape),
                  _const_spec(w_k.shape), _const_spec(w_vt.shape), _const_spec(gk.shape),
                  _const_spec(table.shape)],
        out_specs=[pl.BlockSpec((None, N_HEADS, n, QK_HEAD_DIM), lambda i: (i, 0, 0, 0)),
                   pl.BlockSpec((None, N_HEADS, V_HEAD_DIM, n), lambda i: (i, 0, 0, 0))],
        out_shape=[jax.ShapeDtypeStruct((b, N_HEADS, n, QK_HEAD_DIM), BF16),
                   jax.ShapeDtypeStruct((b, N_HEADS, V_HEAD_DIM, n), BF16)],
        compiler_params=pltpu.CompilerParams(dimension_semantics=("arbitrary",),
                                             vmem_limit_bytes=VMEM_LIMIT_BYTES),
        name="ctx_kv",
    )(ctx, mod_ctx, nmix, w_kv_in, kvan, w_k, w_vt, gk, table)


def _in_proj_kernel(x_ref, mod_ref, nmix_ref, w_all_ref, w_gate_ref, w_kv_in_ref, bgate_ref,
                    qan_ref, wqb_ref, kvan_ref, w_k_ref, w_vt_ref, gq_ref, gk_ref, table_ref,
                    bx_ref, u_ref, gc_ref, ga_ref, q_ref, k_ref, vt_ref):
    d = x_ref.shape[-1]
    mod = mod_ref[...]
    h = _modulated_norm(x_ref[...], nmix_ref[...], mod[:, :d], mod[:, d:2 * d]).astype(BF16)
    table = table_ref[...]

    def proj(w_ref, j):
        return jnp.dot(h, w_ref[:, j * d:(j + 1) * d], preferred_element_type=F32)

    q_a = jnp.dot(h, w_all_ref[:, 3 * d:3 * d + Q_LORA_RANK], preferred_element_type=F32)
    kvp = _kv_down(h, w_kv_in_ref)
    bgate = bgate_ref[...]
    gc_ref[...] = _sigmoid(proj(w_gate_ref, 0) + bgate[:, :d]).astype(BF16)
    r = _rsqrt_mean(jnp.sum(q_a * q_a, axis=-1, keepdims=True), Q_LORA_RANK)
    q_n = (q_a * r * qan_ref[...]).astype(BF16)
    q = jnp.dot(q_n, wqb_ref[...], preferred_element_type=F32)
    k_nope, v_t, kr2 = _kv_up(kvp, kvan_ref, w_k_ref, w_vt_ref)
    ga_ref[...] = _sigmoid(proj(w_gate_ref, 1) + bgate[:, d:]).astype(BF16)

    gq = gq_ref[...]
    table_q = table * gq[:, QK_NOPE_DIM:]
    sm_scale = SM_SCALE_LOG2
    hw = 2 * LANES
    for hd in range(N_HEADS):
        q_nope = q[:, hd * hw:hd * hw + QK_NOPE_DIM]
        q_r2 = q[:, hd * hw + QK_NOPE_DIM:(hd + 1) * hw]
        ss = jnp.sum(q_nope * q_nope + 0.5 * (q_r2 * q_r2), axis=-1, keepdims=True)
        rh = _rsqrt_mean(ss, QK_HEAD_DIM) * sm_scale
        q_ref[hd, :, :QK_NOPE_DIM] = (q_nope * rh * gq[:, :QK_NOPE_DIM]).astype(BF16)
        q_ref[hd, :, QK_NOPE_DIM:] = (_rope_pair(q_r2 * rh, table_q)[:, :QK_ROPE_DIM]).astype(BF16)

    u_ref[...] = (proj(w_all_ref, 1) * proj(w_all_ref, 2)).astype(BF16)
    _kv_finish(k_nope, v_t, kr2, gk_ref, table, k_ref, vt_ref)
    bx_ref[...] = proj(w_all_ref, 0).astype(BF16)


def _in_proj(x, mod_x, nmix, w_all, w_gate, w_kv_in, bgate, qan, wqb, kvan, w_k, w_vt, gq, gk,
             table):
    b, s, d = x.shape
    tm = TOKEN_TILE
    tok = pl.BlockSpec((None, tm, d), lambda bi, i: (bi, i, 0))

    def head_spec(width):
        return pl.BlockSpec((None, N_HEADS, tm, width), lambda bi, i: (bi, 0, i, 0))

    tok_shape = jax.ShapeDtypeStruct((b, s, d), BF16)
    return pl.pallas_call(
        _in_proj_kernel,
        grid=(b, s // tm),
        in_specs=[tok,
                  pl.BlockSpec((None, 1, mod_x.shape[-1]), lambda bi, i: (bi, 0, 0)),
                  _const_spec(nmix.shape),
                  _const_spec((d, 3 * d + Q_LORA_RANK)), _const_spec(w_gate.shape),
                  _const_spec(w_kv_in.shape), _const_spec(bgate.shape), _const_spec(qan.shape),
                  _const_spec(wqb.shape), _const_spec(kvan.shape), _const_spec(w_k.shape),
                  _const_spec(w_vt.shape), _const_spec(gq.shape), _const_spec(gk.shape),
                  pl.BlockSpec((tm, 2 * QK_ROPE_DIM), lambda bi, i: (i, 0))],
        out_specs=[tok, tok, tok, tok,
                   head_spec(QK_HEAD_DIM), head_spec(QK_HEAD_DIM),
                   pl.BlockSpec((None, N_HEADS, V_HEAD_DIM, tm), lambda bi, i: (bi, 0, 0, i))],
        out_shape=[tok_shape, tok_shape, tok_shape, tok_shape,
                   jax.ShapeDtypeStruct((b, N_HEADS, s, QK_HEAD_DIM), BF16),
                   jax.ShapeDtypeStruct((b, N_HEADS, s, QK_HEAD_DIM), BF16),
                   jax.ShapeDtypeStruct((b, N_HEADS, V_HEAD_DIM, s), BF16)],
        compiler_params=pltpu.CompilerParams(dimension_semantics=("arbitrary", "arbitrary"),
                                             vmem_limit_bytes=VMEM_LIMIT_BYTES),
        name="in_proj",
    )(x, mod_x, nmix, w_all, w_gate, w_kv_in, bgate, qan, wqb, kvan, w_k, w_vt, gq, gk, table)


def _attention_kernel(bound_ref, q_ref, kc_ref, vct_ref, kl_ref, vlt_ref, o_ref,
                      lt_ref, acct_ref, m_ref, l_ref, acc_ref, side_work):
    tq = q_ref.shape[0]

    def scores_t(unit):
        k_ref, _, k0, tk = unit
        return lax.dot_general(k_ref[k0:k0 + tk, :], q_ref[...], CONTRACT_LAST,
                               preferred_element_type=F32)

    def finish_bounded(unit, st):
        _, vt_ref, k0, tk = unit
        pt = jnp.exp2(st)
        lt_ref[...] += pt.reshape(tk // SUBLANES, SUBLANES, tq).sum(axis=0)
        acct_ref[...] += jnp.dot(vt_ref[:, k0:k0 + tk], pt.astype(BF16),
                                 preferred_element_type=F32)

    def bounded_path():
        side_work()
        lt_ref[...] = jnp.zeros(lt_ref.shape, F32)
        acct_ref[...] = jnp.zeros(acct_ref.shape, F32)
        tk = ATTN_KV_TILE
        units = [(kc_ref, vct_ref, 0, kc_ref.shape[0])]
        units += [(kl_ref, vlt_ref, k0, tk) for k0 in range(0, kl_ref.shape[0], tk)]
        pending = [scores_t(u) for u in units[:ATTN_LOOKAHEAD]]
        for i, unit in enumerate(units):
            if i + ATTN_LOOKAHEAD < len(units):
                pending.append(scores_t(units[i + ATTN_LOOKAHEAD]))
            finish_bounded(unit, pending[i])
        l = jnp.sum(lt_ref[...], axis=0, keepdims=True)
        o_ref[...] = (acct_ref[...] / l).T.astype(o_ref.dtype)

    def scores(unit):
        k_ref, _, k0, tk, r0 = unit
        return lax.dot_general(q_ref[r0:r0 + ATTN_Q_SUB, :], k_ref[pl.ds(k0, tk), :],
                               CONTRACT_LAST, preferred_element_type=F32)

    def finish_online(unit, s):
        _, vt_ref, k0, tk, r0 = unit
        rows = slice(r0, r0 + ATTN_Q_SUB)
        cols = [s[:, c:c + LANES] for c in range(0, tk, LANES)]
        m_old = m_ref[rows, :]
        row_max = jnp.max(functools.reduce(jnp.maximum, cols), axis=-1, keepdims=True)
        m_new = jnp.maximum(m_old, row_max)
        alpha = jnp.exp2(m_old - m_new)
        ps = [jnp.exp2(col - m_new) for col in cols]
        l_ref[rows, :] = alpha * l_ref[rows, :] + functools.reduce(jnp.add, ps)
        p = jnp.concatenate([x.astype(BF16) for x in ps], axis=1)
        pv = lax.dot_general(p, vt_ref[:, pl.ds(k0, tk)], CONTRACT_LAST,
                             preferred_element_type=F32)
        acc_ref[rows, :] = alpha * acc_ref[rows, :] + pv
        m_ref[rows, :] = m_new

    def run_online(units):
        pending = [scores(u) for u in units[:ATTN_LOOKAHEAD]]
        for i, unit in enumerate(units):
            if i + ATTN_LOOKAHEAD < len(units):
                pending.append(scores(units[i + ATTN_LOOKAHEAD]))
            finish_online(unit, pending[i])

    def units_of(k_ref, vt_ref, starts, tk):
        return [(k_ref, vt_ref, k0, tk, r0) for k0 in starts for r0 in range(0, tq, ATTN_Q_SUB)]

    def online_path():
        side_work()
        m_ref[...] = jnp.full(m_ref.shape, -jnp.inf, F32)
        l_ref[...] = jnp.zeros(l_ref.shape, F32)
        acc_ref[...] = jnp.zeros(acc_ref.shape, F32)
        tk = ATTN_KV_TILE_ONLINE
        span = tk * ATTN_KV_UNROLL_ONLINE
        run_online(units_of(kc_ref, vct_ref, [0], kc_ref.shape[0]))

        def body(j, carry):
            base = pl.multiple_of(j * span, span)
            run_online(units_of(kl_ref, vlt_ref,
                                [base + u * tk for u in range(ATTN_KV_UNROLL_ONLINE)], tk))
            return carry

        lax.fori_loop(0, kl_ref.shape[0] // span, body, 0)
        l = jnp.sum(l_ref[...], axis=-1, keepdims=True)
        o_ref[...] = (acc_ref[...] / l).astype(o_ref.dtype)

    bounded = bound_ref[0] <= ATTN_EXP2_SAFE_RANGE
    pl.when(bounded)(bounded_path)
    pl.when(jnp.logical_not(bounded))(online_path)


def _attention_and_casts_kernel(n_cast, *refs):
    attn_in, rest = refs[:6], refs[6:]
    w_refs, rest = rest[:n_cast], rest[n_cast:]
    o_ref, rest = rest[0], rest[1:]
    wo_refs, scratch = rest[:n_cast], rest[n_cast:]

    def cast_weights():
        for w_ref, wo_ref in zip(w_refs, wo_refs):
            wo_ref[...] = w_ref[...].astype(wo_ref.dtype)

    _attention_kernel(*attn_in, o_ref, *scratch, cast_weights)


def _cast_plan(rows, steps):
    share = 1
    while steps % share == 0:
        blocks = steps // share
        if rows % blocks == 0 and (rows // blocks) % BF16_SUBLANES == 0:
            return blocks, share
        share *= 2
    raise ValueError(f"cannot split {rows} rows over {steps} grid steps")


def _attention(score_bound, q, k_ctx, vt_ctx, k_lat, vt_lat, weights):
    b, nh, s, dk = q.shape
    nc = k_ctx.shape[2]
    dv = vt_lat.shape[2]
    assert dv == LANES
    tq = ATTN_Q_TILE
    nq = s // tq
    steps = b * nh * nq

    def per_head(rows, cols):
        return pl.BlockSpec((None, None, rows, cols), lambda bi, hi, i: (bi, hi, 0, 0))

    w_in_specs, w_out_specs, w_shapes = [], [], []
    for w in weights:
        _, rows, cols = w.shape
        blocks, share = _cast_plan(rows, steps)

        def block_of(bi, hi, i, share=share):
            return ((bi * nh + hi) * nq + i) // share

        w_in_specs.append(pl.BlockSpec((None, rows // blocks, cols),
                                       lambda bi, hi, i, f=block_of: (0, f(bi, hi, i), 0)))
        w_out_specs.append(pl.BlockSpec((rows // blocks, cols),
                                        lambda bi, hi, i, f=block_of: (f(bi, hi, i), 0)))
        w_shapes.append(jax.ShapeDtypeStruct((rows, cols), BF16))

    attn, *w_bf16 = pl.pallas_call(
        functools.partial(_attention_and_casts_kernel, len(weights)),
        grid=(b, nh, nq),
        in_specs=[pl.BlockSpec(memory_space=pltpu.SMEM),
                  pl.BlockSpec((None, None, tq, dk), lambda bi, hi, i: (bi, hi, i, 0)),
                  per_head(nc, dk), per_head(dv, nc), per_head(s, dk), per_head(dv, s)]
        + w_in_specs,
        out_specs=[pl.BlockSpec((None, tq, dv), lambda bi, hi, i: (bi, i, hi))] + w_out_specs,
        out_shape=[jax.ShapeDtypeStruct((b, s, nh * dv), BF16)] + w_shapes,
        scratch_shapes=[pltpu.VMEM((SUBLANES, tq), F32), pltpu.VMEM((dv, tq), F32),
                        pltpu.VMEM((tq, LANES), F32), pltpu.VMEM((tq, LANES), F32),
                        pltpu.VMEM((tq, dv), F32)],
        compiler_params=pltpu.CompilerParams(
            dimension_semantics=("arbitrary", "arbitrary", "arbitrary"),
            vmem_limit_bytes=VMEM_LIMIT_BYTES),
        name="attention",
    )(score_bound, q, k_ctx, vt_ctx, k_lat, vt_lat, *weights)
    return attn, w_bf16


def _merge_kernel(x_ref, mod_ref, bx_ref, u_ref, u_prev_ref, u_next_ref, gc_ref, ga_ref,
                  attn_ref, convw_ref, convb_ref, w_conv_out_ref, w_attn_o_ref, w_out_ref,
                  o_ref):
    i = pl.program_id(1)
    tm, d = x_ref.shape
    y_attn = jnp.dot(attn_ref[...], w_attn_o_ref[...], preferred_element_type=F32)
    u = u_ref[...].astype(F32)
    prev_row = u_prev_ref[SUBLANES - 1:SUBLANES, :].astype(F32) * (i > 0).astype(F32)
    next_row = u_next_ref[0:1, :].astype(F32) * (i < pl.num_programs(1) - 1).astype(F32)
    row = lax.broadcasted_iota(jnp.int32, (tm, d), 0)
    u_m1 = jnp.where(row == 0, prev_row, pltpu.roll(u, 1, axis=0))
    u_p1 = jnp.where(row == tm - 1, next_row, pltpu.roll(u, tm - 1, axis=0))
    cw = convw_ref[...]
    conv = u_m1 * cw[0:1, :] + u * cw[1:2, :] + u_p1 * cw[2:3, :] + convb_ref[...]
    z = (bx_ref[...].astype(F32) * conv).astype(BF16)
    y_conv = jnp.dot(z, w_conv_out_ref[...], preferred_element_type=F32)
    merged = (gc_ref[...].astype(F32) * y_conv + ga_ref[...].astype(F32) * y_attn).astype(BF16)
    y = jnp.dot(merged, w_out_ref[...], preferred_element_type=F32)
    g1 = mod_ref[:, 2 * d:3 * d]
    o_ref[...] = x_ref[...] + g1 * y


def _merge(x, mod_x, bx, u, gc, ga, attn, conv_w, conv_b, w_conv_out, w_attn_o, w_out):
    b, s, d = x.shape
    tm = TOKEN_TILE
    hb = tm // SUBLANES
    last_hb = s // SUBLANES - 1
    tok = pl.BlockSpec((None, tm, d), lambda bi, i: (bi, i, 0))
    prev = pl.BlockSpec((None, SUBLANES, d), lambda bi, i: (bi, jnp.maximum(i * hb - 1, 0), 0))
    nxt = pl.BlockSpec((None, SUBLANES, d),
                       lambda bi, i: (bi, jnp.minimum((i + 1) * hb, last_hb), 0))
    return pl.pallas_call(
        _merge_kernel,
        grid=(b, s // tm),
        in_specs=[tok,
                  pl.BlockSpec((None, 1, mod_x.shape[-1]), lambda bi, i: (bi, 0, 0)),
                  tok, tok, prev, nxt, tok, tok, tok,
                  _const_spec(conv_w.shape), _const_spec(conv_b.shape),
                  _const_spec(w_conv_out.shape), _const_spec(w_attn_o.shape),
                  _const_spec(w_out.shape)],
        out_specs=tok,
        out_shape=jax.ShapeDtypeStruct((b, s, d), F32),
        compiler_params=pltpu.CompilerParams(dimension_semantics=("arbitrary", "arbitrary"),
                                             vmem_limit_bytes=VMEM_LIMIT_BYTES),
        name="merge",
    )(x, mod_x, bx, u, u, u, gc, ga, attn, conv_w, conv_b, w_conv_out, w_attn_o, w_out)


def _ffn_kernel(x_ref, mod_ref, nffn_ref, w_in_ref, w_out_ref, o_ref, act_ref):
    d = x_ref.shape[-1]
    d_ff = w_out_ref.shape[0]
    x = x_ref[...]
    mod = mod_ref[...]
    h = _modulated_norm(x, nffn_ref[...], mod[:, 3 * d:4 * d], mod[:, 4 * d:5 * d]).astype(BF16)
    for j in range(d_ff // FFN_CHUNK):
        c0 = j * FFN_CHUNK
        gate = jnp.dot(h, w_in_ref[:, c0:c0 + FFN_CHUNK], preferred_element_type=F32)
        up = jnp.dot(h, w_in_ref[:, d_ff + c0:d_ff + c0 + FFN_CHUNK],
                     preferred_element_type=F32)
        act_ref[:, c0:c0 + FFN_CHUNK] = (gate * _sigmoid(gate) * up).astype(BF16)
    y = jnp.dot(act_ref[...], w_out_ref[...], preferred_element_type=F32)
    o_ref[...] = x + mod[:, 5 * d:6 * d] * y


def _ffn(x, mod_x, nffn, w_ffn_in, w_ffn_out):
    b, s, d = x.shape
    tm = TOKEN_TILE
    d_ff = w_ffn_out.shape[0]
    tok = pl.BlockSpec((None, tm, d), lambda bi, i: (bi, i, 0))
    return pl.pallas_call(
        _ffn_kernel,
        grid=(b, s // tm),
        in_specs=[tok,
                  pl.BlockSpec((None, 1, mod_x.shape[-1]), lambda bi, i: (bi, 0, 0)),
                  _const_spec(nffn.shape), _const_spec(w_ffn_in.shape),
                  _const_spec(w_ffn_out.shape)],
        out_specs=tok,
        out_shape=jax.ShapeDtypeStruct((b, s, d), F32),
        scratch_shapes=[pltpu.VMEM((tm, d_ff), BF16)],
        compiler_params=pltpu.CompilerParams(dimension_semantics=("arbitrary", "arbitrary"),
                                             vmem_limit_bytes=VMEM_LIMIT_BYTES),
        name="ffn",
    )(x, mod_x, nffn, w_ffn_in, w_ffn_out)


def _rope_table(seq_len):
    quarter = QK_ROPE_DIM // 4
    freqs = ROPE_THETA ** (-np.arange(quarter, dtype=np.float64) / quarter)
    t = np.arange(seq_len)
    ang = [(t // GRID_W)[:, None] * freqs, (t % GRID_W)[:, None] * freqs]
    cos = [np.cos(a) for a in ang]
    sin = [np.sin(a) for a in ang]
    c = np.concatenate([cos[0], cos[0], cos[1], cos[1]], axis=-1)
    s = np.concatenate([-sin[0], sin[0], -sin[1], sin[1]], axis=-1)
    return jnp.asarray(np.concatenate([c, s], axis=-1), F32)


def _swap_halves(t):
    q = QK_ROPE_DIM // 4
    return jnp.concatenate([t[..., q:2 * q], t[..., :q], t[..., 3 * q:], t[..., 2 * q:3 * q]],
                           axis=-1)


def _norm_gain_pair(g):
    rope = g[QK_NOPE_DIM:]
    return jnp.concatenate([g[:QK_NOPE_DIM], rope, _swap_halves(rope)])[None, :].astype(F32)


def kernel(x, c, ctx, c_ctx, w_mod, b_mod, norm_mix, norm_ffn, w_in, b_gate, conv_w, conv_b,
           w_conv_out, q_a_norm, w_q_b, kv_a_norm, w_kv_b, q_norm, k_norm, w_attn_o, w_out,
           w_ffn_in, w_ffn_out):
    depth = w_mod.shape[0]
    assert depth == 1, "context stream update is only needed between layers"
    b, s, d = x.shape
    n_ctx = ctx.shape[1]
    assert s % TOKEN_TILE == 0 and s % ATTN_Q_TILE == 0 and s % ATTN_KV_TILE == 0
    assert s % (ATTN_KV_TILE_ONLINE * ATTN_KV_UNROLL_ONLINE) == 0
    assert b + 1 <= SUBLANES
    assert w_mod.shape[-1] == MOD_CHUNKS * d and w_ffn_out.shape[1] % FFN_CHUNK == 0

    cond = jnp.zeros((SUBLANES, d), F32).at[:b].set(c).at[b].set(c_ctx)
    mod = _adaln(cond, w_mod, b_mod)
    mod_x = mod[:b, None, :]
    mod_ctx = mod[b:b + 1]

    wi = w_in[0]
    o_q = 3 * d
    o_kv = o_q + Q_LORA_RANK
    o_kr = o_kv + KV_LORA_RANK
    o_gc = o_kr + QK_ROPE_DIM
    assert o_q % LANES == 0 and o_kv % LANES == 0
    w_all = wi.astype(BF16)
    w_gate = w_all[:, o_gc:]
    w_kr = w_all[:, o_kr:o_gc]
    w_kv_in = jnp.concatenate([w_all[:, o_kv:o_kr], w_kr, _swap_halves(w_kr)], axis=1)
    wq = w_q_b[0].reshape(Q_LORA_RANK, N_HEADS, QK_HEAD_DIM)
    wq = jnp.concatenate([wq, _swap_halves(wq[..., QK_NOPE_DIM:])], axis=-1)
    wqb = wq.reshape(Q_LORA_RANK, N_HEADS * 2 * LANES).astype(BF16)
    wkv = w_kv_b[0].reshape(KV_LORA_RANK, N_HEADS, QK_NOPE_DIM + V_HEAD_DIM)
    w_k = wkv[..., :QK_NOPE_DIM].reshape(KV_LORA_RANK, -1).astype(BF16)
    w_vt = wkv[..., QK_NOPE_DIM:].reshape(KV_LORA_RANK, -1).T.astype(BF16)
    gq = _norm_gain_pair(q_norm[0])
    gk = _norm_gain_pair(k_norm[0])
    table = _rope_table(s)
    table_ctx = jnp.concatenate([jnp.ones((n_ctx, QK_ROPE_DIM), F32),
                                 jnp.zeros((n_ctx, QK_ROPE_DIM), F32)], axis=1)
    nmix = norm_mix[0][None, :]
    kvan = kv_a_norm[0][None, :]

    k_ctx, vt_ctx = _ctx_kv(ctx, mod_ctx, nmix, w_kv_in, kvan, w_k, w_vt, gk, table_ctx)
    bx, u, gc, ga, q, k_lat, vt_lat = _in_proj(
        x, mod_x, nmix, w_all, w_gate, w_kv_in, b_gate[0][None, :], q_a_norm[0][None, :], wqb,
        kvan, w_k, w_vt, gq, gk, table)
    score_bound = (QK_HEAD_DIM * SM_SCALE_LOG2 * BF16_ROUNDING_MARGIN
                   * jnp.max(jnp.abs(q_norm[0])) * jnp.max(jnp.abs(k_norm[0]))).reshape(1)
    attn, (w_conv_out_b, w_attn_o_b, w_out_b, w_ffn_in_b, w_ffn_out_b) = _attention(
        score_bound.astype(F32), q, k_ctx, vt_ctx, k_lat, vt_lat,
        [w_conv_out, w_attn_o, w_out, w_ffn_in, w_ffn_out])
    x_mid = _merge(x, mod_x, bx, u, gc, ga, attn, conv_w[0], conv_b[0][None, :],
                   w_conv_out_b, w_attn_o_b, w_out_b)
    return _ffn(x_mid, mod_x, norm_ffn[0][None, :], w_ffn_in_b, w_ffn_out_b)
```

```python
import functools

import jax
import jax.numpy as jnp
import numpy as np
from jax import lax
from jax.experimental import pallas as pl
from jax.experimental.pallas import tpu as pltpu

F32 = jnp.float32
BF16 = jnp.bfloat16

N_HEADS = 8
QK_NOPE_DIM = 128
QK_ROPE_DIM = 64
QK_HEAD_DIM = QK_NOPE_DIM + QK_ROPE_DIM
V_HEAD_DIM = 128
Q_LORA_RANK = 384
KV_LORA_RANK = 256
GRID_W = 64
ROPE_THETA = 10000.0
NORM_EPS = 1e-6
MOD_CHUNKS = 6
SM_SCALE_LOG2 = float(QK_HEAD_DIM ** -0.5 * np.log2(np.e))
ATTN_EXP2_SAFE_RANGE = 40.0
BF16_ROUNDING_MARGIN = 1.01

LANES = 128
SUBLANES = 8
BF16_SUBLANES = 16
CONTRACT_LAST = (((1,), (1,)), ((), ()))
VMEM_LIMIT_BYTES = 56 * 1024 * 1024

TOKEN_TILE = 512
MERGE_TOKEN_TILE = 1024
ATTN_Q_TILE = 2048
ATTN_Q_SUB = 256
ATTN_KV_TILE = 1024
ATTN_KV_TILE_ONLINE = 512
ATTN_KV_UNROLL_ONLINE = 2
ATTN_LOOKAHEAD = 2
ADALN_N_TILE = 1536
FFN_CHUNK = 256


def _rsqrt_mean(ss, n):
    return lax.rsqrt(ss * (1.0 / n) + NORM_EPS)


def _sigmoid(t):
    return 0.5 * jnp.tanh(0.5 * t) + 0.5


def _const_spec(shape):
    nd = len(shape)
    return pl.BlockSpec(shape, lambda *_: (0,) * nd, pipeline_mode=pl.Buffered(1))


def _adaln_kernel(cond_ref, w_ref, b_ref, o_ref):
    c = cond_ref[...]
    a = c * _sigmoid(c)
    o_ref[...] = jnp.dot(a, w_ref[...], preferred_element_type=F32,
                         precision=lax.Precision.HIGHEST) + b_ref[...]


def _adaln(cond, w_mod, b_mod):
    rows, d = cond.shape
    n = w_mod.shape[-1]
    return pl.pallas_call(
        _adaln_kernel,
        grid=(n // ADALN_N_TILE,),
        in_specs=[pl.BlockSpec((rows, d), lambda j: (0, 0)),
                  pl.BlockSpec((None, d, ADALN_N_TILE), lambda j: (0, 0, j)),
                  pl.BlockSpec((1, ADALN_N_TILE), lambda j: (0, j))],
        out_specs=pl.BlockSpec((rows, ADALN_N_TILE), lambda j: (0, j)),
        out_shape=jax.ShapeDtypeStruct((rows, n), F32),
        compiler_params=pltpu.CompilerParams(dimension_semantics=("arbitrary",),
                                             vmem_limit_bytes=VMEM_LIMIT_BYTES),
        name="adaln",
    )(cond, w_mod, b_mod)


def _modulated_norm(x, gain, shift, scale):
    d = x.shape[-1]
    r = _rsqrt_mean(jnp.sum(x * x, axis=-1, keepdims=True), d)
    return (x * r) * (gain * (1.0 + scale)) + shift


def _rope_pair(t2, table):
    w = t2 * table
    return w + pltpu.roll(w, QK_ROPE_DIM, axis=1)


def _kv_down(h, w_kv_in_ref):
    return jnp.dot(h, w_kv_in_ref[...], preferred_element_type=F32)


def _kv_up(kvp, kvan_ref, w_k_ref, w_vt_ref):
    kv_a = kvp[:, :KV_LORA_RANK]
    r = _rsqrt_mean(jnp.sum(kv_a * kv_a, axis=-1, keepdims=True), KV_LORA_RANK)
    kv_n = (kv_a * r * kvan_ref[...]).astype(BF16)
    k_nope = jnp.dot(kv_n, w_k_ref[...], preferred_element_type=F32)
    v_t = lax.dot_general(w_vt_ref[...], kv_n, CONTRACT_LAST, preferred_element_type=F32)
    return k_nope, v_t, kvp[:, KV_LORA_RANK:]


def _kv_finish(kv, v_t, kr2, gk_ref, table, k_ref, vt_ref):
    gk = gk_ref[...]
    ss_rope = 0.5 * jnp.sum(kr2 * kr2, axis=-1, keepdims=True)
    kr = _rope_pair(kr2 * gk[:, QK_NOPE_DIM:], table)[:, :QK_ROPE_DIM]
    for hd in range(N_HEADS):
        k_nope = kv[:, hd * QK_NOPE_DIM:(hd + 1) * QK_NOPE_DIM]
        ss = jnp.sum(k_nope * k_nope, axis=-1, keepdims=True) + ss_rope
        rh = _rsqrt_mean(ss, QK_HEAD_DIM)
        k_ref[hd, :, :QK_NOPE_DIM] = (k_nope * rh * gk[:, :QK_NOPE_DIM]).astype(BF16)
        k_ref[hd, :, QK_NOPE_DIM:] = (kr * rh).astype(BF16)
        vt_ref[hd] = v_t[hd * V_HEAD_DIM:(hd + 1) * V_HEAD_DIM, :].astype(BF16)


def _ctx_kv_kernel(ctx_ref, mod_ref, nmix_ref, w_kv_in_ref, kvan_ref, w_k_ref, w_vt_ref, gk_ref,
                   table_ref, k_ref, vt_ref):
    d = ctx_ref.shape[-1]
    mod = mod_ref[...]
    h = _modulated_norm(ctx_ref[...], nmix_ref[...], mod[:, :d], mod[:, d:2 * d]).astype(BF16)
    k_nope, v_t, kr2 = _kv_up(_kv_down(h, w_kv_in_ref), kvan_ref, w_k_ref, w_vt_ref)
    _kv_finish(k_nope, v_t, kr2, gk_ref, table_ref[...], k_ref, vt_ref)


def _ctx_kv(ctx, mod_ctx, nmix, w_kv_in, kvan, w_k, w_vt, gk, table):
    b, n, d = ctx.shape
    return pl.pallas_call(
        _ctx_kv_kernel,
        grid=(b,),
        in_specs=[pl.BlockSpec((None, n, d), lambda i: (i, 0, 0)),
                  _const_spec(mod_ctx.shape), _const_spec(nmix.shape),
                  _const_spec(w_kv_in.shape), _const_spec(kvan.shape),
                  _const_spec(w_k.shape), _const_spec(w_vt.shape), _const_spec(gk.shape),
                  _const_spec(table.shape)],
        out_specs=[pl.BlockSpec((None, N_HEADS, n, QK_HEAD_DIM), lambda i: (i, 0, 0, 0)),
                   pl.BlockSpec((None, N_HEADS, V_HEAD_DIM, n), lambda i: (i, 0, 0, 0))],
        out_shape=[jax.ShapeDtypeStruct((b, N_HEADS, n, QK_HEAD_DIM), BF16),
                   jax.ShapeDtypeStruct((b, N_HEADS, V_HEAD_DIM, n), BF16)],
        compiler_params=pltpu.CompilerParams(dimension_semantics=("arbitrary",),
                                             vmem_limit_bytes=VMEM_LIMIT_BYTES),
        name="ctx_kv",
    )(ctx, mod_ctx, nmix, w_kv_in, kvan, w_k, w_vt, gk, table)


def _in_proj_kernel(x_ref, mod_ref, nmix_ref, w_all_ref, w_gate_ref, w_kv_in_ref, bgate_ref,
                    qan_ref, wqb_ref, kvan_ref, w_k_ref, w_vt_ref, gq_ref, gk_ref, table_ref,
                    bx_ref, u_ref, gc_ref, ga_ref, q_ref, k_ref, vt_ref):
    d = x_ref.shape[-1]
    mod = mod_ref[...]
    h = _modulated_norm(x_ref[...], nmix_ref[...], mod[:, :d], mod[:, d:2 * d]).astype(BF16)
    table = table_ref[...]

    def proj(w_ref, j):
        return jnp.dot(h, w_ref[:, j * d:(j + 1) * d], preferred_element_type=F32)

    q_a = jnp.dot(h, w_all_ref[:, 3 * d:3 * d + Q_LORA_RANK], preferred_element_type=F32)
    kvp = _kv_down(h, w_kv_in_ref)
    bgate = bgate_ref[...]
    gc_ref[...] = _sigmoid(proj(w_gate_ref, 0) + bgate[:, :d]).astype(BF16)
    r = _rsqrt_mean(jnp.sum(q_a * q_a, axis=-1, keepdims=True), Q_LORA_RANK)
    q_n = (q_a * r * qan_ref[...]).astype(BF16)
    q = jnp.dot(q_n, wqb_ref[...], preferred_element_type=F32)
    k_nope, v_t, kr2 = _kv_up(kvp, kvan_ref, w_k_ref, w_vt_ref)
    ga_ref[...] = _sigmoid(proj(w_gate_ref, 1) + bgate[:, d:]).astype(BF16)

    gq = gq_ref[...]
    table_q = table * gq[:, QK_NOPE_DIM:]
    sm_scale = SM_SCALE_LOG2
    hw = 2 * LANES
    for hd in range(N_HEADS):
        q_nope = q[:, hd * hw:hd * hw + QK_NOPE_DIM]
        q_r2 = q[:, hd * hw + QK_NOPE_DIM:(hd + 1) * hw]
        ss = jnp.sum(q_nope * q_nope + 0.5 * (q_r2 * q_r2), axis=-1, keepdims=True)
        rh = _rsqrt_mean(ss, QK_HEAD_DIM) * sm_scale
        q_ref[hd, :, :QK_NOPE_DIM] = (q_nope * rh * gq[:, :QK_NOPE_DIM]).astype(BF16)
        q_ref[hd, :, QK_NOPE_DIM:] = (_rope_pair(q_r2 * rh, table_q)[:, :QK_ROPE_DIM]).astype(BF16)

    u_ref[...] = (proj(w_all_ref, 1) * proj(w_all_ref, 2)).astype(BF16)
    _kv_finish(k_nope, v_t, kr2, gk_ref, table, k_ref, vt_ref)
    bx_ref[...] = proj(w_all_ref, 0).astype(BF16)


def _in_proj(x, mod_x, nmix, w_all, w_gate, w_kv_in, bgate, qan, wqb, kvan, w_k, w_vt, gq, gk,
             table):
    b, s, d = x.shape
    tm = TOKEN_TILE
    tok = pl.BlockSpec((None, tm, d), lambda bi, i: (bi, i, 0))

    def head_spec(width):
        return pl.BlockSpec((None, N_HEADS, tm, width), lambda bi, i: (bi, 0, i, 0))

    tok_shape = jax.ShapeDtypeStruct((b, s, d), BF16)
    return pl.pallas_call(
        _in_proj_kernel,
        grid=(b, s // tm),
        in_specs=[tok,
                  pl.BlockSpec((None, 1, mod_x.shape[-1]), lambda bi, i: (bi, 0, 0)),
                  _const_spec(nmix.shape),
                  _const_spec((d, 3 * d + Q_LORA_RANK)), _const_spec(w_gate.shape),
                  _const_spec(w_kv_in.shape), _const_spec(bgate.shape), _const_spec(qan.shape),
                  _const_spec(wqb.shape), _const_spec(kvan.shape), _const_spec(w_k.shape),
                  _const_spec(w_vt.shape), _const_spec(gq.shape), _const_spec(gk.shape),
                  pl.BlockSpec((tm, 2 * QK_ROPE_DIM), lambda bi, i: (i, 0))],
        out_specs=[tok, tok, tok, tok,
                   head_spec(QK_HEAD_DIM), head_spec(QK_HEAD_DIM),
                   pl.BlockSpec((None, N_HEADS, V_HEAD_DIM, tm), lambda bi, i: (bi, 0, 0, i))],
        out_shape=[tok_shape, tok_shape, tok_shape, tok_shape,
                   jax.ShapeDtypeStruct((b, N_HEADS, s, QK_HEAD_DIM), BF16),
                   jax.ShapeDtypeStruct((b, N_HEADS, s, QK_HEAD_DIM), BF16),
                   jax.ShapeDtypeStruct((b, N_HEADS, V_HEAD_DIM, s), BF16)],
        compiler_params=pltpu.CompilerParams(dimension_semantics=("arbitrary", "arbitrary"),
                                             vmem_limit_bytes=VMEM_LIMIT_BYTES),
        name="in_proj",
    )(x, mod_x, nmix, w_all, w_gate, w_kv_in, bgate, qan, wqb, kvan, w_k, w_vt, gq, gk, table)


def _attention_kernel(bound_ref, q_ref, kc_ref, vct_ref, kl_ref, vlt_ref, o_ref,
                      lt_ref, acct_ref, m_ref, l_ref, acc_ref, side_work):
    tq = q_ref.shape[0]

    def scores_t(unit):
        k_ref, _, k0, tk = unit
        return lax.dot_general(k_ref[k0:k0 + tk, :], q_ref[...], CONTRACT_LAST,
                               preferred_element_type=F32)

    def finish_bounded(unit, st):
        _, vt_ref, k0, tk = unit
        pt = jnp.exp2(st)
        lt_ref[...] += pt.reshape(tk // SUBLANES, SUBLANES, tq).sum(axis=0)
        acct_ref[...] += jnp.dot(vt_ref[:, k0:k0 + tk], pt.astype(BF16),
                                 preferred_element_type=F32)

    def bounded_path():
        side_work()
        lt_ref[...] = jnp.zeros(lt_ref.shape, F32)
        acct_ref[...] = jnp.zeros(acct_ref.shape, F32)
        tk = ATTN_KV_TILE
        units = [(kc_ref, vct_ref, 0, kc_ref.shape[0])]
        units += [(kl_ref, vlt_ref, k0, tk) for k0 in range(0, kl_ref.shape[0], tk)]
        pending = [scores_t(u) for u in units[:ATTN_LOOKAHEAD]]
        for i, unit in enumerate(units):
            if i + ATTN_LOOKAHEAD < len(units):
                pending.append(scores_t(units[i + ATTN_LOOKAHEAD]))
            finish_bounded(unit, pending[i])
        l = jnp.sum(lt_ref[...], axis=0, keepdims=True)
        o_ref[...] = (acct_ref[...] / l).T.astype(o_ref.dtype)

    def scores(unit):
        k_ref, _, k0, tk, r0 = unit
        return lax.dot_general(q_ref[r0:r0 + ATTN_Q_SUB, :], k_ref[pl.ds(k0, tk), :],
                               CONTRACT_LAST, preferred_element_type=F32)

    def finish_online(unit, s):
        _, vt_ref, k0, tk, r0 = unit
        rows = slice(r0, r0 + ATTN_Q_SUB)
        cols = [s[:, c:c + LANES] for c in range(0, tk, LANES)]
        m_old = m_ref[rows, :]
        row_max = jnp.max(functools.reduce(jnp.maximum, cols), axis=-1, keepdims=True)
        m_new = jnp.maximum(m_old, row_max)
        alpha = jnp.exp2(m_old - m_new)
        ps = [jnp.exp2(col - m_new) for col in cols]
        l_ref[rows, :] = alpha * l_ref[rows, :] + functools.reduce(jnp.add, ps)
        p = jnp.concatenate([x.astype(BF16) for x in ps], axis=1)
        pv = lax.dot_general(p, vt_ref[:, pl.ds(k0, tk)], CONTRACT_LAST,
                             preferred_element_type=F32)
        acc_ref[rows, :] = alpha * acc_ref[rows, :] + pv
        m_ref[rows, :] = m_new

    def run_online(units):
        pending = [scores(u) for u in units[:ATTN_LOOKAHEAD]]
        for i, unit in enumerate(units):
            if i + ATTN_LOOKAHEAD < len(units):
                pending.append(scores(units[i + ATTN_LOOKAHEAD]))
            finish_online(unit, pending[i])

    def units_of(k_ref, vt_ref, starts, tk):
        return [(k_ref, vt_ref, k0, tk, r0) for k0 in starts for r0 in range(0, tq, ATTN_Q_SUB)]

    def online_path():
        side_work()
        m_ref[...] = jnp.full(m_ref.shape, -jnp.inf, F32)
        l_ref[...] = jnp.zeros(l_ref.shape, F32)
        acc_ref[...] = jnp.zeros(acc_ref.shape, F32)
        tk = ATTN_KV_TILE_ONLINE
        span = tk * ATTN_KV_UNROLL_ONLINE
        run_online(units_of(kc_ref, vct_ref, [0], kc_ref.shape[0]))

        def body(j, carry):
            base = pl.multiple_of(j * span, span)
            run_online(units_of(kl_ref, vlt_ref,
                                [base + u * tk for u in range(ATTN_KV_UNROLL_ONLINE)], tk))
            return carry

        lax.fori_loop(0, kl_ref.shape[0] // span, body, 0)
        l = jnp.sum(l_ref[...], axis=-1, keepdims=True)
        o_ref[...] = (acc_ref[...] / l).astype(o_ref.dtype)

    bounded = bound_ref[0] <= ATTN_EXP2_SAFE_RANGE
    pl.when(bounded)(bounded_path)
    pl.when(jnp.logical_not(bounded))(online_path)


def _attention_and_casts_kernel(n_cast, *refs):
    attn_in, rest = refs[:6], refs[6:]
    w_refs, rest = rest[:n_cast], rest[n_cast:]
    o_ref, rest = rest[0], rest[1:]
    wo_refs, scratch = rest[:n_cast], rest[n_cast:]

    def cast_weights():
        for w_ref, wo_ref in zip(w_refs, wo_refs):
            wo_ref[...] = w_ref[...].astype(wo_ref.dtype)

    _attention_kernel(*attn_in, o_ref, *scratch, cast_weights)


def _cast_plan(rows, steps):
    share = 1
    while steps % share == 0:
        blocks = steps // share
        if rows % blocks == 0 and (rows // blocks) % BF16_SUBLANES == 0:
            return blocks, share
        share *= 2
    raise ValueError(f"cannot split {rows} rows over {steps} grid steps")


def _attention(score_bound, q, k_ctx, vt_ctx, k_lat, vt_lat, weights):
    b, nh, s, dk = q.shape
    nc = k_ctx.shape[2]
    dv = vt_lat.shape[2]
    assert dv == LANES
    tq = ATTN_Q_TILE
    nq = s // tq
    steps = b * nh * nq

    def per_head(rows, cols):
        return pl.BlockSpec((None, None, rows, cols), lambda bi, hi, i: (bi, hi, 0, 0))

    w_in_specs, w_out_specs, w_shapes = [], [], []
    for w in weights:
        _, rows, cols = w.shape
        blocks, share = _cast_plan(rows, steps)

        def block_of(bi, hi, i, share=share):
            return ((bi * nh + hi) * nq + i) // share

        w_in_specs.append(pl.BlockSpec((None, rows // blocks, cols),
                                       lambda bi, hi, i, f=block_of: (0, f(bi, hi, i), 0)))
        w_out_specs.append(pl.BlockSpec((rows // blocks, cols),
                                        lambda bi, hi, i, f=block_of: (f(bi, hi, i), 0)))
        w_shapes.append(jax.ShapeDtypeStruct((rows, cols), BF16))

    attn, *w_bf16 = pl.pallas_call(
        functools.partial(_attention_and_casts_kernel, len(weights)),
        grid=(b, nh, nq),
        in_specs=[pl.BlockSpec(memory_space=pltpu.SMEM),
                  pl.BlockSpec((None, None, tq, dk), lambda bi, hi, i: (bi, hi, i, 0)),
                  per_head(nc, dk), per_head(dv, nc), per_head(s, dk), per_head(dv, s)]
        + w_in_specs,
        out_specs=[pl.BlockSpec((None, tq, dv), lambda bi, hi, i: (bi, i, hi))] + w_out_specs,
        out_shape=[jax.ShapeDtypeStruct((b, s, nh * dv), BF16)] + w_shapes,
        scratch_shapes=[pltpu.VMEM((SUBLANES, tq), F32), pltpu.VMEM((dv, tq), F32),
                        pltpu.VMEM((tq, LANES), F32), pltpu.VMEM((tq, LANES), F32),
                        pltpu.VMEM((tq, dv), F32)],
        compiler_params=pltpu.CompilerParams(
            dimension_semantics=("arbitrary", "arbitrary", "arbitrary"),
            vmem_limit_bytes=VMEM_LIMIT_BYTES),
        name="attention",
    )(score_bound, q, k_ctx, vt_ctx, k_lat, vt_lat, *weights)
    return attn, w_bf16


def _merge_kernel(x_ref, mod_ref, bx_ref, u_ref, u_prev_ref, u_next_ref, gc_ref, ga_ref,
                  attn_ref, convw_ref, convb_ref, w_conv_out_ref, w_attn_o_ref, w_out_ref,
                  o_ref):
    i = pl.program_id(1)
    tm, d = x_ref.shape
    y_attn = jnp.dot(attn_ref[...], w_attn_o_ref[...], preferred_element_type=F32)
    u = u_ref[...].astype(F32)
    prev_row = u_prev_ref[SUBLANES - 1:SUBLANES, :].astype(F32) * (i > 0).astype(F32)
    next_row = u_next_ref[0:1, :].astype(F32) * (i < pl.num_programs(1) - 1).astype(F32)
    row = lax.broadcasted_iota(jnp.int32, (tm, d), 0)
    u_m1 = jnp.where(row == 0, prev_row, pltpu.roll(u, 1, axis=0))
    u_p1 = jnp.where(row == tm - 1, next_row, pltpu.roll(u, tm - 1, axis=0))
    cw = convw_ref[...]
    conv = u_m1 * cw[0:1, :] + u * cw[1:2, :] + u_p1 * cw[2:3, :] + convb_ref[...]
    z = (bx_ref[...].astype(F32) * conv).astype(BF16)
    y_conv = jnp.dot(z, w_conv_out_ref[...], preferred_element_type=F32)
    merged = (gc_ref[...].astype(F32) * y_conv + ga_ref[...].astype(F32) * y_attn).astype(BF16)
    y = jnp.dot(merged, w_out_ref[...], preferred_element_type=F32)
    g1 = mod_ref[:, 2 * d:3 * d]
    o_ref[...] = x_ref[...] + g1 * y


def _merge(x, mod_x, bx, u, gc, ga, attn, conv_w, conv_b, w_conv_out, w_attn_o, w_out):
    b, s, d = x.shape
    tm = MERGE_TOKEN_TILE
    hb = tm // SUBLANES
    last_hb = s // SUBLANES - 1
    tok = pl.BlockSpec((None, tm, d), lambda bi, i: (bi, i, 0))
    prev = pl.BlockSpec((None, SUBLANES, d), lambda bi, i: (bi, jnp.maximum(i * hb - 1, 0), 0))
    nxt = pl.BlockSpec((None, SUBLANES, d),
                       lambda bi, i: (bi, jnp.minimum((i + 1) * hb, last_hb), 0))
    return pl.pallas_call(
        _merge_kernel,
        grid=(b, s // tm),
        in_specs=[tok,
                  pl.BlockSpec((None, 1, mod_x.shape[-1]), lambda bi, i: (bi, 0, 0)),
                  tok, tok, prev, nxt, tok, tok, tok,
                  _const_spec(conv_w.shape), _const_spec(conv_b.shape),
                  _const_spec(w_conv_out.shape), _const_spec(w_attn_o.shape),
                  _const_spec(w_out.shape)],
        out_specs=tok,
        out_shape=jax.ShapeDtypeStruct((b, s, d), F32),
        compiler_params=pltpu.CompilerParams(dimension_semantics=("arbitrary", "arbitrary"),
                                             vmem_limit_bytes=VMEM_LIMIT_BYTES),
        name="merge",
    )(x, mod_x, bx, u, u, u, gc, ga, attn, conv_w, conv_b, w_conv_out, w_attn_o, w_out)


def _ffn_kernel(x_ref, mod_ref, nffn_ref, w_in_ref, w_out_ref, o_ref, act_ref):
    d = x_ref.shape[-1]
    d_ff = w_out_ref.shape[0]
    x = x_ref[...]
    mod = mod_ref[...]
    h = _modulated_norm(x, nffn_ref[...], mod[:, 3 * d:4 * d], mod[:, 4 * d:5 * d]).astype(BF16)
    def gate_up(j):
        c0 = j * FFN_CHUNK
        return (jnp.dot(h, w_in_ref[:, c0:c0 + FFN_CHUNK], preferred_element_type=F32),
                jnp.dot(h, w_in_ref[:, d_ff + c0:d_ff + c0 + FFN_CHUNK],
                        preferred_element_type=F32))

    n_chunks = d_ff // FFN_CHUNK
    pending = gate_up(0)
    for j in range(n_chunks):
        gate, up = pending
        if j + 1 < n_chunks:
            pending = gate_up(j + 1)
        c0 = j * FFN_CHUNK
        act_ref[:, c0:c0 + FFN_CHUNK] = (gate * _sigmoid(gate) * up).astype(BF16)
    y = jnp.dot(act_ref[...], w_out_ref[...], preferred_element_type=F32)
    o_ref[...] = x + mod[:, 5 * d:6 * d] * y


def _ffn(x, mod_x, nffn, w_ffn_in, w_ffn_out):
    b, s, d = x.shape
    tm = TOKEN_TILE
    d_ff = w_ffn_out.shape[0]
    tok = pl.BlockSpec((None, tm, d), lambda bi, i: (bi, i, 0))
    return pl.pallas_call(
        _ffn_kernel,
        grid=(b, s // tm),
        in_specs=[tok,
                  pl.BlockSpec((None, 1, mod_x.shape[-1]), lambda bi, i: (bi, 0, 0)),
                  _const_spec(nffn.shape), _const_spec(w_ffn_in.shape),
                  _const_spec(w_ffn_out.shape)],
        out_specs=tok,
        out_shape=jax.ShapeDtypeStruct((b, s, d), F32),
        scratch_shapes=[pltpu.VMEM((tm, d_ff), BF16)],
        compiler_params=pltpu.CompilerParams(dimension_semantics=("arbitrary", "arbitrary"),
                                             vmem_limit_bytes=VMEM_LIMIT_BYTES),
        name="ffn",
    )(x, mod_x, nffn, w_ffn_in, w_ffn_out)


def _rope_table(seq_len):
    quarter = QK_ROPE_DIM // 4
    freqs = ROPE_THETA ** (-np.arange(quarter, dtype=np.float64) / quarter)
    t = np.arange(seq_len)
    ang = [(t // GRID_W)[:, None] * freqs, (t % GRID_W)[:, None] * freqs]
    cos = [np.cos(a) for a in ang]
    sin = [np.sin(a) for a in ang]
    c = np.concatenate([cos[0], cos[0], cos[1], cos[1]], axis=-1)
    s = np.concatenate([-sin[0], sin[0], -sin[1], sin[1]], axis=-1)
    return jnp.asarray(np.concatenate([c, s], axis=-1), F32)


def _swap_halves(t):
    q = QK_ROPE_DIM // 4
    return jnp.concatenate([t[..., q:2 * q], t[..., :q], t[..., 3 * q:], t[..., 2 * q:3 * q]],
                           axis=-1)


def _norm_gain_pair(g):
    rope = g[QK_NOPE_DIM:]
    return jnp.concatenate([g[:QK_NOPE_DIM], rope, _swap_halves(rope)])[None, :].astype(F32)


def kernel(x, c, ctx, c_ctx, w_mod, b_mod, norm_mix, norm_ffn, w_in, b_gate, conv_w, conv_b,
           w_conv_out, q_a_norm, w_q_b, kv_a_norm, w_kv_b, q_norm, k_norm, w_attn_o, w_out,
           w_ffn_in, w_ffn_out):
    depth = w_mod.shape[0]
    assert depth == 1, "context stream update is only needed between layers"
    b, s, d = x.shape
    n_ctx = ctx.shape[1]
    assert s % TOKEN_TILE == 0 and s % ATTN_Q_TILE == 0 and s % ATTN_KV_TILE == 0
    assert s % MERGE_TOKEN_TILE == 0
    assert s % (ATTN_KV_TILE_ONLINE * ATTN_KV_UNROLL_ONLINE) == 0
    assert b + 1 <= SUBLANES
    assert w_mod.shape[-1] == MOD_CHUNKS * d and w_ffn_out.shape[1] % FFN_CHUNK == 0

    cond = jnp.zeros((SUBLANES, d), F32).at[:b].set(c).at[b].set(c_ctx)
    mod = _adaln(cond, w_mod, b_mod)
    mod_x = mod[:b, None, :]
    mod_ctx = mod[b:b + 1]

    wi = w_in[0]
    o_q = 3 * d
    o_kv = o_q + Q_LORA_RANK
    o_kr = o_kv + KV_LORA_RANK
    o_gc = o_kr + QK_ROPE_DIM
    assert o_q % LANES == 0 and o_kv % LANES == 0
    w_all = wi.astype(BF16)
    w_gate = w_all[:, o_gc:]
    w_kr = w_all[:, o_kr:o_gc]
    w_kv_in = jnp.concatenate([w_all[:, o_kv:o_kr], w_kr, _swap_halves(w_kr)], axis=1)
    wq = w_q_b[0].reshape(Q_LORA_RANK, N_HEADS, QK_HEAD_DIM)
    wq = jnp.concatenate([wq, _swap_halves(wq[..., QK_NOPE_DIM:])], axis=-1)
    wqb = wq.reshape(Q_LORA_RANK, N_HEADS * 2 * LANES).astype(BF16)
    wkv = w_kv_b[0].reshape(KV_LORA_RANK, N_HEADS, QK_NOPE_DIM + V_HEAD_DIM)
    w_k = wkv[..., :QK_NOPE_DIM].reshape(KV_LORA_RANK, -1).astype(BF16)
    w_vt = wkv[..., QK_NOPE_DIM:].reshape(KV_LORA_RANK, -1).T.astype(BF16)
    gq = _norm_gain_pair(q_norm[0])
    gk = _norm_gain_pair(k_norm[0])
    table = _rope_table(s)
    table_ctx = jnp.concatenate([jnp.ones((n_ctx, QK_ROPE_DIM), F32),
                                 jnp.zeros((n_ctx, QK_ROPE_DIM), F32)], axis=1)
    nmix = norm_mix[0][None, :]
    kvan = kv_a_norm[0][None, :]

    k_ctx, vt_ctx = _ctx_kv(ctx, mod_ctx, nmix, w_kv_in, kvan, w_k, w_vt, gk, table_ctx)
    bx, u, gc, ga, q, k_lat, vt_lat = _in_proj(
        x, mod_x, nmix, w_all, w_gate, w_kv_in, b_gate[0][None, :], q_a_norm[0][None, :], wqb,
        kvan, w_k, w_vt, gq, gk, table)
    score_bound = (QK_HEAD_DIM * SM_SCALE_LOG2 * BF16_ROUNDING_MARGIN
                   * jnp.max(jnp.abs(q_norm[0])) * jnp.max(jnp.abs(k_norm[0]))).reshape(1)
    attn, (w_conv_out_b, w_attn_o_b, w_out_b, w_ffn_in_b, w_ffn_out_b) = _attention(
        score_bound.astype(F32), q, k_ctx, vt_ctx, k_lat, vt_lat,
        [w_conv_out, w_attn_o, w_out, w_ffn_in, w_ffn_out])
    x_mid = _merge(x, mod_x, bx, u, gc, ga, attn, conv_w[0], conv_b[0][None, :],
                   w_conv_out_b, w_attn_o_b, w_out_b)
    return _ffn(x_mid, mod_x, norm_ffn[0][None, :], w_ffn_in_b, w_ffn_out_b)
```

```python
import functools

import jax
import jax.numpy as jnp
import numpy as np
from jax import lax
from jax.experimental import pallas as pl
from jax.experimental.pallas import tpu as pltpu

F32 = jnp.float32
BF16 = jnp.bfloat16

N_HEADS = 8
QK_NOPE_DIM = 128
QK_ROPE_DIM = 64
QK_HEAD_DIM = QK_NOPE_DIM + QK_ROPE_DIM
V_HEAD_DIM = 128
Q_LORA_RANK = 384
KV_LORA_RANK = 256
GRID_W = 64
ROPE_THETA = 10000.0
NORM_EPS = 1e-6
MOD_CHUNKS = 6
SM_SCALE_LOG2 = float(QK_HEAD_DIM ** -0.5 * np.log2(np.e))
ATTN_EXP2_SAFE_RANGE = 40.0
BF16_ROUNDING_MARGIN = 1.01

LANES = 128
SUBLANES = 8
BF16_SUBLANES = 16
CONTRACT_LAST = (((1,), (1,)), ((), ()))
VMEM_LIMIT_BYTES = 56 * 1024 * 1024

TOKEN_TILE = 512
MERGE_TOKEN_TILE = 1024
ATTN_Q_TILE = 2048
ATTN_Q_SUB = 256
ATTN_KV_TILE = 1024
ATTN_KV_TILE_ONLINE = 512
ATTN_KV_UNROLL_ONLINE = 2
ATTN_LOOKAHEAD = 3
ADALN_N_TILE = 1536
FFN_CHUNK = 256


def _rsqrt_mean(ss, n):
    return lax.rsqrt(ss * (1.0 / n) + NORM_EPS)


def _sigmoid(t):
    return 0.5 * jnp.tanh(0.5 * t) + 0.5


def _const_spec(shape):
    nd = len(shape)
    return pl.BlockSpec(shape, lambda *_: (0,) * nd, pipeline_mode=pl.Buffered(1))


def _adaln_kernel(cond_ref, w_ref, b_ref, o_ref):
    c = cond_ref[...]
    a = c * _sigmoid(c)
    o_ref[...] = jnp.dot(a, w_ref[...], preferred_element_type=F32,
                         precision=lax.Precision.HIGHEST) + b_ref[...]


def _adaln(cond, w_mod, b_mod):
    rows, d = cond.shape
    n = w_mod.shape[-1]
    return pl.pallas_call(
        _adaln_kernel,
        grid=(n // ADALN_N_TILE,),
        in_specs=[pl.BlockSpec((rows, d), lambda j: (0, 0)),
                  pl.BlockSpec((None, d, ADALN_N_TILE), lambda j: (0, 0, j)),
                  pl.BlockSpec((1, ADALN_N_TILE), lambda j: (0, j))],
        out_specs=pl.BlockSpec((rows, ADALN_N_TILE), lambda j: (0, j)),
        out_shape=jax.ShapeDtypeStruct((rows, n), F32),
        compiler_params=pltpu.CompilerParams(dimension_semantics=("arbitrary",),
                                             vmem_limit_bytes=VMEM_LIMIT_BYTES),
        name="adaln",
    )(cond, w_mod, b_mod)


def _modulated_norm(x, gain, shift, scale):
    d = x.shape[-1]
    r = _rsqrt_mean(jnp.sum(x * x, axis=-1, keepdims=True), d)
    return (x * r) * (gain * (1.0 + scale)) + shift


def _rope_pair(t2, table):
    w = t2 * table
    return w + pltpu.roll(w, QK_ROPE_DIM, axis=1)


def _kv_down(h, w_kv_in_ref):
    return jnp.dot(h, w_kv_in_ref[...], preferred_element_type=F32)


def _kv_up(kvp, kvan_ref, w_k_ref, w_vt_ref):
    kv_a = kvp[:, :KV_LORA_RANK]
    r = _rsqrt_mean(jnp.sum(kv_a * kv_a, axis=-1, keepdims=True), KV_LORA_RANK)
    kv_n = (kv_a * r * kvan_ref[...]).astype(BF16)
    k_nope = jnp.dot(kv_n, w_k_ref[...], preferred_element_type=F32)
    v_t = lax.dot_general(w_vt_ref[...], kv_n, CONTRACT_LAST, preferred_element_type=F32)
    return k_nope, v_t, kvp[:, KV_LORA_RANK:]


def _kv_finish(kv, v_t, kr2, gk_ref, table, k_ref, vt_ref):
    gk = gk_ref[...]
    ss_rope = 0.5 * jnp.sum(kr2 * kr2, axis=-1, keepdims=True)
    kr = _rope_pair(kr2 * gk[:, QK_NOPE_DIM:], table)[:, :QK_ROPE_DIM]
    for hd in range(N_HEADS):
        k_nope = kv[:, hd * QK_NOPE_DIM:(hd + 1) * QK_NOPE_DIM]
        ss = jnp.sum(k_nope * k_nope, axis=-1, keepdims=True) + ss_rope
        rh = _rsqrt_mean(ss, QK_HEAD_DIM)
        k_ref[hd, :, :QK_NOPE_DIM] = (k_nope * rh * gk[:, :QK_NOPE_DIM]).astype(BF16)
        k_ref[hd, :, QK_NOPE_DIM:] = (kr * rh).astype(BF16)
        vt_ref[hd] = v_t[hd * V_HEAD_DIM:(hd + 1) * V_HEAD_DIM, :].astype(BF16)


def _ctx_kv_kernel(ctx_ref, mod_ref, nmix_ref, w_kv_in_ref, kvan_ref, w_k_ref, w_vt_ref, gk_ref,
                   table_ref, k_ref, vt_ref):
    d = ctx_ref.shape[-1]
    mod = mod_ref[...]
    h = _modulated_norm(ctx_ref[...], nmix_ref[...], mod[:, :d], mod[:, d:2 * d]).astype(BF16)
    k_nope, v_t, kr2 = _kv_up(_kv_down(h, w_kv_in_ref), kvan_ref, w_k_ref, w_vt_ref)
    _kv_finish(k_nope, v_t, kr2, gk_ref, table_ref[...], k_ref, vt_ref)


def _ctx_kv(ctx, mod_ctx, nmix, w_kv_in, kvan, w_k, w_vt, gk, table):
    b, n, d = ctx.shape
    return pl.pallas_call(
        _ctx_kv_kernel,
        grid=(b,),
        in_specs=[pl.BlockSpec((None, n, d), lambda i: (i, 0, 0)),
                  _const_spec(mod_ctx.shape), _const_spec(nmix.shape),
                  _const_spec(w_kv_in.shape), _const_spec(kvan.shape),
                  _const_spec(w_k.shape), _const_spec(w_vt.shape), _const_spec(gk.shape),
                  _const_spec(table.shape)],
        out_specs=[pl.BlockSpec((None, N_HEADS, n, QK_HEAD_DIM), lambda i: (i, 0, 0, 0)),
                   pl.BlockSpec((None, N_HEADS, V_HEAD_DIM, n), lambda i: (i, 0, 0, 0))],
        out_shape=[jax.ShapeDtypeStruct((b, N_HEADS, n, QK_HEAD_DIM), BF16),
                   jax.ShapeDtypeStruct((b, N_HEADS, V_HEAD_DIM, n), BF16)],
        compiler_params=pltpu.CompilerParams(dimension_semantics=("arbitrary",),
                                             vmem_limit_bytes=VMEM_LIMIT_BYTES),
        name="ctx_kv",
    )(ctx, mod_ctx, nmix, w_kv_in, kvan, w_k, w_vt, gk, table)


def _in_proj_kernel(x_ref, mod_ref, nmix_ref, w_all_ref, w_gate_ref, w_kv_in_ref, bgate_ref,
                    qan_ref, wqb_ref, kvan_ref, w_k_ref, w_vt_ref, gq_ref, gk_ref, table_ref,
                    bx_ref, u_ref, gc_ref, ga_ref, q_ref, k_ref, vt_ref):
    d = x_ref.shape[-1]
    mod = mod_ref[...]
    h = _modulated_norm(x_ref[...], nmix_ref[...], mod[:, :d], mod[:, d:2 * d]).astype(BF16)
    table = table_ref[...]

    def proj(w_ref, j):
        return jnp.dot(h, w_ref[:, j * d:(j + 1) * d], preferred_element_type=F32)

    q_a = jnp.dot(h, w_all_ref[:, 3 * d:3 * d + Q_LORA_RANK], preferred_element_type=F32)
    kvp = _kv_down(h, w_kv_in_ref)
    bgate = bgate_ref[...]
    gc_ref[...] = _sigmoid(proj(w_gate_ref, 0) + bgate[:, :d]).astype(BF16)
    r = _rsqrt_mean(jnp.sum(q_a * q_a, axis=-1, keepdims=True), Q_LORA_RANK)
    q_n = (q_a * r * qan_ref[...]).astype(BF16)
    q = jnp.dot(q_n, wqb_ref[...], preferred_element_type=F32)
    k_nope, v_t, kr2 = _kv_up(kvp, kvan_ref, w_k_ref, w_vt_ref)
    ga_ref[...] = _sigmoid(proj(w_gate_ref, 1) + bgate[:, d:]).astype(BF16)

    gq = gq_ref[...]
    table_q = table * gq[:, QK_NOPE_DIM:]
    sm_scale = SM_SCALE_LOG2
    hw = 2 * LANES
    for hd in range(N_HEADS):
        q_nope = q[:, hd * hw:hd * hw + QK_NOPE_DIM]
        q_r2 = q[:, hd * hw + QK_NOPE_DIM:(hd + 1) * hw]
        ss = jnp.sum(q_nope * q_nope + 0.5 * (q_r2 * q_r2), axis=-1, keepdims=True)
        rh = _rsqrt_mean(ss, QK_HEAD_DIM) * sm_scale
        q_ref[hd, :, :QK_NOPE_DIM] = (q_nope * rh * gq[:, :QK_NOPE_DIM]).astype(BF16)
        q_ref[hd, :, QK_NOPE_DIM:] = (_rope_pair(q_r2 * rh, table_q)[:, :QK_ROPE_DIM]).astype(BF16)

    u_ref[...] = (proj(w_all_ref, 1) * proj(w_all_ref, 2)).astype(BF16)
    _kv_finish(k_nope, v_t, kr2, gk_ref, table, k_ref, vt_ref)
    bx_ref[...] = proj(w_all_ref, 0).astype(BF16)


def _in_proj(x, mod_x, nmix, w_all, w_gate, w_kv_in, bgate, qan, wqb, kvan, w_k, w_vt, gq, gk,
             table):
    b, s, d = x.shape
    tm = TOKEN_TILE
    tok = pl.BlockSpec((None, tm, d), lambda bi, i: (bi, i, 0))

    def head_spec(width):
        return pl.BlockSpec((None, N_HEADS, tm, width), lambda bi, i: (bi, 0, i, 0))

    tok_shape = jax.ShapeDtypeStruct((b, s, d), BF16)
    return pl.pallas_call(
        _in_proj_kernel,
        grid=(b, s // tm),
        in_specs=[tok,
                  pl.BlockSpec((None, 1, mod_x.shape[-1]), lambda bi, i: (bi, 0, 0)),
                  _const_spec(nmix.shape),
                  _const_spec((d, 3 * d + Q_LORA_RANK)), _const_spec(w_gate.shape),
                  _const_spec(w_kv_in.shape), _const_spec(bgate.shape), _const_spec(qan.shape),
                  _const_spec(wqb.shape), _const_spec(kvan.shape), _const_spec(w_k.shape),
                  _const_spec(w_vt.shape), _const_spec(gq.shape), _const_spec(gk.shape),
                  pl.BlockSpec((tm, 2 * QK_ROPE_DIM), lambda bi, i: (i, 0))],
        out_specs=[tok, tok, tok, tok,
                   head_spec(QK_HEAD_DIM), head_spec(QK_HEAD_DIM),
                   pl.BlockSpec((None, N_HEADS, V_HEAD_DIM, tm), lambda bi, i: (bi, 0, 0, i))],
        out_shape=[tok_shape, tok_shape, tok_shape, tok_shape,
                   jax.ShapeDtypeStruct((b, N_HEADS, s, QK_HEAD_DIM), BF16),
                   jax.ShapeDtypeStruct((b, N_HEADS, s, QK_HEAD_DIM), BF16),
                   jax.ShapeDtypeStruct((b, N_HEADS, V_HEAD_DIM, s), BF16)],
        compiler_params=pltpu.CompilerParams(dimension_semantics=("arbitrary", "arbitrary"),
                                             vmem_limit_bytes=VMEM_LIMIT_BYTES),
        name="in_proj",
    )(x, mod_x, nmix, w_all, w_gate, w_kv_in, bgate, qan, wqb, kvan, w_k, w_vt, gq, gk, table)


def _attention_kernel(bound_ref, q_ref, kc_ref, vct_ref, kl_ref, vlt_ref, o_ref,
                      lt_ref, acct_ref, m_ref, l_ref, acc_ref, side_work):
    tq = q_ref.shape[0]

    def scores_t(unit):
        keys, _ = unit
        return lax.dot_general(keys(), q_ref[...], CONTRACT_LAST,
                               preferred_element_type=F32)

    def finish_bounded(unit, st):
        _, values_t = unit
        pt = jnp.exp2(st)
        lt_ref[...] += pt.reshape(pt.shape[0] // SUBLANES, SUBLANES, tq).sum(axis=0)
        acct_ref[...] += jnp.dot(values_t(), pt.astype(BF16),
                                 preferred_element_type=F32)

    def bounded_path():
        side_work()
        lt_ref[...] = jnp.zeros(lt_ref.shape, F32)
        acct_ref[...] = jnp.zeros(acct_ref.shape, F32)
        tk = ATTN_KV_TILE
        units = [(lambda: jnp.concatenate([kc_ref[...], kl_ref[0:tk, :]], axis=0),
                  lambda: jnp.concatenate([vct_ref[...], vlt_ref[:, 0:tk]], axis=1))]
        units += [(lambda k0=k0: kl_ref[k0:k0 + tk, :], lambda k0=k0: vlt_ref[:, k0:k0 + tk])
                  for k0 in range(tk, kl_ref.shape[0], tk)]
        pending = [scores_t(u) for u in units[:ATTN_LOOKAHEAD]]
        for i, unit in enumerate(units):
            if i + ATTN_LOOKAHEAD < len(units):
                pending.append(scores_t(units[i + ATTN_LOOKAHEAD]))
            finish_bounded(unit, pending[i])
        l = jnp.sum(lt_ref[...], axis=0, keepdims=True)
        o_ref[...] = (acct_ref[...] / l).T.astype(o_ref.dtype)

    def scores(unit):
        k_ref, _, k0, tk, r0 = unit
        return lax.dot_general(q_ref[r0:r0 + ATTN_Q_SUB, :], k_ref[pl.ds(k0, tk), :],
                               CONTRACT_LAST, preferred_element_type=F32)

    def finish_online(unit, s):
        _, vt_ref, k0, tk, r0 = unit
        rows = slice(r0, r0 + ATTN_Q_SUB)
        cols = [s[:, c:c + LANES] for c in range(0, tk, LANES)]
        m_old = m_ref[rows, :]
        row_max = jnp.max(functools.reduce(jnp.maximum, cols), axis=-1, keepdims=True)
        m_new = jnp.maximum(m_old, row_max)
        alpha = jnp.exp2(m_old - m_new)
        ps = [jnp.exp2(col - m_new) for col in cols]
        l_ref[rows, :] = alpha * l_ref[rows, :] + functools.reduce(jnp.add, ps)
        p = jnp.concatenate([x.astype(BF16) for x in ps], axis=1)
        pv = lax.dot_general(p, vt_ref[:, pl.ds(k0, tk)], CONTRACT_LAST,
                             preferred_element_type=F32)
        acc_ref[rows, :] = alpha * acc_ref[rows, :] + pv
        m_ref[rows, :] = m_new

    def run_online(units):
        pending = [scores(u) for u in units[:ATTN_LOOKAHEAD]]
        for i, unit in enumerate(units):
            if i + ATTN_LOOKAHEAD < len(units):
                pending.append(scores(units[i + ATTN_LOOKAHEAD]))
            finish_online(unit, pending[i])

    def units_of(k_ref, vt_ref, starts, tk):
        return [(k_ref, vt_ref, k0, tk, r0) for k0 in starts for r0 in range(0, tq, ATTN_Q_SUB)]

    def online_path():
        side_work()
        m_ref[...] = jnp.full(m_ref.shape, -jnp.inf, F32)
        l_ref[...] = jnp.zeros(l_ref.shape, F32)
        acc_ref[...] = jnp.zeros(acc_ref.shape, F32)
        tk = ATTN_KV_TILE_ONLINE
        span = tk * ATTN_KV_UNROLL_ONLINE
        run_online(units_of(kc_ref, vct_ref, [0], kc_ref.shape[0]))

        def body(j, carry):
            base = pl.multiple_of(j * span, span)
            run_online(units_of(kl_ref, vlt_ref,
                                [base + u * tk for u in range(ATTN_KV_UNROLL_ONLINE)], tk))
            return carry

        lax.fori_loop(0, kl_ref.shape[0] // span, body, 0)
        l = jnp.sum(l_ref[...], axis=-1, keepdims=True)
        o_ref[...] = (acc_ref[...] / l).astype(o_ref.dtype)

    bounded = bound_ref[0] <= ATTN_EXP2_SAFE_RANGE
    pl.when(bounded)(bounded_path)
    pl.when(jnp.logical_not(bounded))(online_path)


def _attention_and_casts_kernel(n_cast, *refs):
    attn_in, rest = refs[:6], refs[6:]
    w_refs, rest = rest[:n_cast], rest[n_cast:]
    o_ref, rest = rest[0], rest[1:]
    wo_refs, scratch = rest[:n_cast], rest[n_cast:]

    def cast_weights():
        for w_ref, wo_ref in zip(w_refs, wo_refs):
            wo_ref[...] = w_ref[...].astype(wo_ref.dtype)

    _attention_kernel(*attn_in, o_ref, *scratch, cast_weights)


def _cast_plan(rows, steps):
    share = 1
    while steps % share == 0:
        blocks = steps // share
        if rows % blocks == 0 and (rows // blocks) % BF16_SUBLANES == 0:
            return blocks, share
        share *= 2
    raise ValueError(f"cannot split {rows} rows over {steps} grid steps")


def _attention(score_bound, q, k_ctx, vt_ctx, k_lat, vt_lat, weights):
    b, nh, s, dk = q.shape
    nc = k_ctx.shape[2]
    dv = vt_lat.shape[2]
    assert dv == LANES
    tq = ATTN_Q_TILE
    nq = s // tq
    steps = b * nh * nq

    def per_head(rows, cols):
        return pl.BlockSpec((None, None, rows, cols), lambda bi, hi, i: (bi, hi, 0, 0))

    w_in_specs, w_out_specs, w_shapes = [], [], []
    for w in weights:
        _, rows, cols = w.shape
        blocks, share = _cast_plan(rows, steps)

        def block_of(bi, hi, i, share=share):
            return ((bi * nh + hi) * nq + i) // share

        w_in_specs.append(pl.BlockSpec((None, rows // blocks, cols),
                                       lambda bi, hi, i, f=block_of: (0, f(bi, hi, i), 0)))
        w_out_specs.append(pl.BlockSpec((rows // blocks, cols),
                                        lambda bi, hi, i, f=block_of: (f(bi, hi, i), 0)))
        w_shapes.append(jax.ShapeDtypeStruct((rows, cols), BF16))

    attn, *w_bf16 = pl.pallas_call(
        functools.partial(_attention_and_casts_kernel, len(weights)),
        grid=(b, nh, nq),
        in_specs=[pl.BlockSpec(memory_space=pltpu.SMEM),
                  pl.BlockSpec((None, None, tq, dk), lambda bi, hi, i: (bi, hi, i, 0)),
                  per_head(nc, dk), per_head(dv, nc), per_head(s, dk), per_head(dv, s)]
        + w_in_specs,
        out_specs=[pl.BlockSpec((None, tq, dv), lambda bi, hi, i: (bi, i, hi))] + w_out_specs,
        out_shape=[jax.ShapeDtypeStruct((b, s, nh * dv), BF16)] + w_shapes,
        scratch_shapes=[pltpu.VMEM((SUBLANES, tq), F32), pltpu.VMEM((dv, tq), F32),
                        pltpu.VMEM((tq, LANES), F32), pltpu.VMEM((tq, LANES), F32),
                        pltpu.VMEM((tq, dv), F32)],
        compiler_params=pltpu.CompilerParams(
            dimension_semantics=("arbitrary", "arbitrary", "arbitrary"),
            vmem_limit_bytes=VMEM_LIMIT_BYTES),
        name="attention",
    )(score_bound, q, k_ctx, vt_ctx, k_lat, vt_lat, *weights)
    return attn, w_bf16


def _merge_kernel(x_ref, mod_ref, bx_ref, u_ref, u_prev_ref, u_next_ref, gc_ref, ga_ref,
                  attn_ref, convw_ref, convb_ref, w_conv_out_ref, w_attn_o_ref, w_out_ref,
                  o_ref):
    i = pl.program_id(1)
    tm, d = x_ref.shape
    y_attn = jnp.dot(attn_ref[...], w_attn_o_ref[...], preferred_element_type=F32)
    u = u_ref[...].astype(F32)
    prev_row = u_prev_ref[SUBLANES - 1:SUBLANES, :].astype(F32) * (i > 0).astype(F32)
    next_row = u_next_ref[0:1, :].astype(F32) * (i < pl.num_programs(1) - 1).astype(F32)
    row = lax.broadcasted_iota(jnp.int32, (tm, d), 0)
    u_m1 = jnp.where(row == 0, prev_row, pltpu.roll(u, 1, axis=0))
    u_p1 = jnp.where(row == tm - 1, next_row, pltpu.roll(u, tm - 1, axis=0))
    cw = convw_ref[...]
    conv = u_m1 * cw[0:1, :] + u * cw[1:2, :] + u_p1 * cw[2:3, :] + convb_ref[...]
    z = (bx_ref[...].astype(F32) * conv).astype(BF16)
    y_conv = jnp.dot(z, w_conv_out_ref[...], preferred_element_type=F32)
    merged = (gc_ref[...].astype(F32) * y_conv + ga_ref[...].astype(F32) * y_attn).astype(BF16)
    y = jnp.dot(merged, w_out_ref[...], preferred_element_type=F32)
    g1 = mod_ref[:, 2 * d:3 * d]
    o_ref[...] = x_ref[...] + g1 * y


def _merge(x, mod_x, bx, u, gc, ga, attn, conv_w, conv_b, w_conv_out, w_attn_o, w_out):
    b, s, d = x.shape
    tm = MERGE_TOKEN_TILE
    hb = tm // SUBLANES
    last_hb = s // SUBLANES - 1
    tok = pl.BlockSpec((None, tm, d), lambda bi, i: (bi, i, 0))
    prev = pl.BlockSpec((None, SUBLANES, d), lambda bi, i: (bi, jnp.maximum(i * hb - 1, 0), 0))
    nxt = pl.BlockSpec((None, SUBLANES, d),
                       lambda bi, i: (bi, jnp.minimum((i + 1) * hb, last_hb), 0))
    return pl.pallas_call(
        _merge_kernel,
        grid=(b, s // tm),
        in_specs=[tok,
                  pl.BlockSpec((None, 1, mod_x.shape[-1]), lambda bi, i: (bi, 0, 0)),
                  tok, tok, prev, nxt, tok, tok, tok,
                  _const_spec(conv_w.shape), _const_spec(conv_b.shape),
                  _const_spec(w_conv_out.shape), _const_spec(w_attn_o.shape),
                  _const_spec(w_out.shape)],
        out_specs=tok,
        out_shape=jax.ShapeDtypeStruct((b, s, d), F32),
        compiler_params=pltpu.CompilerParams(dimension_semantics=("arbitrary", "arbitrary"),
                                             vmem_limit_bytes=VMEM_LIMIT_BYTES),
        name="merge",
    )(x, mod_x, bx, u, u, u, gc, ga, attn, conv_w, conv_b, w_conv_out, w_attn_o, w_out)


def _ffn_kernel(x_ref, mod_ref, nffn_ref, w_in_ref, w_out_ref, o_ref, act_ref):
    d = x_ref.shape[-1]
    d_ff = w_out_ref.shape[0]
    x = x_ref[...]
    mod = mod_ref[...]
    h = _modulated_norm(x, nffn_ref[...], mod[:, 3 * d:4 * d], mod[:, 4 * d:5 * d]).astype(BF16)
    def gate_up(j):
        c0 = j * FFN_CHUNK
        return (jnp.dot(h, w_in_ref[:, c0:c0 + FFN_CHUNK], preferred_element_type=F32),
                jnp.dot(h, w_in_ref[:, d_ff + c0:d_ff + c0 + FFN_CHUNK],
                        preferred_element_type=F32))

    n_chunks = d_ff // FFN_CHUNK
    pending = gate_up(0)
    for j in range(n_chunks):
        gate, up = pending
        if j + 1 < n_chunks:
            pending = gate_up(j + 1)
        c0 = j * FFN_CHUNK
        act_ref[:, c0:c0 + FFN_CHUNK] = (gate * _sigmoid(gate) * up).astype(BF16)
    y = jnp.dot(act_ref[...], w_out_ref[...], preferred_element_type=F32)
    o_ref[...] = x + mod[:, 5 * d:6 * d] * y


def _ffn(x, mod_x, nffn, w_ffn_in, w_ffn_out):
    b, s, d = x.shape
    tm = TOKEN_TILE
    d_ff = w_ffn_out.shape[0]
    tok = pl.BlockSpec((None, tm, d), lambda bi, i: (bi, i, 0))
    return pl.pallas_call(
        _ffn_kernel,
        grid=(b, s // tm),
        in_specs=[tok,
                  pl.BlockSpec((None, 1, mod_x.shape[-1]), lambda bi, i: (bi, 0, 0)),
                  _const_spec(nffn.shape), _const_spec(w_ffn_in.shape),
                  _const_spec(w_ffn_out.shape)],
        out_specs=tok,
        out_shape=jax.ShapeDtypeStruct((b, s, d), F32),
        scratch_shapes=[pltpu.VMEM((tm, d_ff), BF16)],
        compiler_params=pltpu.CompilerParams(dimension_semantics=("arbitrary", "arbitrary"),
                                             vmem_limit_bytes=VMEM_LIMIT_BYTES),
        name="ffn",
    )(x, mod_x, nffn, w_ffn_in, w_ffn_out)


def _rope_table(seq_len):
    quarter = QK_ROPE_DIM // 4
    freqs = ROPE_THETA ** (-np.arange(quarter, dtype=np.float64) / quarter)
    t = np.arange(seq_len)
    ang = [(t // GRID_W)[:, None] * freqs, (t % GRID_W)[:, None] * freqs]
    cos = [np.cos(a) for a in ang]
    sin = [np.sin(a) for a in ang]
    c = np.concatenate([cos[0], cos[0], cos[1], cos[1]], axis=-1)
    s = np.concatenate([-sin[0], sin[0], -sin[1], sin[1]], axis=-1)
    return jnp.asarray(np.concatenate([c, s], axis=-1), F32)


def _swap_halves(t):
    q = QK_ROPE_DIM // 4
    return jnp.concatenate([t[..., q:2 * q], t[..., :q], t[..., 3 * q:], t[..., 2 * q:3 * q]],
                           axis=-1)


def _norm_gain_pair(g):
    rope = g[QK_NOPE_DIM:]
    return jnp.concatenate([g[:QK_NOPE_DIM], rope, _swap_halves(rope)])[None, :].astype(F32)


def kernel(x, c, ctx, c_ctx, w_mod, b_mod, norm_mix, norm_ffn, w_in, b_gate, conv_w, conv_b,
           w_conv_out, q_a_norm, w_q_b, kv_a_norm, w_kv_b, q_norm, k_norm, w_attn_o, w_out,
           w_ffn_in, w_ffn_out):
    depth = w_mod.shape[0]
    assert depth == 1, "context stream update is only needed between layers"
    b, s, d = x.shape
    n_ctx = ctx.shape[1]
    assert s % TOKEN_TILE == 0 and s % ATTN_Q_TILE == 0 and s % ATTN_KV_TILE == 0
    assert s % MERGE_TOKEN_TILE == 0
    assert s % (ATTN_KV_TILE_ONLINE * ATTN_KV_UNROLL_ONLINE) == 0
    assert b + 1 <= SUBLANES
    assert w_mod.shape[-1] == MOD_CHUNKS * d and w_ffn_out.shape[1] % FFN_CHUNK == 0

    cond = jnp.zeros((SUBLANES, d), F32).at[:b].set(c).at[b].set(c_ctx)
    mod = _adaln(cond, w_mod, b_mod)
    mod_x = mod[:b, None, :]
    mod_ctx = mod[b:b + 1]

    wi = w_in[0]
    o_q = 3 * d
    o_kv = o_q + Q_LORA_RANK
    o_kr = o_kv + KV_LORA_RANK
    o_gc = o_kr + QK_ROPE_DIM
    assert o_q % LANES == 0 and o_kv % LANES == 0
    w_all = wi.astype(BF16)
    w_gate = w_all[:, o_gc:]
    w_kr = w_all[:, o_kr:o_gc]
    w_kv_in = jnp.concatenate([w_all[:, o_kv:o_kr], w_kr, _swap_halves(w_kr)], axis=1)
    wq = w_q_b[0].reshape(Q_LORA_RANK, N_HEADS, QK_HEAD_DIM)
    wq = jnp.concatenate([wq, _swap_halves(wq[..., QK_NOPE_DIM:])], axis=-1)
    wqb = wq.reshape(Q_LORA_RANK, N_HEADS * 2 * LANES).astype(BF16)
    wkv = w_kv_b[0].reshape(KV_LORA_RANK, N_HEADS, QK_NOPE_DIM + V_HEAD_DIM)
    w_k = wkv[..., :QK_NOPE_DIM].reshape(KV_LORA_RANK, -1).astype(BF16)
    w_vt = wkv[..., QK_NOPE_DIM:].reshape(KV_LORA_RANK, -1).T.astype(BF16)
    gq = _norm_gain_pair(q_norm[0])
    gk = _norm_gain_pair(k_norm[0])
    table = _rope_table(s)
    table_ctx = jnp.concatenate([jnp.ones((n_ctx, QK_ROPE_DIM), F32),
                                 jnp.zeros((n_ctx, QK_ROPE_DIM), F32)], axis=1)
    nmix = norm_mix[0][None, :]
    kvan = kv_a_norm[0][None, :]

    k_ctx, vt_ctx = _ctx_kv(ctx, mod_ctx, nmix, w_kv_in, kvan, w_k, w_vt, gk, table_ctx)
    bx, u, gc, ga, q, k_lat, vt_lat = _in_proj(
        x, mod_x, nmix, w_all, w_gate, w_kv_in, b_gate[0][None, :], q_a_norm[0][None, :], wqb,
        kvan, w_k, w_vt, gq, gk, table)
    score_bound = (QK_HEAD_DIM * SM_SCALE_LOG2 * BF16_ROUNDING_MARGIN
                   * jnp.max(jnp.abs(q_norm[0])) * jnp.max(jnp.abs(k_norm[0]))).reshape(1)
    attn, (w_conv_out_b, w_attn_o_b, w_out_b, w_ffn_in_b, w_ffn_out_b) = _attention(
        score_bound.astype(F32), q, k_ctx, vt_ctx, k_lat, vt_lat,
        [w_conv_out, w_attn_o, w_out, w_ffn_in, w_ffn_out])
    x_mid = _merge(x, mod_x, bx, u, gc, ga, attn, conv_w[0], conv_b[0][None, :],
                   w_conv_out_b, w_attn_o_b, w_out_b)
    return _ffn(x_mid, mod_x, norm_ffn[0][None, :], w_ffn_in_b, w_ffn_out_b)
```
